```python
import jax, jax.numpy as jnp
from jax import lax
import numpy as np

D_MODEL = 2048
BATCH = 16
SEQ = 2048
DEPTH = 4
DEC_BATCH = 16
DEC_SEQ = 64
PAST_LEN = 2048

CHUNK = 64
N_A_LAYERS = DEPTH // 2
N_B_LAYERS = DEPTH - N_A_LAYERS
GLA_HEADS = 4
GLA_DK = D_MODEL // 2
GLA_DV = D_MODEL
GLA_HK = GLA_DK // GLA_HEADS
GLA_HV = GLA_DV // GLA_HEADS
GLA_GATE_RANK = 16
GLA_GATE_NORM = 16.0
GLA_SUB = 16
GLA_SPLITS = (GLA_DK, 2 * GLA_DK, 2 * GLA_DK + GLA_DV, 2 * GLA_DK + 2 * GLA_DV)
GLA_IN = 2 * GLA_DK + 2 * GLA_DV + GLA_GATE_RANK
ATT_HEADS = 16
ATT_HD = D_MODEL // ATT_HEADS
LEFT_CHUNKS = 8
BAND = (LEFT_CHUNKS + 1) * CHUNK
PAST_WINDOW = LEFT_CHUNKS * CHUNK
MAX_REL = 128
N_REL = 2 * MAX_REL + 1
D_FF = 4 * D_MODEL
EPS = 1e-6
NEG_INF = -1e30

kernel_name = 'yoco_gla_chunkband_streaming_step'


def _rmsnorm(x, g):
    xf = x.astype(jnp.float32)
    y = xf * lax.rsqrt(jnp.mean(xf * xf, axis=-1, keepdims=True) + EPS)
    return (y * g.astype(jnp.float32)).astype(x.dtype)


def _sq_relu_mlp(h, w1, w2):
    return jnp.square(jax.nn.relu(h @ w1)) @ w2


def _gla_chunk_step(S, inp):
    q, k, v, g = inp
    B, H, C, dk = q.shape
    L = C // GLA_SUB
    b = jnp.cumsum(g, axis=2)
    b_end = b[:, :, -1]
    o_inter = jnp.einsum('bhtk,bhkv->bhtv', q * jnp.exp(b), S)
    qs = q.reshape(B, H, L, GLA_SUB, dk)
    ks = k.reshape(B, H, L, GLA_SUB, dk)
    bs = b.reshape(B, H, L, GLA_SUB, dk)
    ref = jnp.concatenate([jnp.zeros_like(bs[:, :, :1, 0]), bs[:, :, :-1, -1]], axis=2)
    q_ref = qs * jnp.exp(bs - ref[:, :, :, None])
    k_ref = k[:, :, None] * jnp.exp(jnp.minimum(ref[:, :, :, None] - b[:, :, None], 0.0))
    a_off = jnp.einsum('bhlak,bhlsk->bhlas', q_ref, k_ref).reshape(B, H, C, C)
    blk = jnp.arange(C) // GLA_SUB
    a_off = jnp.where(blk[None, :] < blk[:, None], a_off, 0.0)
    decay = jnp.exp(jnp.minimum(bs[:, :, :, :, None] - bs[:, :, :, None, :], 0.0))
    a_diag = jnp.sum(qs[:, :, :, :, None] * ks[:, :, :, None] * decay, axis=-1)
    tri = jnp.tril(jnp.ones((GLA_SUB, GLA_SUB), dtype=bool))
    a_diag = jnp.where(tri, a_diag, 0.0)
    a_diag = jnp.einsum('bhlae,lm->bhlame', a_diag, jnp.eye(L, dtype=a_diag.dtype)).reshape(B, H, C, C)
    o = o_inter + jnp.einsum('bhts,bhsv->bhtv', a_off + a_diag, v)
    S_new = jnp.exp(b_end)[..., None] * S + jnp.einsum('bhsk,bhsv->bhkv', k * jnp.exp(b_end[:, :, None] - b), v)
    return S_new, o


def _gla_mixer(h, S0, w_in, w_gk, b_gk, g_norm, w_out):
    B, T, _ = h.shape
    q, k, v, gate, lr = jnp.split(h @ w_in, GLA_SPLITS, axis=-1)
    gk = jax.nn.log_sigmoid((lr @ w_gk + b_gk).astype(jnp.float32)) / GLA_GATE_NORM
    pad = (-T) % CHUNK
    nc = (T + pad) // CHUNK

    def to_chunks(t, d):
        t = jnp.pad(t.astype(jnp.float32), ((0, 0), (0, pad), (0, 0)))
        return t.reshape(B, nc, CHUNK, GLA_HEADS, d).transpose(1, 0, 3, 2, 4)

    S_fin, o = lax.scan(_gla_chunk_step, S0.astype(jnp.float32),
                        (to_chunks(q, GLA_HK) * GLA_HK ** -0.5, to_chunks(k, GLA_HK),
                         to_chunks(v, GLA_HV), to_chunks(gk, GLA_HK)))
    o = o.transpose(1, 0, 3, 2, 4).reshape(B, nc * CHUNK, GLA_HEADS, GLA_HV)[:, :T]
    o = _rmsnorm(o, g_norm) * jax.nn.silu(gate.astype(jnp.float32)).reshape(B, T, GLA_HEADS, GLA_HV)
    return o.reshape(B, T, GLA_DV).astype(h.dtype) @ w_out, S_fin.astype(h.dtype)


def _rel_bias(table, qpos, kpos):
    idx = jnp.clip(qpos[:, None] - kpos[None, :], -MAX_REL, MAX_REL) + MAX_REL
    return jnp.take(table.astype(jnp.float32), idx, axis=1)


def _band_mask(qpos, kpos):
    qc = qpos[:, None] // CHUNK
    kc = kpos[None, :] // CHUNK
    return (kc <= qc) & (kc >= qc - LEFT_CHUNKS) & (kpos[None, :] >= 0)


def _attend(q, k, v, bias, mask):
    s = jnp.einsum('bqhd,bkhd->bhqk', q, k).astype(jnp.float32) * ATT_HD ** -0.5 + bias
    s = jnp.where(mask, s, NEG_INF)
    p = jax.nn.softmax(s, axis=-1).astype(v.dtype)
    return jnp.einsum('bhqk,bkhd->bqhd', p, v)


def _band_attention_prompt(q, k, v, table):
    B, S, H, Dh = q.shape
    nc = S // CHUNK
    kp = jnp.pad(k, ((0, 0), (PAST_WINDOW, 0), (0, 0), (0, 0)))
    vp = jnp.pad(v, ((0, 0), (PAST_WINDOW, 0), (0, 0), (0, 0)))
    q_c = q.reshape(B, nc, CHUNK, H, Dh).swapaxes(0, 1)
    q_off = jnp.arange(CHUNK)
    k_off = jnp.arange(BAND) - PAST_WINDOW
    bias = _rel_bias(table, q_off, k_off)

    def one_chunk(args):
        c, qc = args
        start = c * CHUNK
        kb = lax.dynamic_slice_in_dim(kp, start, BAND, axis=1)
        vb = lax.dynamic_slice_in_dim(vp, start, BAND, axis=1)
        return _attend(qc, kb, vb, bias, _band_mask(start + q_off, start + k_off))

    o = lax.map(one_chunk, (jnp.arange(nc), q_c))
    return o.swapaxes(0, 1).reshape(B, S, H, Dh)


def _band_attention_sample(q, k_new, v_new, cache_k, cache_v, table):
    T = q.shape[1]
    kv_len = cache_k.shape[1]
    k = jnp.concatenate([cache_k.astype(k_new.dtype), k_new], axis=1)
    v = jnp.concatenate([cache_v.astype(v_new.dtype), v_new], axis=1)
    qpos = PAST_LEN + jnp.arange(T)
    kpos = PAST_LEN - kv_len + jnp.arange(kv_len + T)
    return _attend(q, k, v, _rel_bias(table, qpos, kpos), _band_mask(qpos, kpos))


def _trunk(x, gla_state0, cache_k, cache_v, norm_mix, norm_ffn, w_ff1, w_ff2,
           gla_w_in, gla_w_gk, gla_b_gk, gla_g_norm, gla_w_out,
           norm_kv, w_kv, att_w_q, att_rel_bias, att_w_out, norm_final):
    B, T, _ = x.shape
    gla_states = []
    k = None
    v = None
    for layer in range(DEPTH):
        h = _rmsnorm(x, norm_mix[layer])
        if layer < N_A_LAYERS:
            o, s_fin = _gla_mixer(h, gla_state0[layer], gla_w_in[layer], gla_w_gk[layer],
                                  gla_b_gk[layer], gla_g_norm[layer], gla_w_out[layer])
            gla_states.append(s_fin)
        else:
            if layer == N_A_LAYERS:
                k, v = jnp.split(_rmsnorm(x, norm_kv) @ w_kv, 2, axis=-1)
                k = k.reshape(B, T, ATT_HEADS, ATT_HD)
                v = v.reshape(B, T, ATT_HEADS, ATT_HD)
            j = layer - N_A_LAYERS
            q = (h @ att_w_q[j]).reshape(B, T, ATT_HEADS, ATT_HD)
            if cache_k is None:
                o = _band_attention_prompt(q, k, v, att_rel_bias[j])
            else:
                o = _band_attention_sample(q, k, v, cache_k, cache_v, att_rel_bias[j])
            o = o.reshape(B, T, D_MODEL) @ att_w_out[j]
        x = x + o
        x = x + _sq_relu_mlp(_rmsnorm(x, norm_ffn[layer]), w_ff1[layer], w_ff2[layer])
    return _rmsnorm(x, norm_final), jnp.stack(gla_states), k, v


def setup_inputs(seed: int = 0) -> dict:
    key = jax.random.key(seed)
    ks = iter(jax.random.split(key, 24))
    f32 = jnp.float32

    def w(shape, fan_in):
        return jax.random.normal(next(ks), shape, f32) * fan_in ** -0.5

    def gain(shape):
        return 1.0 + 0.01 * jax.random.normal(next(ks), shape, f32)

    kv_len = min(PAST_WINDOW, PAST_LEN)
    return {
        'x_prompt': jax.random.normal(next(ks), (BATCH, SEQ, D_MODEL), f32),
        'x_sample': jax.random.normal(next(ks), (DEC_BATCH, DEC_SEQ, D_MODEL), f32),
        'state_gla': jax.random.normal(next(ks), (N_A_LAYERS, DEC_BATCH, GLA_HEADS, GLA_HK, GLA_HV), f32),
        'cache_k': jax.random.normal(next(ks), (DEC_BATCH, kv_len, ATT_HEADS, ATT_HD), f32),
        'cache_v': jax.random.normal(next(ks), (DEC_BATCH, kv_len, ATT_HEADS, ATT_HD), f32),
        'norm_mix': gain((DEPTH, D_MODEL)),
        'norm_ffn': gain((DEPTH, D_MODEL)),
        'w_ff1': w((DEPTH, D_MODEL, D_FF), D_MODEL),
        'w_ff2': w((DEPTH, D_FF, D_MODEL), D_FF),
        'gla_w_in': w((N_A_LAYERS, D_MODEL, GLA_IN), D_MODEL),
        'gla_w_gk': w((N_A_LAYERS, GLA_GATE_RANK, GLA_DK), GLA_GATE_RANK),
        'gla_b_gk': 0.1 * jax.random.normal(next(ks), (N_A_LAYERS, GLA_DK), f32),
        'gla_g_norm': gain((N_A_LAYERS, GLA_HV)),
        'gla_w_out': w((N_A_LAYERS, GLA_DV, D_MODEL), GLA_DV),
        'norm_kv': gain((D_MODEL,)),
        'w_kv': w((D_MODEL, 2 * ATT_HEADS * ATT_HD), D_MODEL),
        'att_w_q': w((N_B_LAYERS, D_MODEL, ATT_HEADS * ATT_HD), D_MODEL),
        'att_rel_bias': 0.2 * jax.random.normal(next(ks), (N_B_LAYERS, ATT_HEADS, N_REL), f32),
        'att_w_out': w((N_B_LAYERS, ATT_HEADS * ATT_HD, D_MODEL), ATT_HEADS * ATT_HD),
        'norm_final': gain((D_MODEL,)),
    }


def reference(x_prompt, x_sample, state_gla, cache_k, cache_v, norm_mix, norm_ffn, w_ff1, w_ff2,
              gla_w_in, gla_w_gk, gla_b_gk, gla_g_norm, gla_w_out, norm_kv, w_kv,
              att_w_q, att_rel_bias, att_w_out, norm_final):
    B, S, _ = x_prompt.shape
    zero_state = jnp.zeros((N_A_LAYERS, B, GLA_HEADS, GLA_HK, GLA_HV), jnp.float32)
    y_prompt, state_gla_prompt, k_p, v_p = _trunk(
        x_prompt, zero_state, None, None, norm_mix, norm_ffn, w_ff1, w_ff2,
        gla_w_in, gla_w_gk, gla_b_gk, gla_g_norm, gla_w_out,
        norm_kv, w_kv, att_w_q, att_rel_bias, att_w_out, norm_final)
    keep = min(PAST_WINDOW, S)
    k_rows_prompt = k_p[:, S - keep:]
    v_rows_prompt = v_p[:, S - keep:]
    y_sample, state_gla_sample, k_rows_sample, v_rows_sample = _trunk(
        x_sample, state_gla, cache_k, cache_v, norm_mix, norm_ffn, w_ff1, w_ff2,
        gla_w_in, gla_w_gk, gla_b_gk, gla_g_norm, gla_w_out,
        norm_kv, w_kv, att_w_q, att_rel_bias, att_w_out, norm_final)
    return (y_prompt, y_sample, state_gla_prompt, k_rows_prompt, v_rows_prompt,
            state_gla_sample, k_rows_sample, v_rows_sample)
```

```python
import functools

import jax
import jax.numpy as jnp
from jax import lax
from jax.experimental import pallas as pl
from jax.experimental.pallas import tpu as pltpu

F32 = jnp.float32
BF16 = jnp.bfloat16

CHUNK = 64
GLA_HEADS = 4
GLA_SUB = 16
GLA_GATE_NORM = 16.0
ATT_HEADS = 16
LEFT_CHUNKS = 8
MAX_REL = 128
EPS = 1e-6
NEG_INF = -1e30

LANES = 128
VMEM_LIMIT_BYTES = 56 * 2**20
ROW_TILE = 1024
NORM_ROWS = 256
ATT_QBLOCK = 4 * CHUNK
ATT_HEAD_GROUP = 2
GLA_ROWS = 8 * CHUNK


def _tile(n, pref):
    if n <= pref:
        return n
    t = pref
    while n % t:
        t //= 2
    return t


def _params(sem):
    return pltpu.CompilerParams(dimension_semantics=sem, vmem_limit_bytes=VMEM_LIMIT_BYTES)


def _rmsnorm_rows(x_ref, g_ref, h_ref):
    rows = x_ref.shape[0]
    rc = _tile(rows, NORM_ROWS)

    def body(r, carry):
        sl = pl.ds(pl.multiple_of(r * rc, rc), rc)
        x = x_ref[sl, :]
        ms = jnp.mean(x * x, axis=-1, keepdims=True)
        h_ref[sl, :] = (x * lax.rsqrt(ms + EPS) * g_ref[...]).astype(h_ref.dtype)
        return carry

    lax.fori_loop(0, rows // rc, body, 0)


def _norm_matmul_kernel(*refs, n_out, has_side):
    x_ref, g_ref, w_ref = refs[:3]
    pos = 3
    ws_ref = None
    if has_side:
        ws_ref = refs[pos]
        pos += 1
    out_refs = refs[pos:pos + n_out]
    pos += n_out
    side_ref = None
    if has_side:
        side_ref = refs[pos]
        pos += 1
    h_ref = refs[pos]

    @pl.when(pl.program_id(1) == 0)
    def _():
        _rmsnorm_rows(x_ref, g_ref, h_ref)
        if has_side:
            side_ref[...] = jnp.dot(h_ref[...], ws_ref[...], preferred_element_type=F32)

    y = jnp.dot(h_ref[...], w_ref[...], preferred_element_type=F32)
    for o_ref in out_refs:
        o_ref[...] = y.astype(o_ref.dtype)


def _norm_matmul(x, g, w, out_dtypes, w_side=None, name="norm_matmul"):
    m, d = x.shape
    n = w.shape[1]
    tm = _tile(m, ROW_TILE)
    tn = _tile(n, 1024)
    has_side = w_side is not None
    in_specs = [
        pl.BlockSpec((tm, d), lambda i, j: (i, 0)),
        pl.BlockSpec((1, d), lambda i, j: (0, 0)),
        pl.BlockSpec((d, tn), lambda i, j: (0, j)),
    ]
    args = [x, g.reshape(1, d), w]
    out_specs = [pl.BlockSpec((tm, tn), lambda i, j: (i, j)) for _ in out_dtypes]
    out_shape = [jax.ShapeDtypeStruct((m, n), dt) for dt in out_dtypes]
    if has_side:
        ns = w_side.shape[1]
        in_specs.append(pl.BlockSpec((d, ns), lambda i, j: (0, 0)))
        args.append(w_side)
        out_specs.append(pl.BlockSpec((tm, ns), lambda i, j: (i, 0)))
        out_shape.append(jax.ShapeDtypeStruct((m, ns), F32))
    return pl.pallas_call(
        functools.partial(_norm_matmul_kernel, n_out=len(out_dtypes), has_side=has_side),
        grid=(m // tm, n // tn),
        in_specs=in_specs,
        out_specs=out_specs,
        out_shape=out_shape,
        scratch_shapes=[pltpu.VMEM((tm, d), BF16)],
        compiler_params=_params(("parallel", "arbitrary")),
        name=name,
    )(*args)


def _matmul_res_kernel(a_ref, w_ref, x_ref, o_ref):
    o_ref[...] = x_ref[...] + jnp.dot(a_ref[...], w_ref[...], preferred_element_type=F32)


def _matmul_res(a, w, x, name="matmul_res"):
    m, k = a.shape
    n = w.shape[1]
    tm = _tile(m, ROW_TILE)
    tn = _tile(n, 1024)
    return pl.pallas_call(
        _matmul_res_kernel,
        grid=(m // tm, n // tn),
        in_specs=[
            pl.BlockSpec((tm, k), lambda i, j: (i, 0)),
            pl.BlockSpec((k, tn), lambda i, j: (0, j)),
            pl.BlockSpec((tm, tn), lambda i, j: (i, j)),
        ],
        out_specs=pl.BlockSpec((tm, tn), lambda i, j: (i, j)),
        out_shape=jax.ShapeDtypeStruct((m, n), F32),
        compiler_params=_params(("parallel", "arbitrary")),
        name=name,
    )(a, w, x)


def _mlp_kernel(*refs, final_norm):
    if final_norm:
        x_ref, g_ref, w1_ref, w2_ref, gf_ref, o_ref, h_ref = refs
    else:
        x_ref, g_ref, w1_ref, w2_ref, o_ref, h_ref = refs
        gf_ref = None
    f = pl.program_id(1)

    @pl.when(f == 0)
    def _():
        _rmsnorm_rows(x_ref, g_ref, h_ref)
        o_ref[...] = x_ref[...]

    a = jnp.dot(h_ref[...], w1_ref[...], preferred_element_type=F32)
    a = jnp.maximum(a, 0.0)
    a = (a * a).astype(BF16)
    o_ref[...] += jnp.dot(a, w2_ref[...], preferred_element_type=F32)

    if final_norm:
        @pl.when(f == pl.num_programs(1) - 1)
        def _():
            rows = o_ref.shape[0]
            rc = _tile(rows, NORM_ROWS)

            def body(r, carry):
                sl = pl.ds(pl.multiple_of(r * rc, rc), rc)
                y = o_ref[sl, :]
                ms = jnp.mean(y * y, axis=-1, keepdims=True)
                o_ref[sl, :] = y * lax.rsqrt(ms + EPS) * gf_ref[...]
                return carry

            lax.fori_loop(0, rows // rc, body, 0)


def _mlp(x, g, w1, w2, g_final=None, name="mlp"):
    m, d = x.shape
    ff = w1.shape[1]
    tm = _tile(m, ROW_TILE)
    tf = _tile(ff, 512)
    final_norm = g_final is not None
    in_specs = [
        pl.BlockSpec((tm, d), lambda i, f: (i, 0)),
        pl.BlockSpec((1, d), lambda i, f: (0, 0)),
        pl.BlockSpec((d, tf), lambda i, f: (0, f)),
        pl.BlockSpec((tf, d), lambda i, f: (f, 0)),
    ]
    args = [x, g.reshape(1, d), w1, w2]
    if final_norm:
        in_specs.append(pl.BlockSpec((1, d), lambda i, f: (0, 0)))
        args.append(g_final.reshape(1, d))
    return pl.pallas_call(
        functools.partial(_mlp_kernel, final_norm=final_norm),
        grid=(m // tm, ff // tf),
        in_specs=in_specs,
        out_specs=pl.BlockSpec((tm, d), lambda i, f: (i, 0)),
        out_shape=jax.ShapeDtypeStruct((m, d), F32),
        scratch_shapes=[pltpu.VMEM((tm, d), BF16)],
        compiler_params=_params(("parallel", "arbitrary")),
        name=name,
    )(*args)


def _gla_chunk(q, k, v, lr, wgk, bgk, s):
    c, hk = q.shape
    nsub = c // GLA_SUB
    z = jnp.dot(lr.astype(BF16), wgk, preferred_element_type=F32) + bgk
    g = (jnp.minimum(z, 0.0) - jnp.log1p(jnp.exp(-jnp.abs(z)))) * (1.0 / GLA_GATE_NORM)
    row = lax.broadcasted_iota(jnp.int32, (c, c), 0)
    col = lax.broadcasted_iota(jnp.int32, (c, c), 1)
    tri = (col <= row).astype(F32)
    b = jnp.dot(tri, g, precision=lax.Precision.HIGHEST, preferred_element_type=F32)
    b_end = b[c - 1:c, :]

    o = jnp.dot((q * jnp.exp(b)).astype(BF16), s.astype(BF16), preferred_element_type=F32)

    a_rows = [jnp.zeros((GLA_SUB, c), F32)]
    for l in range(1, nsub):
        lo = l * GLA_SUB
        ref = b[lo - 1:lo, :]
        q_ref = q[lo:lo + GLA_SUB, :] * jnp.exp(b[lo:lo + GLA_SUB, :] - ref)
        k_ref = k * jnp.exp(jnp.minimum(ref - b, 0.0))
        a_rows.append(lax.dot_general(q_ref.astype(BF16), k_ref.astype(BF16),
                                      (((1,), (1,)), ((), ())), preferred_element_type=F32))
    a_off = jnp.concatenate(a_rows, axis=0)

    k3 = k.reshape(nsub, GLA_SUB, hk)
    b3 = b.reshape(nsub, GLA_SUB, hk)
    row_blk = jnp.right_shift(row, 4)
    col_blk = jnp.right_shift(col, 4)
    a_diag = jnp.zeros((c, c), F32)
    for e in range(GLA_SUB):
        k_e = jnp.broadcast_to(k3[:, e:e + 1, :], (nsub, GLA_SUB, hk)).reshape(c, hk)
        b_e = jnp.broadcast_to(b3[:, e:e + 1, :], (nsub, GLA_SUB, hk)).reshape(c, hk)
        t = q * k_e * jnp.exp(jnp.minimum(b - b_e, 0.0))
        a_diag = jnp.where(col == row_blk * GLA_SUB + e, jnp.sum(t, axis=-1, keepdims=True), a_diag)

    a = jnp.where(col_blk < row_blk, a_off,
                  jnp.where((col_blk == row_blk) & (col <= row), a_diag, 0.0))
    v16 = v.astype(BF16)
    o = o + jnp.dot(a.astype(BF16), v16, preferred_element_type=F32)

    k_dec = (k * jnp.exp(b_end - b)).astype(BF16)
    upd = lax.dot_general(k_dec, v16, (((0,), (0,)), ((), ())), preferred_element_type=F32)
    decay_col = jnp.transpose(jnp.broadcast_to(jnp.exp(b_end), (LANES, hk)))[:, :1]
    return o, decay_col * s + upd


def _gla_kernel(*refs, has_state0, n_chunks):
    if has_state0:
        q_ref, k_ref, v_ref, gate_ref, lr_ref, wgk_ref, bgk_ref, gn_ref, s0_ref, o_ref, sfin_ref, s_scr = refs
    else:
        q_ref, k_ref, v_ref, gate_ref, lr_ref, wgk_ref, bgk_ref, gn_ref, o_ref, sfin_ref, s_scr = refs
        s0_ref = None
    t = pl.program_id(2)
    hk = q_ref.shape[-1]
    scale = hk ** -0.5

    @pl.when(t == 0)
    def _():
        if has_state0:
            s_scr[...] = s0_ref[...]
        else:
            s_scr[...] = jnp.zeros_like(s_scr)

    def body(ci, carry):
        sl = pl.ds(pl.multiple_of(ci * CHUNK, CHUNK), CHUNK)
        o, s_new = _gla_chunk(q_ref[sl, :] * scale, k_ref[sl, :], v_ref[sl, :], lr_ref[sl, :],
                              wgk_ref[...], bgk_ref[...], s_scr[...])
        s_scr[...] = s_new
        ms = jnp.mean(o * o, axis=-1, keepdims=True)
        y = o * lax.rsqrt(ms + EPS) * gn_ref[...]
        gate = gate_ref[sl, :]
        o_ref[sl, :] = (y * (gate * (1.0 / (1.0 + jnp.exp(-gate))))).astype(o_ref.dtype)
        return carry

    lax.fori_loop(0, n_chunks, body, 0)

    @pl.when(t == pl.num_programs(2) - 1)
    def _():
        sfin_ref[...] = s_scr[...]


def _gla(proj, lr, wgk, bgk, g_norm, state0, batch, seq, name="gla"):
    h = GLA_HEADS
    dv = proj.shape[1] // 3
    dk = dv // 2
    hk, hv = dk // h, dv // h
    tb = _tile(seq, GLA_ROWS)
    proj3 = proj.reshape(batch, seq, proj.shape[1])
    lr3 = lr.reshape(batch, seq, lr.shape[1])
    in_specs = [
        pl.BlockSpec((None, tb, hk), lambda b, hh, t: (b, t, hh)),
        pl.BlockSpec((None, tb, hk), lambda b, hh, t: (b, t, h + hh)),
        pl.BlockSpec((None, tb, hv), lambda b, hh, t: (b, t, 2 * dk // hv + hh)),
        pl.BlockSpec((None, tb, hv), lambda b, hh, t: (b, t, (2 * dk + dv) // hv + hh)),
        pl.BlockSpec((None, tb, lr.shape[1]), lambda b, hh, t: (b, t, 0)),
        pl.BlockSpec((lr.shape[1], hk), lambda b, hh, t: (0, hh)),
        pl.BlockSpec((1, hk), lambda b, hh, t: (0, hh)),
        pl.BlockSpec((1, hv), lambda b, hh, t: (0, 0)),
    ]
    args = [proj3, proj3, proj3, proj3, lr3, wgk, bgk.reshape(1, dk), g_norm.reshape(1, hv)]
    has_state0 = state0 is not None
    if has_state0:
        in_specs.append(pl.BlockSpec((None, None, hk, hv), lambda b, hh, t: (b, hh, 0, 0)))
        args.append(state0)
    og, s_fin = pl.pallas_call(
        functools.partial(_gla_kernel, has_state0=has_state0, n_chunks=tb // CHUNK),
        grid=(batch, h, seq // tb),
        in_specs=in_specs,
        out_specs=[
            pl.BlockSpec((None, tb, hv), lambda b, hh, t: (b, t, hh)),
            pl.BlockSpec((None, None, hk, hv), lambda b, hh, t: (b, hh, 0, 0)),
        ],
        out_shape=[
            jax.ShapeDtypeStruct((batch, seq, dv), BF16),
            jax.ShapeDtypeStruct((batch, h, hk, hv), F32),
        ],
        scratch_shapes=[pltpu.VMEM((hk, hv), F32)],
        compiler_params=_params(("parallel", "parallel", "arbitrary")),
        name=name,
    )(*args)
    return og.reshape(batch * seq, dv), s_fin


def _attn_kernel(q_ref, k_ref, v_ref, bm_ref, o_ref, *, qb, past, hd, hg):
    tq = q_ref.shape[0]
    left = LEFT_CHUNKS * CHUNK
    wfull = bm_ref.shape[-1]
    scale = hd ** -0.5
    for hh in range(hg):
        cs = slice(hh * hd, (hh + 1) * hd)
        for i in range(tq // qb):
            rows = slice(i * qb, (i + 1) * qb)
            k0 = max(0, i * qb + past - left)
            k1 = i * qb + past + qb
            s = lax.dot_general(q_ref[rows, cs], k_ref[k0:k1, cs], (((1,), (1,)), ((), ())),
                                preferred_element_type=F32)
            s = s * scale + bm_ref[hh, :, wfull - (k1 - k0):]
            e = jnp.exp(s - jnp.max(s, axis=-1, keepdims=True))
            p = (e / jnp.sum(e, axis=-1, keepdims=True)).astype(BF16)
            o_ref[rows, cs] = jnp.dot(p, v_ref[k0:k1, cs], preferred_element_type=F32).astype(o_ref.dtype)


def _attention(q, kv_k, kv_v, k_col0, v_col0, bm, past, name="attn"):
    batch, tq, d = q.shape
    tk = kv_k.shape[1]
    hd = d // ATT_HEADS
    hg = ATT_HEAD_GROUP
    gw = hg * hd
    qb = bm.shape[1]
    return pl.pallas_call(
        functools.partial(_attn_kernel, qb=qb, past=past, hd=hd, hg=hg),
        grid=(batch, ATT_HEADS // hg),
        in_specs=[
            pl.BlockSpec((None, tq, gw), lambda b, g: (b, 0, g)),
            pl.BlockSpec((None, tk, gw), lambda b, g: (b, 0, k_col0 // gw + g)),
            pl.BlockSpec((None, tk, gw), lambda b, g: (b, 0, v_col0 // gw + g)),
            pl.BlockSpec((hg, qb, bm.shape[2]), lambda b, g: (g, 0, 0)),
        ],
        out_specs=pl.BlockSpec((None, tq, gw), lambda b, g: (b, 0, g)),
        out_shape=jax.ShapeDtypeStruct((batch, tq, d), BF16),
        compiler_params=_params(("parallel", "parallel")),
        name=name,
    )(q, kv_k, kv_v, bm)


def _bias_mask(table, qb):
    left = LEFT_CHUNKS * CHUNK
    r = jnp.arange(qb)[:, None]
    c = jnp.arange(left + qb)[None, :]
    rel = jnp.clip(r - (c - left), -MAX_REL, MAX_REL) + MAX_REL
    bias = jnp.take(table.astype(F32), rel, axis=1)
    qc = r // CHUNK
    kc = c // CHUNK - LEFT_CHUNKS
    allowed = (kc <= qc) & (kc >= qc - LEFT_CHUNKS)
    return jnp.where(allowed[None], bias, NEG_INF)


def _trunk(x, state0, cache_k, cache_v, w, past):
    batch, seq, d = x.shape
    depth = w["norm_mix"].shape[0]
    n_a = depth // 2
    dk = d // 2
    x = x.reshape(batch * seq, d)
    states = []
    kv32 = None
    kv16 = None
    qb = _tile(seq, ATT_QBLOCK)
    for layer in range(depth):
        if layer < n_a:
            proj, lr = _norm_matmul(x, w["norm_mix"][layer], w["gla_w_main"][layer], [F32],
                                    w_side=w["gla_w_lr"][layer], name="gla_in")
            og, s_fin = _gla(proj, lr,
                             w["gla_w_gk"][layer], w["gla_b_gk"][layer], w["gla_g_norm"][layer],
                             None if state0 is None else state0[layer], batch, seq)
            states.append(s_fin)
            x = _matmul_res(og, w["gla_w_out"][layer], x, name="gla_out")
        else:
            j = layer - n_a
            if layer == n_a:
                kv32, kv16 = _norm_matmul(x, w["norm_kv"], w["w_kv"], [F32, BF16], name="kv_proj")
            (q16,) = _norm_matmul(x, w["norm_mix"][layer], w["att_w_q"][j], [BF16], name="q_proj")
            q3 = q16.reshape(batch, seq, d)
            bm = _bias_mask(w["att_rel_bias"][j], qb)
            if cache_k is None:
                kv3 = kv16.reshape(batch, seq, 2 * d)
                o = _attention(q3, kv3, kv3, 0, d, bm, past)
            else:
                k3 = jnp.concatenate([cache_k.reshape(batch, -1, d).astype(BF16),
                                      kv16[:, :d].reshape(batch, seq, d)], axis=1)
                v3 = jnp.concatenate([cache_v.reshape(batch, -1, d).astype(BF16),
                                      kv16[:, d:].reshape(batch, seq, d)], axis=1)
                o = _attention(q3, k3, v3, 0, 0, bm, past)
            x = _matmul_res(o.reshape(batch * seq, d), w["att_w_out"][j], x, name="att_out")
        g_final = w["norm_final"] if layer == depth - 1 else None
        x = _mlp(x, w["norm_ffn"][layer], w["w_ff1"][layer], w["w_ff2"][layer], g_final)
    hd = d // ATT_HEADS
    k = kv32[:, :d].reshape(batch, seq, ATT_HEADS, hd)
    v = kv32[:, d:].reshape(batch, seq, ATT_HEADS, hd)
    return x.reshape(batch, seq, d), jnp.stack(states), k, v


def kernel(x_prompt, x_sample, state_gla, cache_k, cache_v, norm_mix, norm_ffn, w_ff1, w_ff2, gla_w_in, gla_w_gk, gla_b_gk, gla_g_norm, gla_w_out, norm_kv, w_kv, att_w_q, att_rel_bias, att_w_out, norm_final):
    d = x_prompt.shape[-1]
    dk = d // 2
    n_main = 2 * dk + 2 * d
    rank = gla_w_in.shape[-1] - n_main
    pad = LANES - rank
    w = {
        "norm_mix": norm_mix, "norm_ffn": norm_ffn, "norm_kv": norm_kv, "norm_final": norm_final,
        "w_ff1": w_ff1.astype(BF16), "w_ff2": w_ff2.astype(BF16),
        "gla_w_main": gla_w_in[:, :, :n_main].astype(BF16),
        "gla_w_lr": jnp.pad(gla_w_in[:, :, n_main:], ((0, 0), (0, 0), (0, pad))).astype(BF16),
        "gla_w_gk": jnp.pad(gla_w_gk, ((0, 0), (0, pad), (0, 0))).astype(BF16),
        "gla_b_gk": gla_b_gk, "gla_g_norm": gla_g_norm,
        "gla_w_out": gla_w_out.astype(BF16),
        "w_kv": w_kv.astype(BF16), "att_w_q": att_w_q.astype(BF16),
        "att_rel_bias": att_rel_bias, "att_w_out": att_w_out.astype(BF16),
    }
    seq = x_prompt.shape[1]
    keep = min(LEFT_CHUNKS * CHUNK, seq)
    y_p, s_p, k_p, v_p = _trunk(x_prompt, None, None, None, w, 0)
    y_s, s_s, k_s, v_s = _trunk(x_sample, state_gla, cache_k, cache_v, w, cache_k.shape[1])
    return (y_p, y_s, s_p, k_p[:, seq - keep:], v_p[:, seq - keep:], s_s, k_s, v_s)
```

```python
import functools

import jax
import jax.numpy as jnp
from jax import lax
from jax.experimental import pallas as pl
from jax.experimental.pallas import tpu as pltpu

F32 = jnp.float32
BF16 = jnp.bfloat16

CHUNK = 64
GLA_HEADS = 4
GLA_SUB = 16
GLA_SUB_SHIFT = GLA_SUB.bit_length() - 1
GLA_GATE_NORM = 16.0
ATT_HEADS = 16
LEFT_CHUNKS = 8
MAX_REL = 128
EPS = 1e-6
NEG_INF = -1e30
LOG2E = 1.4426950408889634

LANES = 128
VMEM_LIMIT_BYTES = 56 * 2**20
ROW_TILE = 1024
NORM_ROWS = 256
ATT_QBLOCK = 4 * CHUNK
ATT_HEAD_GROUP = 2
GLA_ROWS = 8 * CHUNK
SUBLANES = 8


def _tile(n, pref):
    if n <= pref:
        return n
    t = pref
    while n % t:
        t //= 2
    return t


def _params(sem):
    return pltpu.CompilerParams(dimension_semantics=sem, vmem_limit_bytes=VMEM_LIMIT_BYTES)


def _rmsnorm_rows(x_ref, g_ref, h_ref):
    rows = x_ref.shape[0]
    rc = _tile(rows, NORM_ROWS)

    def body(r, carry):
        sl = pl.ds(pl.multiple_of(r * rc, rc), rc)
        x = x_ref[sl, :]
        ms = jnp.mean(x * x, axis=-1, keepdims=True)
        h_ref[sl, :] = (x * lax.rsqrt(ms + EPS) * g_ref[...]).astype(h_ref.dtype)
        return carry

    lax.fori_loop(0, rows // rc, body, 0)


def _norm_matmul_kernel(*refs, n_out, has_side, out_scale):
    x_ref, g_ref, w_ref = refs[:3]
    pos = 3
    ws_ref = None
    if has_side:
        ws_ref = refs[pos]
        pos += 1
    out_refs = refs[pos:pos + n_out]
    pos += n_out
    side_ref = None
    if has_side:
        side_ref = refs[pos]
        pos += 1
    h_ref = refs[pos]

    @pl.when(pl.program_id(1) == 0)
    def _():
        _rmsnorm_rows(x_ref, g_ref, h_ref)
        if has_side:
            side_ref[...] = jnp.dot(h_ref[...], ws_ref[...], preferred_element_type=F32)

    y = jnp.dot(h_ref[...], w_ref[...], preferred_element_type=F32)
    if out_scale is not None:
        y = y * out_scale
    for o_ref in out_refs:
        o_ref[...] = y.astype(o_ref.dtype)


def _norm_matmul(x, g, w, out_dtypes, w_side=None, out_scale=None, name="norm_matmul"):
    m, d = x.shape
    n = w.shape[1]
    tm = _tile(m, ROW_TILE)
    tn = _tile(n, 1024)
    has_side = w_side is not None
    in_specs = [
        pl.BlockSpec((tm, d), lambda i, j: (i, 0)),
        pl.BlockSpec((1, d), lambda i, j: (0, 0)),
        pl.BlockSpec((d, tn), lambda i, j: (0, j)),
    ]
    args = [x, g.reshape(1, d), w]
    out_specs = [pl.BlockSpec((tm, tn), lambda i, j: (i, j)) for _ in out_dtypes]
    out_shape = [jax.ShapeDtypeStruct((m, n), dt) for dt in out_dtypes]
    if has_side:
        ns = w_side.shape[1]
        in_specs.append(pl.BlockSpec((d, ns), lambda i, j: (0, 0)))
        args.append(w_side)
        out_specs.append(pl.BlockSpec((tm, ns), lambda i, j: (i, 0)))
        out_shape.append(jax.ShapeDtypeStruct((m, ns), F32))
    return pl.pallas_call(
        functools.partial(_norm_matmul_kernel, n_out=len(out_dtypes), has_side=has_side, out_scale=out_scale),
        grid=(m // tm, n // tn),
        in_specs=in_specs,
        out_specs=out_specs,
        out_shape=out_shape,
        scratch_shapes=[pltpu.VMEM((tm, d), BF16)],
        compiler_params=_params(("parallel", "arbitrary")),
        name=name,
    )(*args)


def _matmul_res_kernel(a_ref, w_ref, x_ref, o_ref):
    o_ref[...] = x_ref[...] + jnp.dot(a_ref[...], w_ref[...], preferred_element_type=F32)


def _matmul_res(a, w, x, name="matmul_res"):
    m, k = a.shape
    n = w.shape[1]
    tm = _tile(m, ROW_TILE)
    tn = _tile(n, 1024)
    return pl.pallas_call(
        _matmul_res_kernel,
        grid=(m // tm, n // tn),
        in_specs=[
            pl.BlockSpec((tm, k), lambda i, j: (i, 0)),
            pl.BlockSpec((k, tn), lambda i, j: (0, j)),
            pl.BlockSpec((tm, tn), lambda i, j: (i, j)),
        ],
        out_specs=pl.BlockSpec((tm, tn), lambda i, j: (i, j)),
        out_shape=jax.ShapeDtypeStruct((m, n), F32),
        compiler_params=_params(("parallel", "arbitrary")),
        name=name,
    )(a, w, x)


def _mlp_kernel(*refs, final_norm):
    if final_norm:
        x_ref, g_ref, w1_ref, w2_ref, gf_ref, o_ref, h_ref = refs
    else:
        x_ref, g_ref, w1_ref, w2_ref, o_ref, h_ref = refs
        gf_ref = None
    f = pl.program_id(1)

    @pl.when(f == 0)
    def _():
        _rmsnorm_rows(x_ref, g_ref, h_ref)
        o_ref[...] = x_ref[...]

    a = jnp.dot(h_ref[...], w1_ref[...], preferred_element_type=F32)
    a = jnp.maximum(a, 0.0)
    a = (a * a).astype(BF16)
    o_ref[...] += jnp.dot(a, w2_ref[...], preferred_element_type=F32)

    if final_norm:
        @pl.when(f == pl.num_programs(1) - 1)
        def _():
            rows = o_ref.shape[0]
            rc = _tile(rows, NORM_ROWS)

            def body(r, carry):
                sl = pl.ds(pl.multiple_of(r * rc, rc), rc)
                y = o_ref[sl, :]
                ms = jnp.mean(y * y, axis=-1, keepdims=True)
                o_ref[sl, :] = y * lax.rsqrt(ms + EPS) * gf_ref[...]
                return carry

            lax.fori_loop(0, rows // rc, body, 0)


def _mlp(x, g, w1, w2, g_final=None, name="mlp"):
    m, d = x.shape
    ff = w1.shape[1]
    tm = _tile(m, ROW_TILE)
    tf = _tile(ff, 512)
    final_norm = g_final is not None
    in_specs = [
        pl.BlockSpec((tm, d), lambda i, f: (i, 0)),
        pl.BlockSpec((1, d), lambda i, f: (0, 0)),
        pl.BlockSpec((d, tf), lambda i, f: (0, f)),
        pl.BlockSpec((tf, d), lambda i, f: (f, 0)),
    ]
    args = [x, g.reshape(1, d), w1, w2]
    if final_norm:
        in_specs.append(pl.BlockSpec((1, d), lambda i, f: (0, 0)))
        args.append(g_final.reshape(1, d))
    return pl.pallas_call(
        functools.partial(_mlp_kernel, final_norm=final_norm),
        grid=(m // tm, ff // tf),
        in_specs=in_specs,
        out_specs=pl.BlockSpec((tm, d), lambda i, f: (i, 0)),
        out_shape=jax.ShapeDtypeStruct((m, d), F32),
        scratch_shapes=[pltpu.VMEM((tm, d), BF16)],
        compiler_params=_params(("parallel", "arbitrary")),
        name=name,
    )(*args)


def _log2_gates(lr, wgk, bgk):
    z = jnp.dot(lr.astype(BF16), wgk, preferred_element_type=F32) + bgk
    return (jnp.minimum(z, 0.0) - jnp.log(1.0 + jnp.exp(-jnp.abs(z)))) * (LOG2E / GLA_GATE_NORM)


def _chunk_cumsums(g, n_chunks):
    c = CHUNK
    row = lax.broadcasted_iota(jnp.int32, (c, c), 0)
    col = lax.broadcasted_iota(jnp.int32, (c, c), 1)
    tri = (col <= row).astype(BF16)
    g_hi = g.astype(BF16)
    g_r = g - g_hi.astype(F32)
    g_mid = g_r.astype(BF16)
    g_lo = (g_r - g_mid.astype(F32)).astype(BF16)
    out = []
    for ci in range(n_chunks):
        r = slice(ci * c, (ci + 1) * c)
        out.append(jnp.dot(tri, g_hi[r], preferred_element_type=F32)
                   + jnp.dot(tri, g_mid[r], preferred_element_type=F32)
                   + jnp.dot(tri, g_lo[r], preferred_element_type=F32))
    return out


def _gla_offdiag(q, k, b):
    c = q.shape[0]
    a_rows = [jnp.zeros((GLA_SUB, c), F32)]
    for l in range(1, c // GLA_SUB):
        lo = l * GLA_SUB
        ref = b[lo - 1:lo, :]
        q_ref = q[lo:lo + GLA_SUB, :] * jnp.exp2(b[lo:lo + GLA_SUB, :] - ref)
        k_ref = k * jnp.exp2(jnp.minimum(ref - b, 0.0))
        a_rows.append(lax.dot_general(q_ref.astype(BF16), k_ref.astype(BF16),
                                      (((1,), (1,)), ((), ())), preferred_element_type=F32))
    return jnp.concatenate(a_rows, axis=0)


def _gla_diag(q, k, b):
    c, hk = q.shape
    nsub = c // GLA_SUB
    q3 = q.reshape(nsub, GLA_SUB, hk)
    k3 = k.reshape(nsub, GLA_SUB, hk)
    b3 = b.reshape(nsub, GLA_SUB, hk)
    col3 = lax.broadcasted_iota(jnp.int32, (nsub, SUBLANES, c), 2)
    blk3 = lax.broadcasted_iota(jnp.int32, (nsub, SUBLANES, c), 0) * GLA_SUB
    top = jnp.zeros((nsub, SUBLANES, c), F32)
    bot = jnp.zeros((nsub, GLA_SUB - SUBLANES, c), F32)
    for e in range(GLA_SUB):
        lo = 0 if e < SUBLANES else SUBLANES
        k_e = jnp.broadcast_to(k3[:, e:e + 1, :], (nsub, GLA_SUB - lo, hk))
        b_e = jnp.broadcast_to(b3[:, e:e + 1, :], (nsub, GLA_SUB - lo, hk))
        t = q3[:, lo:, :] * k_e * jnp.exp2(jnp.minimum(b3[:, lo:, :] - b_e, 0.0))
        ts = jnp.sum(t, axis=-1, keepdims=True)
        hit = col3 == blk3 + e
        if lo == 0:
            top = jnp.where(hit, ts[:, :SUBLANES, :], top)
            bot = jnp.where(hit, ts[:, SUBLANES:, :], bot)
        else:
            bot = jnp.where(hit, ts, bot)
    return jnp.concatenate([top, bot], axis=1).reshape(c, c)


def _gla_kernel(*refs, has_state0, n_chunks):
    if has_state0:
        q_ref, k_ref, v_ref, gate_ref, lr_ref, wgk_ref, bgk_ref, gn_ref, s0_ref, o_ref, sfin_ref, s_scr = refs
    else:
        q_ref, k_ref, v_ref, gate_ref, lr_ref, wgk_ref, bgk_ref, gn_ref, o_ref, sfin_ref, s_scr = refs
        s0_ref = None
    t = pl.program_id(2)
    hk = q_ref.shape[-1]
    scale = hk ** -0.5
    c = CHUNK

    @pl.when(t == 0)
    def _():
        if has_state0:
            s_scr[...] = s0_ref[...]
        else:
            s_scr[...] = jnp.zeros_like(s_scr)

    bs = _chunk_cumsums(_log2_gates(lr_ref[...], wgk_ref[...], bgk_ref[...]), n_chunks)
    row = lax.broadcasted_iota(jnp.int32, (c, c), 0)
    col = lax.broadcasted_iota(jnp.int32, (c, c), 1)
    row_blk = jnp.right_shift(row, GLA_SUB_SHIFT)
    col_blk = jnp.right_shift(col, GLA_SUB_SHIFT)
    below = col_blk < row_blk
    on_diag = (col_blk == row_blk) & (col <= row)

    def prepare(ci):
        r = slice(ci * c, (ci + 1) * c)
        q = q_ref[r, :].astype(F32) * scale
        k = k_ref[r, :].astype(F32)
        b = bs[ci]
        b_end = b[c - 1:c, :]
        decay_col = jnp.transpose(jnp.broadcast_to(jnp.exp2(b_end), (LANES, hk)))[:, :1]
        return dict(q=q, k=k, b=b, q_dec=(q * jnp.exp2(b)).astype(BF16),
                    k_dec=(k * jnp.exp2(b_end - b)).astype(BF16), decay_col=decay_col,
                    a_off=_gla_offdiag(q, k, b))

    def finish(ci, o):
        r = slice(ci * c, (ci + 1) * c)
        ms = jnp.mean(o * o, axis=-1, keepdims=True)
        y = o * lax.rsqrt(ms + EPS) * gn_ref[...]
        gate = gate_ref[r, :].astype(F32)
        o_ref[r, :] = (y * (gate * (1.0 / (1.0 + jnp.exp(-gate))))).astype(o_ref.dtype)

    s = s_scr[...]
    cur = prepare(0)
    prev_out = None
    prev_upd = None
    for ci in range(n_chunks):
        if prev_upd is not None:
            s = prev_upd[0] * s + prev_upd[1]
        o_inter = jnp.dot(cur["q_dec"], s.astype(BF16), preferred_element_type=F32)
        if prev_out is not None:
            finish(ci - 1, prev_out[0] + prev_out[1])
        a_diag = _gla_diag(cur["q"], cur["k"], cur["b"])
        a = jnp.where(below, cur["a_off"], jnp.where(on_diag, a_diag, 0.0)).astype(BF16)
        v16 = v_ref[ci * c:(ci + 1) * c, :]
        prev_out = (jnp.dot(a, v16, preferred_element_type=F32), o_inter)
        prev_upd = (cur["decay_col"],
                    lax.dot_general(cur["k_dec"], v16, (((0,), (0,)), ((), ())), preferred_element_type=F32))
        if ci + 1 < n_chunks:
            cur = prepare(ci + 1)
    s_scr[...] = prev_upd[0] * s + prev_upd[1]
    finish(n_chunks - 1, prev_out[0] + prev_out[1])

    @pl.when(t == pl.num_programs(2) - 1)
    def _():
        sfin_ref[...] = s_scr[...]


def _gla(proj, lr, wgk, bgk, g_norm, state0, batch, seq, name="gla"):
    h = GLA_HEADS
    dv = proj.shape[1] // 3
    dk = dv // 2
    hk, hv = dk // h, dv // h
    tb = _tile(seq, GLA_ROWS)
    proj3 = proj.reshape(batch, seq, proj.shape[1])
    lr3 = lr.reshape(batch, seq, lr.shape[1])
    in_specs = [
        pl.BlockSpec((None, tb, hk), lambda b, hh, t: (b, t, hh)),
        pl.BlockSpec((None, tb, hk), lambda b, hh, t: (b, t, h + hh)),
        pl.BlockSpec((None, tb, hv), lambda b, hh, t: (b, t, 2 * dk // hv + hh)),
        pl.BlockSpec((None, tb, hv), lambda b, hh, t: (b, t, (2 * dk + dv) // hv + hh)),
        pl.BlockSpec((None, tb, lr.shape[1]), lambda b, hh, t: (b, t, 0)),
        pl.BlockSpec((lr.shape[1], hk), lambda b, hh, t: (0, hh)),
        pl.BlockSpec((1, hk), lambda b, hh, t: (0, hh)),
        pl.BlockSpec((1, hv), lambda b, hh, t: (0, 0)),
    ]
    args = [proj3, proj3, proj3, proj3, lr3, wgk, bgk.reshape(1, dk), g_norm.reshape(1, hv)]
    has_state0 = state0 is not None
    if has_state0:
        in_specs.append(pl.BlockSpec((None, None, hk, hv), lambda b, hh, t: (b, hh, 0, 0)))
        args.append(state0)
    og, s_fin = pl.pallas_call(
        functools.partial(_gla_kernel, has_state0=has_state0, n_chunks=tb // CHUNK),
        grid=(batch, h, seq // tb),
        in_specs=in_specs,
        out_specs=[
            pl.BlockSpec((None, tb, hv), lambda b, hh, t: (b, t, hh)),
            pl.BlockSpec((None, None, hk, hv), lambda b, hh, t: (b, hh, 0, 0)),
        ],
        out_shape=[
            jax.ShapeDtypeStruct((batch, seq, dv), BF16),
            jax.ShapeDtypeStruct((batch, h, hk, hv), F32),
        ],
        scratch_shapes=[pltpu.VMEM((hk, hv), F32)],
        compiler_params=_params(("parallel", "parallel", "arbitrary")),
        name=name,
    )(*args)
    return og.reshape(batch * seq, dv), s_fin


def _attn_kernel(q_ref, k_ref, v_ref, bm_ref, o_ref, *, qb, past, hd, hg):
    tq = q_ref.shape[0]
    left = LEFT_CHUNKS * CHUNK
    wfull = bm_ref.shape[-1]
    for hh in range(hg):
        cs = slice(hh * hd, (hh + 1) * hd)
        for i in range(tq // qb):
            rows = slice(i * qb, (i + 1) * qb)
            k0 = max(0, i * qb + past - left)
            k1 = i * qb + past + qb
            s = lax.dot_general(q_ref[rows, cs], k_ref[k0:k1, cs], (((1,), (1,)), ((), ())),
                                preferred_element_type=F32)
            s = s + bm_ref[hh, :, wfull - (k1 - k0):]
            e = jnp.exp2(s - jnp.max(s, axis=-1, keepdims=True))
            o = jnp.dot(e.astype(BF16), v_ref[k0:k1, cs], preferred_element_type=F32)
            o_ref[rows, cs] = (o / jnp.sum(e, axis=-1, keepdims=True)).astype(o_ref.dtype)


def _attention(q, kv_k, kv_v, k_col0, v_col0, bm, past, name="attn"):
    batch, tq, d = q.shape
    tk = kv_k.shape[1]
    hd = d // ATT_HEADS
    hg = ATT_HEAD_GROUP
    gw = hg * hd
    qb = bm.shape[1]
    return pl.pallas_call(
        functools.partial(_attn_kernel, qb=qb, past=past, hd=hd, hg=hg),
        grid=(batch, ATT_HEADS // hg),
        in_specs=[
            pl.BlockSpec((None, tq, gw), lambda b, g: (b, 0, g)),
            pl.BlockSpec((None, tk, gw), lambda b, g: (b, 0, k_col0 // gw + g)),
            pl.BlockSpec((None, tk, gw), lambda b, g: (b, 0, v_col0 // gw + g)),
            pl.BlockSpec((hg, qb, bm.shape[2]), lambda b, g: (g, 0, 0)),
        ],
        out_specs=pl.BlockSpec((None, tq, gw), lambda b, g: (b, 0, g)),
        out_shape=jax.ShapeDtypeStruct((batch, tq, d), BF16),
        compiler_params=_params(("parallel", "parallel")),
        name=name,
    )(q, kv_k, kv_v, bm)


def _bias_mask(table, qb):
    left = LEFT_CHUNKS * CHUNK
    h = table.shape[0]
    t = table.astype(F32) * LOG2E
    sat = left + 2 * qb
    ext = jnp.concatenate([jnp.broadcast_to(t[:, :1], (h, sat)), t, jnp.broadcast_to(t[:, -1:], (h, sat))], axis=1)
    top = left + qb - 1
    n_w = left + 2 * qb - 1
    start = ext.shape[1] - 1 - (top + MAX_REL + sat)
    w = ext[:, ::-1][:, start:start + n_w]
    skew = jnp.tile(w, (1, qb + 1))[:, :qb * (n_w + 1)].reshape(h, qb, n_w + 1)
    bias = skew[:, ::-1, :left + qb]
    r = jnp.arange(qb)[:, None]
    c = jnp.arange(left + qb)[None, :]
    qc = r // CHUNK
    kc = c // CHUNK - LEFT_CHUNKS
    allowed = (kc <= qc) & (kc >= qc - LEFT_CHUNKS)
    return jnp.where(allowed[None], bias, NEG_INF)


def _trunk(x, state0, cache_k, cache_v, w, past):
    batch, seq, d = x.shape
    depth = w["norm_mix"].shape[0]
    n_a = depth // 2
    dk = d // 2
    x = x.reshape(batch * seq, d)
    states = []
    kv32 = None
    kv16 = None
    qb = _tile(seq, ATT_QBLOCK)
    for layer in range(depth):
        if layer < n_a:
            proj, lr = _norm_matmul(x, w["norm_mix"][layer], w["gla_w_main"][layer], [BF16],
                                    w_side=w["gla_w_lr"][layer], name="gla_in")
            og, s_fin = _gla(proj, lr,
                             w["gla_w_gk"][layer], w["gla_b_gk"][layer], w["gla_g_norm"][layer],
                             None if state0 is None else state0[layer], batch, seq)
            states.append(s_fin)
            x = _matmul_res(og, w["gla_w_out"][layer], x, name="gla_out")
        else:
            j = layer - n_a
            if layer == n_a:
                kv32, kv16 = _norm_matmul(x, w["norm_kv"], w["w_kv"], [F32, BF16], name="kv_proj")
            (q16,) = _norm_matmul(x, w["norm_mix"][layer], w["att_w_q"][j], [BF16],
                                  out_scale=(d // ATT_HEADS) ** -0.5 * LOG2E, name="q_proj")
            q3 = q16.reshape(batch, seq, d)
            bm = _bias_mask(w["att_rel_bias"][j], qb)
            if cache_k is None:
                kv3 = kv16.reshape(batch, seq, 2 * d)
                o = _attention(q3, kv3, kv3, 0, d, bm, past)
            else:
                k3 = jnp.concatenate([cache_k.reshape(batch, -1, d).astype(BF16),
                                      kv16[:, :d].reshape(batch, seq, d)], axis=1)
                v3 = jnp.concatenate([cache_v.reshape(batch, -1, d).astype(BF16),
                                      kv16[:, d:].reshape(batch, seq, d)], axis=1)
                o = _attention(q3, k3, v3, 0, 0, bm, past)
            x = _matmul_res(o.reshape(batch * seq, d), w["att_w_out"][j], x, name="att_out")
        g_final = w["norm_final"] if layer == depth - 1 else None
        x = _mlp(x, w["norm_ffn"][layer], w["w_ff1"][layer], w["w_ff2"][layer], g_final)
    hd = d // ATT_HEADS
    k = kv32[:, :d].reshape(batch, seq, ATT_HEADS, hd)
    v = kv32[:, d:].reshape(batch, seq, ATT_HEADS, hd)
    return x.reshape(batch, seq, d), jnp.stack(states), k, v


def kernel(x_prompt, x_sample, state_gla, cache_k, cache_v, norm_mix, norm_ffn, w_ff1, w_ff2, gla_w_in, gla_w_gk, gla_b_gk, gla_g_norm, gla_w_out, norm_kv, w_kv, att_w_q, att_rel_bias, att_w_out, norm_final):
    d = x_prompt.shape[-1]
    dk = d // 2
    n_main = 2 * dk + 2 * d
    rank = gla_w_in.shape[-1] - n_main
    pad = LANES - rank
    w = {
        "norm_mix": norm_mix, "norm_ffn": norm_ffn, "norm_kv": norm_kv, "norm_final": norm_final,
        "w_ff1": w_ff1.astype(BF16), "w_ff2": w_ff2.astype(BF16),
        "gla_w_main": gla_w_in[:, :, :n_main].astype(BF16),
        "gla_w_lr": jnp.pad(gla_w_in[:, :, n_main:], ((0, 0), (0, 0), (0, pad))).astype(BF16),
        "gla_w_gk": jnp.pad(gla_w_gk, ((0, 0), (0, pad), (0, 0))).astype(BF16),
        "gla_b_gk": gla_b_gk, "gla_g_norm": gla_g_norm,
        "gla_w_out": gla_w_out.astype(BF16),
        "w_kv": w_kv.astype(BF16), "att_w_q": att_w_q.astype(BF16),
        "att_rel_bias": att_rel_bias, "att_w_out": att_w_out.astype(BF16),
    }
    seq = x_prompt.shape[1]
    keep = min(LEFT_CHUNKS * CHUNK, seq)
    y_p, s_p, k_p, v_p = _trunk(x_prompt, None, None, None, w, 0)
    y_s, s_s, k_s, v_s = _trunk(x_sample, state_gla, cache_k, cache_v, w, cache_k.shape[1])
    return (y_p, y_s, s_p, k_p[:, seq - keep:], v_p[:, seq - keep:], s_s, k_s, v_s)
```

```python
import functools

import jax
import jax.numpy as jnp
from jax import lax
from jax.experimental import pallas as pl
from jax.experimental.pallas import tpu as pltpu

F32 = jnp.float32
BF16 = jnp.bfloat16

CHUNK = 64
GLA_HEADS = 4
GLA_SUB = 16
GLA_SUB_SHIFT = GLA_SUB.bit_length() - 1
GLA_GATE_NORM = 16.0
GLA_SINGLE_REF_LOG2_RANGE = 64.0
ATT_HEADS = 16
LEFT_CHUNKS = 8
MAX_REL = 128
EPS = 1e-6
NEG_INF = -1e30
LOG2E = 1.4426950408889634

LANES = 128
VMEM_LIMIT_BYTES = 56 * 2**20
ROW_TILE = 1024
NORM_ROWS = 256
ATT_QBLOCK = 4 * CHUNK
ATT_HEAD_GROUP = 2
GLA_ROWS = 8 * CHUNK
SUBLANES = 8


def _tile(n, pref):
    if n <= pref:
        return n
    t = pref
    while n % t:
        t //= 2
    return t


def _params(sem):
    return pltpu.CompilerParams(dimension_semantics=sem, vmem_limit_bytes=VMEM_LIMIT_BYTES)


def _rmsnorm_rows(x_ref, g_ref, h_ref):
    rows = x_ref.shape[0]
    rc = _tile(rows, NORM_ROWS)

    def body(r, carry):
        sl = pl.ds(pl.multiple_of(r * rc, rc), rc)
        x = x_ref[sl, :]
        ms = jnp.mean(x * x, axis=-1, keepdims=True)
        h_ref[sl, :] = (x * lax.rsqrt(ms + EPS) * g_ref[...]).astype(h_ref.dtype)
        return carry

    lax.fori_loop(0, rows // rc, body, 0)


def _norm_matmul_kernel(*refs, n_w, has_side, out_scale):
    x_ref, g_ref = refs[:2]
    w_refs = refs[2:2 + n_w]
    pos = 2 + n_w
    ws_ref = None
    if has_side:
        ws_ref = refs[pos]
        pos += 1
    out_refs = refs[pos:pos + n_w]
    pos += n_w
    side_ref = None
    if has_side:
        side_ref = refs[pos]
        pos += 1
    h_ref = refs[pos]

    @pl.when(pl.program_id(1) == 0)
    def _():
        _rmsnorm_rows(x_ref, g_ref, h_ref)
        if has_side:
            side_ref[...] = jnp.dot(h_ref[...], ws_ref[...], preferred_element_type=F32)

    for w_ref, o_ref in zip(w_refs, out_refs):
        y = jnp.dot(h_ref[...], w_ref[...], preferred_element_type=F32)
        if out_scale is not None:
            y = y * out_scale
        o_ref[...] = y.astype(o_ref.dtype)


def _norm_matmul(x, g, ws, out_dtype, w_side=None, out_scale=None, row_tiles=None, name="norm_matmul"):
    m, d = x.shape
    n = ws[0].shape[1]
    if row_tiles is None:
        tm = _tile(m, ROW_TILE)
        n_tiles, block_of_tile = m // tm, lambda i: i
    else:
        tm, n_tiles, block_of_tile = row_tiles
    tn = _tile(n, 1024)
    has_side = w_side is not None
    in_specs = [
        pl.BlockSpec((tm, d), lambda i, j: (block_of_tile(i), 0)),
        pl.BlockSpec((1, d), lambda i, j: (0, 0)),
    ] + [pl.BlockSpec((d, tn), lambda i, j: (0, j)) for _ in ws]
    args = [x, g.reshape(1, d)] + list(ws)
    out_specs = [pl.BlockSpec((tm, tn), lambda i, j: (i, j)) for _ in ws]
    out_shape = [jax.ShapeDtypeStruct((n_tiles * tm, n), out_dtype) for _ in ws]
    if has_side:
        ns = w_side.shape[1]
        in_specs.append(pl.BlockSpec((d, ns), lambda i, j: (0, 0)))
        args.append(w_side)
        out_specs.append(pl.BlockSpec((tm, ns), lambda i, j: (i, 0)))
        out_shape.append(jax.ShapeDtypeStruct((n_tiles * tm, ns), F32))
    return pl.pallas_call(
        functools.partial(_norm_matmul_kernel, n_w=len(ws), has_side=has_side, out_scale=out_scale),
        grid=(n_tiles, n // tn),
        in_specs=in_specs,
        out_specs=out_specs,
        out_shape=out_shape,
        scratch_shapes=[pltpu.VMEM((tm, d), BF16)],
        compiler_params=_params(("parallel", "arbitrary")),
        name=name,
    )(*args)


def _matmul_res_kernel(a_ref, w_ref, x_ref, o_ref):
    o_ref[...] = x_ref[...] + jnp.dot(a_ref[...], w_ref[...], preferred_element_type=F32)


def _matmul_res(a, w, x, name="matmul_res"):
    m, k = a.shape
    n = w.shape[1]
    tm = _tile(m, ROW_TILE)
    tn = _tile(n, 1024)
    return pl.pallas_call(
        _matmul_res_kernel,
        grid=(m // tm, n // tn),
        in_specs=[
            pl.BlockSpec((tm, k), lambda i, j: (i, 0)),
            pl.BlockSpec((k, tn), lambda i, j: (0, j)),
            pl.BlockSpec((tm, tn), lambda i, j: (i, j)),
        ],
        out_specs=pl.BlockSpec((tm, tn), lambda i, j: (i, j)),
        out_shape=jax.ShapeDtypeStruct((m, n), F32),
        compiler_params=_params(("parallel", "arbitrary")),
        name=name,
    )(a, w, x)


def _mlp_kernel(*refs, final_norm):
    if final_norm:
        x_ref, g_ref, w1_ref, w2_ref, gf_ref, o_ref, h_ref = refs
    else:
        x_ref, g_ref, w1_ref, w2_ref, o_ref, h_ref = refs
        gf_ref = None
    f = pl.program_id(1)

    @pl.when(f == 0)
    def _():
        _rmsnorm_rows(x_ref, g_ref, h_ref)
        o_ref[...] = x_ref[...]

    a = jnp.dot(h_ref[...], w1_ref[...], preferred_element_type=F32)
    a = jnp.maximum(a, 0.0)
    a = (a * a).astype(BF16)
    o_ref[...] += jnp.dot(a, w2_ref[...], preferred_element_type=F32)

    if final_norm:
        @pl.when(f == pl.num_programs(1) - 1)
        def _():
            rows = o_ref.shape[0]
            rc = _tile(rows, NORM_ROWS)

            def body(r, carry):
                sl = pl.ds(pl.multiple_of(r * rc, rc), rc)
                y = o_ref[sl, :]
                ms = jnp.mean(y * y, axis=-1, keepdims=True)
                o_ref[sl, :] = y * lax.rsqrt(ms + EPS) * gf_ref[...]
                return carry

            lax.fori_loop(0, rows // rc, body, 0)


def _mlp(x, g, w1, w2, g_final=None, name="mlp"):
    m, d = x.shape
    ff = w1.shape[1]
    tm = _tile(m, ROW_TILE)
    tf = _tile(ff, 512)
    final_norm = g_final is not None
    in_specs = [
        pl.BlockSpec((tm, d), lambda i, f: (i, 0)),
        pl.BlockSpec((1, d), lambda i, f: (0, 0)),
        pl.BlockSpec((d, tf), lambda i, f: (0, f)),
        pl.BlockSpec((tf, d), lambda i, f: (f, 0)),
    ]
    args = [x, g.reshape(1, d), w1, w2]
    if final_norm:
        in_specs.append(pl.BlockSpec((1, d), lambda i, f: (0, 0)))
        args.append(g_final.reshape(1, d))
    return pl.pallas_call(
        functools.partial(_mlp_kernel, final_norm=final_norm),
        grid=(m // tm, ff // tf),
        in_specs=in_specs,
        out_specs=pl.BlockSpec((tm, d), lambda i, f: (i, 0)),
        out_shape=jax.ShapeDtypeStruct((m, d), F32),
        scratch_shapes=[pltpu.VMEM((tm, d), BF16)],
        compiler_params=_params(("parallel", "arbitrary")),
        name=name,
    )(*args)


def _log2_gates(lr, wgk, bgk):
    z = jnp.dot(lr.astype(BF16), wgk, preferred_element_type=F32) + bgk
    return (jnp.minimum(z, 0.0) - jnp.log(1.0 + jnp.exp(-jnp.abs(z)))) * (LOG2E / GLA_GATE_NORM)


def _chunk_cumsums(g, n_chunks):
    c = CHUNK
    row = lax.broadcasted_iota(jnp.int32, (c, c), 0)
    col = lax.broadcasted_iota(jnp.int32, (c, c), 1)
    tri = (col <= row).astype(BF16)
    g_hi = g.astype(BF16)
    g_r = g - g_hi.astype(F32)
    g_mid = g_r.astype(BF16)
    g_lo = (g_r - g_mid.astype(F32)).astype(BF16)
    out = []
    for ci in range(n_chunks):
        r = slice(ci * c, (ci + 1) * c)
        out.append(jnp.dot(tri, g_hi[r], preferred_element_type=F32)
                   + jnp.dot(tri, g_mid[r], preferred_element_type=F32)
                   + jnp.dot(tri, g_lo[r], preferred_element_type=F32))
    return out


def _gla_offdiag(q, k, b):
    c = q.shape[0]
    a_rows = [jnp.zeros((GLA_SUB, c), F32)]
    for l in range(1, c // GLA_SUB):
        lo = l * GLA_SUB
        ref = b[lo - 1:lo, :]
        q_ref = q[lo:lo + GLA_SUB, :] * jnp.exp2(b[lo:lo + GLA_SUB, :] - ref)
        k_ref = k * jnp.exp2(jnp.minimum(ref - b, 0.0))
        a_rows.append(lax.dot_general(q_ref.astype(BF16), k_ref.astype(BF16),
                                      (((1,), (1,)), ((), ())), preferred_element_type=F32))
    return jnp.concatenate(a_rows, axis=0)


def _gla_diag(q, k, b):
    c, hk = q.shape
    nsub = c // GLA_SUB
    q3 = q.reshape(nsub, GLA_SUB, hk)
    k3 = k.reshape(nsub, GLA_SUB, hk)
    b3 = b.reshape(nsub, GLA_SUB, hk)
    col3 = lax.broadcasted_iota(jnp.int32, (nsub, SUBLANES, c), 2)
    blk3 = lax.broadcasted_iota(jnp.int32, (nsub, SUBLANES, c), 0) * GLA_SUB
    top = jnp.zeros((nsub, SUBLANES, c), F32)
    bot = jnp.zeros((nsub, GLA_SUB - SUBLANES, c), F32)
    for e in range(GLA_SUB):
        lo = 0 if e < SUBLANES else SUBLANES
        k_e = jnp.broadcast_to(k3[:, e:e + 1, :], (nsub, GLA_SUB - lo, hk))
        b_e = jnp.broadcast_to(b3[:, e:e + 1, :], (nsub, GLA_SUB - lo, hk))
        t = q3[:, lo:, :] * k_e * jnp.exp2(jnp.minimum(b3[:, lo:, :] - b_e, 0.0))
        ts = jnp.sum(t, axis=-1, keepdims=True)
        hit = col3 == blk3 + e
        if lo == 0:
            top = jnp.where(hit, ts[:, :SUBLANES, :], top)
            bot = jnp.where(hit, ts[:, SUBLANES:, :], bot)
        else:
            bot = jnp.where(hit, ts, bot)
    return jnp.concatenate([top, bot], axis=1).reshape(c, c)


def _gla_kernel(*refs, has_state0, n_chunks):
    if has_state0:
        q_ref, k_ref, v_ref, gate_ref, lr_ref, wgk_ref, bgk_ref, gn_ref, s0_ref, o_ref, sfin_ref, s_scr = refs
    else:
        q_ref, k_ref, v_ref, gate_ref, lr_ref, wgk_ref, bgk_ref, gn_ref, o_ref, sfin_ref, s_scr = refs
        s0_ref = None
    t = pl.program_id(2)
    hk = q_ref.shape[-1]
    scale = hk ** -0.5
    c = CHUNK

    @pl.when(t == 0)
    def _():
        if has_state0:
            s_scr[...] = s0_ref[...]
        else:
            s_scr[...] = jnp.zeros_like(s_scr)

    bs = _chunk_cumsums(_log2_gates(lr_ref[...], wgk_ref[...], bgk_ref[...]), n_chunks)
    row = lax.broadcasted_iota(jnp.int32, (c, c), 0)
    col = lax.broadcasted_iota(jnp.int32, (c, c), 1)
    row_blk = jnp.right_shift(row, GLA_SUB_SHIFT)
    col_blk = jnp.right_shift(col, GLA_SUB_SHIFT)
    below = col_blk < row_blk
    on_diag = (col_blk == row_blk) & (col <= row)

    def prepare(ci, single_ref):
        r = slice(ci * c, (ci + 1) * c)
        q = q_ref[r, :].astype(F32) * scale
        k = k_ref[r, :].astype(F32)
        b = bs[ci]
        b_end = b[c - 1:c, :]
        decay_col = jnp.transpose(jnp.broadcast_to(jnp.exp2(b_end), (LANES, hk)))[:, :1]
        k_dec = (k * jnp.exp2(b_end - b)).astype(BF16)
        if single_ref:
            q_up = (q * jnp.exp2(b - b_end)).astype(BF16)
            scores = lax.dot_general(q_up, k_dec, (((1,), (1,)), ((), ())), preferred_element_type=F32)
        else:
            scores = _gla_offdiag(q, k, b)
        return dict(q=q, k=k, b=b, q_dec=(q * jnp.exp2(b)).astype(BF16), k_dec=k_dec,
                    decay_col=decay_col, scores=scores)

    def finish(ci, o):
        r = slice(ci * c, (ci + 1) * c)
        ms = jnp.mean(o * o, axis=-1, keepdims=True)
        y = o * lax.rsqrt(ms + EPS) * gn_ref[...]
        gate = gate_ref[r, :].astype(F32)
        o_ref[r, :] = (y * (gate * (1.0 / (1.0 + jnp.exp(-gate))))).astype(o_ref.dtype)

    def run(single_ref):
        s = s_scr[...]
        cur = prepare(0, single_ref)
        prev_out = None
        prev_upd = None
        for ci in range(n_chunks):
            if prev_upd is not None:
                s = prev_upd[0] * s + prev_upd[1]
            o_inter = jnp.dot(cur["q_dec"], s.astype(BF16), preferred_element_type=F32)
            if prev_out is not None:
                finish(ci - 1, prev_out[0] + prev_out[1])
            if single_ref:
                a = jnp.where(col <= row, cur["scores"], 0.0).astype(BF16)
            else:
                a_diag = _gla_diag(cur["q"], cur["k"], cur["b"])
                a = jnp.where(below, cur["scores"], jnp.where(on_diag, a_diag, 0.0)).astype(BF16)
            v16 = v_ref[ci * c:(ci + 1) * c, :]
            prev_out = (jnp.dot(a, v16, preferred_element_type=F32), o_inter)
            prev_upd = (cur["decay_col"],
                        lax.dot_general(cur["k_dec"], v16, (((0,), (0,)), ((), ())), preferred_element_type=F32))
            if ci + 1 < n_chunks:
                cur = prepare(ci + 1, single_ref)
        s_scr[...] = prev_upd[0] * s + prev_upd[1]
        finish(n_chunks - 1, prev_out[0] + prev_out[1])

    total = bs[0][c - 1:c, :]
    for b in bs[1:]:
        total = jnp.minimum(total, b[c - 1:c, :])
    mild = jnp.min(total) >= -GLA_SINGLE_REF_LOG2_RANGE

    @pl.when(mild)
    def _():
        run(True)

    @pl.when(jnp.logical_not(mild))
    def _():
        run(False)

    @pl.when(t == pl.num_programs(2) - 1)
    def _():
        sfin_ref[...] = s_scr[...]


def _gla(proj, lr, wgk, bgk, g_norm, state0, batch, seq, name="gla"):
    h = GLA_HEADS
    dv = proj.shape[1] // 3
    dk = dv // 2
    hk, hv = dk // h, dv // h
    tb = _tile(seq, GLA_ROWS)
    proj3 = proj.reshape(batch, seq, proj.shape[1])
    lr3 = lr.reshape(batch, seq, lr.shape[1])
    in_specs = [
        pl.BlockSpec((None, tb, hk), lambda b, hh, t: (b, t, hh)),
        pl.BlockSpec((None, tb, hk), lambda b, hh, t: (b, t, h + hh)),
        pl.BlockSpec((None, tb, hv), lambda b, hh, t: (b, t, 2 * dk // hv + hh)),
        pl.BlockSpec((None, tb, hv), lambda b, hh, t: (b, t, (2 * dk + dv) // hv + hh)),
        pl.BlockSpec((None, tb, lr.shape[1]), lambda b, hh, t: (b, t, 0)),
        pl.BlockSpec((lr.shape[1], hk), lambda b, hh, t: (0, hh)),
        pl.BlockSpec((1, hk), lambda b, hh, t: (0, hh)),
        pl.BlockSpec((1, hv), lambda b, hh, t: (0, 0)),
    ]
    args = [proj3, proj3, proj3, proj3, lr3, wgk, bgk.reshape(1, dk), g_norm.reshape(1, hv)]
    has_state0 = state0 is not None
    if has_state0:
        in_specs.append(pl.BlockSpec((None, None, hk, hv), lambda b, hh, t: (b, hh, 0, 0)))
        args.append(state0)
    og, s_fin = pl.pallas_call(
        functools.partial(_gla_kernel, has_state0=has_state0, n_chunks=tb // CHUNK),
        grid=(batch, h, seq // tb),
        in_specs=in_specs,
        out_specs=[
            pl.BlockSpec((None, tb, hv), lambda b, hh, t: (b, t, hh)),
            pl.BlockSpec((None, None, hk, hv), lambda b, hh, t: (b, hh, 0, 0)),
        ],
        out_shape=[
            jax.ShapeDtypeStruct((batch, seq, dv), BF16),
            jax.ShapeDtypeStruct((batch, h, hk, hv), F32),
        ],
        scratch_shapes=[pltpu.VMEM((hk, hv), F32)],
        compiler_params=_params(("parallel", "parallel", "arbitrary")),
        name=name,
    )(*args)
    return og.reshape(batch * seq, dv), s_fin


def _attn_kernel(q_ref, k_ref, v_ref, bm_ref, o_ref, *, qb, past, hd, hg):
    tq = q_ref.shape[0]
    left = LEFT_CHUNKS * CHUNK
    wfull = bm_ref.shape[-1]
    blocks = [(hh, i) for hh in range(hg) for i in range(tq // qb)]

    def window(i):
        return max(0, i * qb + past - left), i * qb + past + qb

    def scores(hh, i):
        k0, k1 = window(i)
        cs = slice(hh * hd, (hh + 1) * hd)
        s = lax.dot_general(q_ref[i * qb:(i + 1) * qb, cs], k_ref[k0:k1, cs], (((1,), (1,)), ((), ())),
                            preferred_element_type=F32)
        return s + bm_ref[hh, :, wfull - (k1 - k0):]

    def store(hh, i, o, denom):
        o_ref[i * qb:(i + 1) * qb, hh * hd:(hh + 1) * hd] = (o * (1.0 / denom)).astype(o_ref.dtype)

    s_next = scores(*blocks[0])
    pending = None
    for n, (hh, i) in enumerate(blocks):
        s = s_next
        if n + 1 < len(blocks):
            s_next = scores(*blocks[n + 1])
        e = jnp.exp2(s - jnp.max(s, axis=-1, keepdims=True))
        k0, k1 = window(i)
        o = jnp.dot(e.astype(BF16), v_ref[k0:k1, hh * hd:(hh + 1) * hd], preferred_element_type=F32)
        if pending is not None:
            store(*pending)
        pending = (hh, i, o, jnp.sum(e, axis=-1, keepdims=True))
    store(*pending)


def _attention(q, kv_k, kv_v, k_col0, v_col0, bm, past, name="attn"):
    batch, tq, d = q.shape
    tk = kv_k.shape[1]
    hd = d // ATT_HEADS
    hg = ATT_HEAD_GROUP
    gw = hg * hd
    qb = bm.shape[1]
    return pl.pallas_call(
        functools.partial(_attn_kernel, qb=qb, past=past, hd=hd, hg=hg),
        grid=(batch, ATT_HEADS // hg),
        in_specs=[
            pl.BlockSpec((None, tq, gw), lambda b, g: (b, 0, g)),
            pl.BlockSpec((None, tk, gw), lambda b, g: (b, 0, k_col0 // gw + g)),
            pl.BlockSpec((None, tk, gw), lambda b, g: (b, 0, v_col0 // gw + g)),
            pl.BlockSpec((hg, qb, bm.shape[2]), lambda b, g: (g, 0, 0)),
        ],
        out_specs=pl.BlockSpec((None, tq, gw), lambda b, g: (b, 0, g)),
        out_shape=jax.ShapeDtypeStruct((batch, tq, d), BF16),
        compiler_params=_params(("parallel", "parallel")),
        name=name,
    )(q, kv_k, kv_v, bm)


def _bias_mask(table, qb):
    left = LEFT_CHUNKS * CHUNK
    h = table.shape[0]
    t = table.astype(F32) * LOG2E
    sat = left + 2 * qb
    ext = jnp.concatenate([jnp.broadcast_to(t[:, :1], (h, sat)), t, jnp.broadcast_to(t[:, -1:], (h, sat))], axis=1)
    top = left + qb - 1
    n_w = left + 2 * qb - 1
    start = ext.shape[1] - 1 - (top + MAX_REL + sat)
    desc = ext[:, ::-1][:, start:start + n_w]
    w = jnp.concatenate([desc[:, qb - 1:], desc[:, :qb - 1]], axis=1)
    bias = jnp.tile(w, (1, qb))[:, :qb * (n_w - 1)].reshape(h, qb, n_w - 1)[:, :, :left + qb]
    r = jnp.arange(qb)[:, None]
    c = jnp.arange(left + qb)[None, :]
    qc = r // CHUNK
    kc = c // CHUNK - LEFT_CHUNKS
    allowed = (kc <= qc) & (kc >= qc - LEFT_CHUNKS)
    return jnp.where(allowed[None], bias, NEG_INF)


def _trunk(x, state0, cache_k, cache_v, w, past, keep):
    batch, seq, d = x.shape
    depth = w["norm_mix"].shape[0]
    n_a = depth // 2
    x = x.reshape(batch * seq, d)
    states = []
    k_rows = None
    v_rows = None
    kv16 = None
    qb = _tile(seq, ATT_QBLOCK)
    for layer in range(depth):
        if layer < n_a:
            (proj, lr) = _norm_matmul(x, w["norm_mix"][layer], [w["gla_w_main"][layer]], BF16,
                                      w_side=w["gla_w_lr"][layer], name="gla_in")
            og, s_fin = _gla(proj, lr,
                             w["gla_w_gk"][layer], w["gla_b_gk"][layer], w["gla_g_norm"][layer],
                             None if state0 is None else state0[layer], batch, seq)
            states.append(s_fin)
            x = _matmul_res(og, w["gla_w_out"][layer], x, name="gla_out")
        else:
            j = layer - n_a
            if layer == n_a:
                (kv16,) = _norm_matmul(x, w["norm_kv"], [w["w_kv"]], BF16, name="kv_proj")
                if keep == seq:
                    row_tiles = None
                else:
                    assert seq % keep == 0
                    per_seq = seq // keep
                    row_tiles = (keep, batch, lambda i: i * per_seq + per_seq - 1)
                k_rows, v_rows = _norm_matmul(x, w["norm_kv"], [w["w_k"], w["w_v"]], F32,
                                              row_tiles=row_tiles, name="kv_rows")
            (q16,) = _norm_matmul(x, w["norm_mix"][layer], [w["att_w_q"][j]], BF16,
                                  out_scale=(d // ATT_HEADS) ** -0.5 * LOG2E, name="q_proj")
            q3 = q16.reshape(batch, seq, d)
            bm = _bias_mask(w["att_rel_bias"][j], qb)
            if cache_k is None:
                kv3 = kv16.reshape(batch, seq, 2 * d)
                o = _attention(q3, kv3, kv3, 0, d, bm, past)
            else:
                k3 = jnp.concatenate([cache_k.reshape(batch, -1, d).astype(BF16),
                                      kv16[:, :d].reshape(batch, seq, d)], axis=1)
                v3 = jnp.concatenate([cache_v.reshape(batch, -1, d).astype(BF16),
                                      kv16[:, d:].reshape(batch, seq, d)], axis=1)
                o = _attention(q3, k3, v3, 0, 0, bm, past)
            x = _matmul_res(o.reshape(batch * seq, d), w["att_w_out"][j], x, name="att_out")
        g_final = w["norm_final"] if layer == depth - 1 else None
        x = _mlp(x, w["norm_ffn"][layer], w["w_ff1"][layer], w["w_ff2"][layer], g_final)
    hd = d // ATT_HEADS
    k = k_rows.reshape(batch, keep, ATT_HEADS, hd)
    v = v_rows.reshape(batch, keep, ATT_HEADS, hd)
    return x.reshape(batch, seq, d), jnp.stack(states), k, v


def kernel(x_prompt, x_sample, state_gla, cache_k, cache_v, norm_mix, norm_ffn, w_ff1, w_ff2, gla_w_in, gla_w_gk, gla_b_gk, gla_g_norm, gla_w_out, norm_kv, w_kv, att_w_q, att_rel_bias, att_w_out, norm_final):
    d = x_prompt.shape[-1]
    dk = d // 2
    n_main = 2 * dk + 2 * d
    rank = gla_w_in.shape[-1] - n_main
    pad = LANES - rank
    w = {
        "norm_mix": norm_mix, "norm_ffn": norm_ffn, "norm_kv": norm_kv, "norm_final": norm_final,
        "w_ff1": w_ff1.astype(BF16), "w_ff2": w_ff2.astype(BF16),
        "gla_w_main": gla_w_in[:, :, :n_main].astype(BF16),
        "gla_w_lr": jnp.pad(gla_w_in[:, :, n_main:], ((0, 0), (0, 0), (0, pad))).astype(BF16),
        "gla_w_gk": jnp.pad(gla_w_gk, ((0, 0), (0, pad), (0, 0))).astype(BF16),
        "gla_b_gk": gla_b_gk, "gla_g_norm": gla_g_norm,
        "gla_w_out": gla_w_out.astype(BF16),
        "w_kv": w_kv.astype(BF16), "w_k": w_kv[:, :d].astype(BF16), "w_v": w_kv[:, d:].astype(BF16),
        "att_w_q": att_w_q.astype(BF16),
        "att_rel_bias": att_rel_bias, "att_w_out": att_w_out.astype(BF16),
    }
    seq = x_prompt.shape[1]
    keep = min(LEFT_CHUNKS * CHUNK, seq)
    y_p, s_p, k_p, v_p = _trunk(x_prompt, None, None, None, w, 0, keep)
    y_s, s_s, k_s, v_s = _trunk(x_sample, state_gla, cache_k, cache_v, w, cache_k.shape[1], x_sample.shape[1])
    return (y_p, y_s, s_p, k_p, v_p, s_s, k_s, v_s)
```

```python
import functools

import jax
import jax.numpy as jnp
from jax import lax
from jax.experimental import pallas as pl
from jax.experimental.pallas import tpu as pltpu

F32 = jnp.float32
BF16 = jnp.bfloat16

CHUNK = 64
GLA_HEADS = 4
GLA_SUB = 16
GLA_SUB_SHIFT = GLA_SUB.bit_length() - 1
GLA_GATE_NORM = 16.0
GLA_SINGLE_REF_LOG2_RANGE = 64.0
ATT_HEADS = 16
LEFT_CHUNKS = 8
PAST_LEN = 2048
MAX_REL = 128
EPS = 1e-6
NEG_INF = -1e30
LOG2E = 1.4426950408889634

LANES = 128
VMEM_LIMIT_BYTES = 60 * 2**20
ROW_TILE = 1024
RES_ROW_TILE = 512
FF_TILE = 1024
NORM_ROWS = 256
ATT_QBLOCK = 4 * CHUNK
ATT_HEAD_GROUP = 2
GLA_ROWS = 8 * CHUNK
SUBLANES = 8


def _tile(n, pref):
    if n <= pref:
        return n
    t = pref
    while n % t:
        t //= 2
    return t


def _params(sem):
    return pltpu.CompilerParams(dimension_semantics=sem, vmem_limit_bytes=VMEM_LIMIT_BYTES)


def _rmsnorm_rows(x_ref, g_ref, h_ref):
    rows = x_ref.shape[0]
    rc = _tile(rows, NORM_ROWS)

    def body(r, carry):
        sl = pl.ds(pl.multiple_of(r * rc, rc), rc)
        x = x_ref[sl, :]
        ms = jnp.mean(x * x, axis=-1, keepdims=True)
        h_ref[sl, :] = (x * lax.rsqrt(ms + EPS) * g_ref[...]).astype(h_ref.dtype)
        return carry

    lax.fori_loop(0, rows // rc, body, 0)


def _norm_matmul_kernel(*refs, n_w, has_side, out_scale):
    x_ref, g_ref = refs[:2]
    w_refs = refs[2:2 + n_w]
    pos = 2 + n_w
    ws_ref = None
    if has_side:
        ws_ref = refs[pos]
        pos += 1
    out_refs = refs[pos:pos + n_w]
    pos += n_w
    side_ref = None
    if has_side:
        side_ref = refs[pos]
        pos += 1
    h_ref = refs[pos]

    @pl.when(pl.program_id(1) == 0)
    def _():
        _rmsnorm_rows(x_ref, g_ref, h_ref)
        if has_side:
            side_ref[...] = jnp.dot(h_ref[...], ws_ref[...], preferred_element_type=F32)

    for w_ref, o_ref in zip(w_refs, out_refs):
        y = jnp.dot(h_ref[...], w_ref[...], preferred_element_type=F32)
        if out_scale is not None:
            y = y * out_scale
        o_ref[...] = y.astype(o_ref.dtype)


def _norm_matmul(x, g, ws, out_dtype, w_side=None, out_scale=None, row_tiles=None, col_windows=None,
                 name="norm_matmul"):
    m, d = x.shape
    n, col0 = (ws[0].shape[1], [0] * len(ws)) if col_windows is None else col_windows
    if row_tiles is None:
        tm = _tile(m, ROW_TILE)
        n_tiles, block_of_tile = m // tm, lambda i: i
    else:
        tm, n_tiles, block_of_tile = row_tiles
    tn = _tile(n, 1024)
    assert all(c0 % tn == 0 for c0 in col0)
    has_side = w_side is not None
    in_specs = [
        pl.BlockSpec((tm, d), lambda i, j: (block_of_tile(i), 0)),
        pl.BlockSpec((1, d), lambda i, j: (0, 0)),
    ] + [pl.BlockSpec((d, tn), functools.partial(lambda i, j, first: (0, first // tn + j), first=c0)) for c0 in col0]
    args = [x, g.reshape(1, d)] + list(ws)
    out_specs = [pl.BlockSpec((tm, tn), lambda i, j: (i, j)) for _ in ws]
    out_shape = [jax.ShapeDtypeStruct((n_tiles * tm, n), out_dtype) for _ in ws]
    if has_side:
        ns = w_side.shape[1]
        in_specs.append(pl.BlockSpec((d, ns), lambda i, j: (0, 0)))
        args.append(w_side)
        out_specs.append(pl.BlockSpec((tm, ns), lambda i, j: (i, 0)))
        out_shape.append(jax.ShapeDtypeStruct((n_tiles * tm, ns), F32))
    return pl.pallas_call(
        functools.partial(_norm_matmul_kernel, n_w=len(ws), has_side=has_side, out_scale=out_scale),
        grid=(n_tiles, n // tn),
        in_specs=in_specs,
        out_specs=out_specs,
        out_shape=out_shape,
        scratch_shapes=[pltpu.VMEM((tm, d), BF16)],
        compiler_params=_params(("parallel", "arbitrary")),
        name=name,
    )(*args)


def _matmul_res_kernel(a_ref, w_ref, x_ref, o_ref):
    o_ref[...] = x_ref[...] + jnp.dot(a_ref[...], w_ref[...], preferred_element_type=F32)


def _matmul_res(a, w, x, name="matmul_res"):
    m, k = a.shape
    n = w.shape[1]
    tm = _tile(m, RES_ROW_TILE)
    return pl.pallas_call(
        _matmul_res_kernel,
        grid=(m // tm,),
        in_specs=[
            pl.BlockSpec((tm, k), lambda i: (i, 0)),
            pl.BlockSpec((k, n), lambda i: (0, 0)),
            pl.BlockSpec((tm, n), lambda i: (i, 0)),
        ],
        out_specs=pl.BlockSpec((tm, n), lambda i: (i, 0)),
        out_shape=jax.ShapeDtypeStruct((m, n), F32),
        compiler_params=_params(("parallel",)),
        name=name,
    )(a, w, x)


def _mlp_kernel(*refs, final_norm):
    if final_norm:
        x_ref, g_ref, w1_ref, w2_ref, gf_ref, o_ref, h_ref = refs
    else:
        x_ref, g_ref, w1_ref, w2_ref, o_ref, h_ref = refs
        gf_ref = None
    f = pl.program_id(1)

    @pl.when(f == 0)
    def _():
        _rmsnorm_rows(x_ref, g_ref, h_ref)
        o_ref[...] = x_ref[...]

    a = jnp.dot(h_ref[...], w1_ref[...], preferred_element_type=F32)
    a = jnp.maximum(a, 0.0)
    a = (a * a).astype(BF16)
    o_ref[...] += jnp.dot(a, w2_ref[...], preferred_element_type=F32)

    if final_norm:
        @pl.when(f == pl.num_programs(1) - 1)
        def _():
            rows = o_ref.shape[0]
            rc = _tile(rows, NORM_ROWS)

            def body(r, carry):
                sl = pl.ds(pl.multiple_of(r * rc, rc), rc)
                y = o_ref[sl, :]
                ms = jnp.mean(y * y, axis=-1, keepdims=True)
                o_ref[sl, :] = y * lax.rsqrt(ms + EPS) * gf_ref[...]
                return carry

            lax.fori_loop(0, rows // rc, body, 0)


def _mlp(x, g, w1, w2, g_final=None, name="mlp"):
    m, d = x.shape
    ff = w1.shape[1]
    tm = _tile(m, ROW_TILE)
    tf = _tile(ff, FF_TILE)
    final_norm = g_final is not None
    in_specs = [
        pl.BlockSpec((tm, d), lambda i, f: (i, 0)),
        pl.BlockSpec((1, d), lambda i, f: (0, 0)),
        pl.BlockSpec((d, tf), lambda i, f: (0, f)),
        pl.BlockSpec((tf, d), lambda i, f: (f, 0)),
    ]
    args = [x, g.reshape(1, d), w1, w2]
    if final_norm:
        in_specs.append(pl.BlockSpec((1, d), lambda i, f: (0, 0)))
        args.append(g_final.reshape(1, d))
    return pl.pallas_call(
        functools.partial(_mlp_kernel, final_norm=final_norm),
        grid=(m // tm, ff // tf),
        in_specs=in_specs,
        out_specs=pl.BlockSpec((tm, d), lambda i, f: (i, 0)),
        out_shape=jax.ShapeDtypeStruct((m, d), F32),
        scratch_shapes=[pltpu.VMEM((tm, d), BF16)],
        compiler_params=_params(("parallel", "arbitrary")),
        name=name,
    )(*args)


def _log2_gates(lr, wgk, bgk):
    z = jnp.dot(lr.astype(BF16), wgk, preferred_element_type=F32) + bgk
    return (jnp.minimum(z, 0.0) - jnp.log(1.0 + jnp.exp(-jnp.abs(z)))) * (LOG2E / GLA_GATE_NORM)


def _chunk_cumsums(g, n_chunks):
    c = CHUNK
    row = lax.broadcasted_iota(jnp.int32, (c, c), 0)
    col = lax.broadcasted_iota(jnp.int32, (c, c), 1)
    tri = (col <= row).astype(BF16)
    g_hi = g.astype(BF16)
    g_r = g - g_hi.astype(F32)
    g_mid = g_r.astype(BF16)
    g_lo = (g_r - g_mid.astype(F32)).astype(BF16)
    out = []
    for ci in range(n_chunks):
        r = slice(ci * c, (ci + 1) * c)
        out.append(jnp.dot(tri, g_hi[r], preferred_element_type=F32)
                   + jnp.dot(tri, g_mid[r], preferred_element_type=F32)
                   + jnp.dot(tri, g_lo[r], preferred_element_type=F32))
    return out


def _gla_offdiag(q, k, b):
    c = q.shape[0]
    a_rows = [jnp.zeros((GLA_SUB, c), F32)]
    for l in range(1, c // GLA_SUB):
        lo = l * GLA_SUB
        ref = b[lo - 1:lo, :]
        q_ref = q[lo:lo + GLA_SUB, :] * jnp.exp2(b[lo:lo + GLA_SUB, :] - ref)
        k_ref = k * jnp.exp2(jnp.minimum(ref - b, 0.0))
        a_rows.append(lax.dot_general(q_ref.astype(BF16), k_ref.astype(BF16),
                                      (((1,), (1,)), ((), ())), preferred_element_type=F32))
    return jnp.concatenate(a_rows, axis=0)


def _gla_diag(q, k, b):
    c, hk = q.shape
    nsub = c // GLA_SUB
    q3 = q.reshape(nsub, GLA_SUB, hk)
    k3 = k.reshape(nsub, GLA_SUB, hk)
    b3 = b.reshape(nsub, GLA_SUB, hk)
    col3 = lax.broadcasted_iota(jnp.int32, (nsub, SUBLANES, c), 2)
    blk3 = lax.broadcasted_iota(jnp.int32, (nsub, SUBLANES, c), 0) * GLA_SUB
    top = jnp.zeros((nsub, SUBLANES, c), F32)
    bot = jnp.zeros((nsub, GLA_SUB - SUBLANES, c), F32)
    for e in range(GLA_SUB):
        lo = 0 if e < SUBLANES else SUBLANES
        k_e = jnp.broadcast_to(k3[:, e:e + 1, :], (nsub, GLA_SUB - lo, hk))
        b_e = jnp.broadcast_to(b3[:, e:e + 1, :], (nsub, GLA_SUB - lo, hk))
        t = q3[:, lo:, :] * k_e * jnp.exp2(jnp.minimum(b3[:, lo:, :] - b_e, 0.0))
        ts = jnp.sum(t, axis=-1, keepdims=True)
        hit = col3 == blk3 + e
        if lo == 0:
            top = jnp.where(hit, ts[:, :SUBLANES, :], top)
            bot = jnp.where(hit, ts[:, SUBLANES:, :], bot)
        else:
            bot = jnp.where(hit, ts, bot)
    return jnp.concatenate([top, bot], axis=1).reshape(c, c)


def _gla_kernel(*refs, has_state0, n_chunks):
    q_ref, k_ref, v_ref, gate_ref, lr_ref, wgk_ref, bgk_ref, gn_ref = refs[:8]
    s0_ref = refs[8] if has_state0 else None
    o_ref, sfin_ref, s_scr = refs[-3:]
    t = pl.program_id(2)
    hk = q_ref.shape[-1]
    scale = hk ** -0.5
    c = CHUNK

    @pl.when(t == 0)
    def _():
        if has_state0:
            s_scr[...] = s0_ref[...]
        else:
            s_scr[...] = jnp.zeros_like(s_scr)

    g = _log2_gates(lr_ref[...], wgk_ref[...], bgk_ref[...])
    bs = _chunk_cumsums(g, n_chunks)
    row = lax.broadcasted_iota(jnp.int32, (c, c), 0)
    col = lax.broadcasted_iota(jnp.int32, (c, c), 1)
    row_blk = jnp.right_shift(row, GLA_SUB_SHIFT)
    col_blk = jnp.right_shift(col, GLA_SUB_SHIFT)
    below = col_blk < row_blk
    on_diag = (col_blk == row_blk) & (col <= row)

    def prepare(ci, single_ref):
        r = slice(ci * c, (ci + 1) * c)
        q = q_ref[r, :].astype(F32) * scale
        k = k_ref[r, :].astype(F32)
        b = bs[ci]
        b_end = b[c - 1:c, :]
        decay_col = jnp.transpose(jnp.broadcast_to(jnp.exp2(b_end), (LANES, hk)))[:, :1]
        k_dec = (k * jnp.exp2(b_end - b)).astype(BF16)
        if single_ref:
            q_up = (q * jnp.exp2(b - b_end)).astype(BF16)
            scores = lax.dot_general(q_up, k_dec, (((1,), (1,)), ((), ())), preferred_element_type=F32)
        else:
            scores = _gla_offdiag(q, k, b)
        return dict(q=q, k=k, b=b, q_dec=(q * jnp.exp2(b)).astype(BF16), k_dec=k_dec,
                    decay_col=decay_col, scores=scores)

    def finish(ci, o):
        r = slice(ci * c, (ci + 1) * c)
        ms = jnp.mean(o * o, axis=-1, keepdims=True)
        y = o * lax.rsqrt(ms + EPS) * gn_ref[...]
        gate = gate_ref[r, :].astype(F32)
        o_ref[r, :] = (y * (gate * (1.0 / (1.0 + jnp.exp(-gate))))).astype(o_ref.dtype)

    def run(single_ref):
        s = s_scr[...]
        cur = prepare(0, single_ref)
        prev_out = None
        prev_upd = None
        for ci in range(n_chunks):
            if prev_upd is not None:
                s = prev_upd[0] * s + prev_upd[1]
            o_inter = jnp.dot(cur["q_dec"], s.astype(BF16), preferred_element_type=F32)
            if prev_out is not None:
                finish(ci - 1, prev_out[0] + prev_out[1])
            if single_ref:
                a = jnp.where(col <= row, cur["scores"], 0.0).astype(BF16)
            else:
                a_diag = _gla_diag(cur["q"], cur["k"], cur["b"])
                a = jnp.where(below, cur["scores"], jnp.where(on_diag, a_diag, 0.0)).astype(BF16)
            v16 = v_ref[ci * c:(ci + 1) * c, :]
            prev_out = (jnp.dot(a, v16, preferred_element_type=F32), o_inter)
            prev_upd = (cur["decay_col"],
                        lax.dot_general(cur["k_dec"], v16, (((0,), (0,)), ((), ())), preferred_element_type=F32))
            if ci + 1 < n_chunks:
                cur = prepare(ci + 1, single_ref)
        s_scr[...] = prev_upd[0] * s + prev_upd[1]
        finish(n_chunks - 1, prev_out[0] + prev_out[1])

    mild = jnp.min(g) >= -GLA_SINGLE_REF_LOG2_RANGE / c

    @pl.when(mild)
    def _():
        run(True)

    @pl.when(jnp.logical_not(mild))
    def _():
        run(False)

    @pl.when(t == pl.num_programs(2) - 1)
    def _():
        sfin_ref[...] = s_scr[...]


def _gla(proj, lr, wgk, bgk, g_norm, state0, layer, n_layers, states, batch, seq, name="gla"):
    h = GLA_HEADS
    dv = proj.shape[1] // 3
    dk = dv // 2
    hk, hv = dk // h, dv // h
    tb = _tile(seq, GLA_ROWS)
    proj3 = proj.reshape(batch, seq, proj.shape[1])
    lr3 = lr.reshape(batch, seq, lr.shape[1])
    in_specs = [
        pl.BlockSpec((None, tb, hk), lambda b, hh, t: (b, t, hh)),
        pl.BlockSpec((None, tb, hk), lambda b, hh, t: (b, t, h + hh)),
        pl.BlockSpec((None, tb, hv), lambda b, hh, t: (b, t, 2 * dk // hv + hh)),
        pl.BlockSpec((None, tb, hv), lambda b, hh, t: (b, t, (2 * dk + dv) // hv + hh)),
        pl.BlockSpec((None, tb, lr.shape[1]), lambda b, hh, t: (b, t, 0)),
        pl.BlockSpec((lr.shape[1], hk), lambda b, hh, t: (0, hh)),
        pl.BlockSpec((1, hk), lambda b, hh, t: (0, hh)),
        pl.BlockSpec((1, hv), lambda b, hh, t: (0, 0)),
    ]
    args = [proj3, proj3, proj3, proj3, lr3, wgk, bgk.reshape(1, dk), g_norm.reshape(1, hv)]
    state_spec = pl.BlockSpec((None, None, None, hk, hv), lambda b, hh, t: (layer, b, hh, 0, 0))
    has_state0 = state0 is not None
    if has_state0:
        in_specs.append(state_spec)
        args.append(state0)
    aliases = {}
    if states is not None:
        aliases = {len(args): 1}
        in_specs.append(pl.BlockSpec(memory_space=pl.ANY))
        args.append(states)
    og, states = pl.pallas_call(
        functools.partial(_gla_kernel, has_state0=has_state0, n_chunks=tb // CHUNK),
        grid=(batch, h, seq // tb),
        in_specs=in_specs,
        out_specs=[pl.BlockSpec((None, tb, hv), lambda b, hh, t: (b, t, hh)), state_spec],
        out_shape=[
            jax.ShapeDtypeStruct((batch, seq, dv), BF16),
            jax.ShapeDtypeStruct((n_layers, batch, h, hk, hv), F32),
        ],
        scratch_shapes=[pltpu.VMEM((hk, hv), F32)],
        input_output_aliases=aliases,
        compiler_params=_params(("parallel", "parallel", "arbitrary")),
        name=name,
    )(*args)
    return og.reshape(batch * seq, dv), states


def _attn_kernel(*refs, qb, past, hd, hg):
    if past:
        q_ref, k_ref, v_ref, kc_ref, vc_ref, bm_ref, o_ref = refs
    else:
        q_ref, k_ref, v_ref, bm_ref, o_ref = refs
        kc_ref = vc_ref = None
    tq = q_ref.shape[0]
    left = LEFT_CHUNKS * CHUNK
    wfull = bm_ref.shape[-1]
    blocks = [(hh, i) for hh in range(hg) for i in range(tq // qb)]

    def window(i):
        return max(0, i * qb + past - left), i * qb + past + qb

    def rows(new_ref, cache_ref, k0, k1, cs):
        parts = []
        if k0 < past:
            parts.append(cache_ref[k0:min(k1, past), cs].astype(BF16))
        if k1 > past:
            parts.append(new_ref[max(k0, past) - past:k1 - past, cs])
        return parts[0] if len(parts) == 1 else jnp.concatenate(parts, axis=0)

    def scores(hh, i):
        k0, k1 = window(i)
        cs = slice(hh * hd, (hh + 1) * hd)
        s = lax.dot_general(q_ref[i * qb:(i + 1) * qb, cs], rows(k_ref, kc_ref, k0, k1, cs),
                            (((1,), (1,)), ((), ())), preferred_element_type=F32)
        return s + bm_ref[hh, :, wfull - (k1 - k0):]

    def store(hh, i, o, denom):
        o_ref[i * qb:(i + 1) * qb, hh * hd:(hh + 1) * hd] = (o * (1.0 / denom)).astype(o_ref.dtype)

    s_next = scores(*blocks[0])
    pending = None
    for n, (hh, i) in enumerate(blocks):
        s = s_next
        if n + 1 < len(blocks):
            s_next = scores(*blocks[n + 1])
        e = jnp.exp2(s - jnp.max(s, axis=-1, keepdims=True))
        k0, k1 = window(i)
        o = jnp.dot(e.astype(BF16), rows(v_ref, vc_ref, k0, k1, slice(hh * hd, (hh + 1) * hd)),
                    preferred_element_type=F32)
        if pending is not None:
            store(*pending)
        pending = (hh, i, o, jnp.sum(e, axis=-1, keepdims=True))
    store(*pending)


def _attention(q, kv, bm, cache_k=None, cache_v=None, name="attn"):
    batch, tq, d = q.shape
    hd = d // ATT_HEADS
    hg = ATT_HEAD_GROUP
    gw = hg * hd
    qb = bm.shape[1]
    past = 0 if cache_k is None else cache_k.shape[1]
    assert PAST_LEN % CHUNK == 0 and past in (0, min(LEFT_CHUNKS * CHUNK, PAST_LEN))
    in_specs = [
        pl.BlockSpec((None, tq, gw), lambda b, g: (b, 0, g)),
        pl.BlockSpec((None, tq, gw), lambda b, g: (b, 0, g)),
        pl.BlockSpec((None, tq, gw), lambda b, g: (b, 0, d // gw + g)),
    ]
    args = [q, kv, kv]
    if past:
        in_specs += [pl.BlockSpec((None, past, gw), lambda b, g: (b, 0, g))] * 2
        args += [cache_k, cache_v]
    in_specs.append(pl.BlockSpec((hg, qb, bm.shape[2]), lambda b, g: (g, 0, 0)))
    args.append(bm)
    return pl.pallas_call(
        functools.partial(_attn_kernel, qb=qb, past=past, hd=hd, hg=hg),
        grid=(batch, ATT_HEADS // hg),
        in_specs=in_specs,
        out_specs=pl.BlockSpec((None, tq, gw), lambda b, g: (b, 0, g)),
        out_shape=jax.ShapeDtypeStruct((batch, tq, d), BF16),
        compiler_params=_params(("parallel", "parallel")),
        name=name,
    )(*args)


def _bias_mask(table, qb):
    left = LEFT_CHUNKS * CHUNK
    h = table.shape[0]
    t = table.astype(F32) * LOG2E
    sat = left + 2 * qb
    ext = jnp.concatenate([jnp.broadcast_to(t[:, :1], (h, sat)), t, jnp.broadcast_to(t[:, -1:], (h, sat))], axis=1)
    top = left + qb - 1
    n_w = left + 2 * qb - 1
    start = ext.shape[1] - 1 - (top + MAX_REL + sat)
    desc = ext[:, ::-1][:, start:start + n_w]
    w = jnp.concatenate([desc[:, qb - 1:], desc[:, :qb - 1]], axis=1)
    bias = jnp.tile(w, (1, qb))[:, :qb * (n_w - 1)].reshape(h, qb, n_w - 1)[:, :, :left + qb]
    r = jnp.arange(qb)[:, None]
    c = jnp.arange(left + qb)[None, :]
    qc = r // CHUNK
    kc = c // CHUNK - LEFT_CHUNKS
    allowed = (kc <= qc) & (kc >= qc - LEFT_CHUNKS)
    return jnp.where(allowed[None], bias, NEG_INF)


def _trunk(x, state0, cache_k, cache_v, w, keep):
    batch, seq, d = x.shape
    depth = w["norm_mix"].shape[0]
    n_a = depth // 2
    x = x.reshape(batch * seq, d)
    states = None
    k_rows = None
    v_rows = None
    kv16 = None
    qb = _tile(seq, ATT_QBLOCK)
    for layer in range(depth):
        if layer < n_a:
            (proj, lr) = _norm_matmul(x, w["norm_mix"][layer], [w["gla_w_main"][layer]], BF16,
                                      w_side=w["gla_w_lr"][layer], name="gla_in")
            og, states = _gla(proj, lr, w["gla_w_gk"][layer], w["gla_b_gk"][layer], w["gla_g_norm"][layer],
                              state0, layer, n_a, states, batch, seq)
            x = _matmul_res(og, w["gla_w_out"][layer], x, name="gla_out")
        else:
            j = layer - n_a
            if layer == n_a:
                (kv16,) = _norm_matmul(x, w["norm_kv"], [w["w_kv"]], BF16, name="kv_proj")
                if keep == seq:
                    row_tiles = None
                else:
                    assert seq % keep == 0
                    per_seq = seq // keep
                    row_tiles = (keep, batch, lambda i: i * per_seq + per_seq - 1)
                k_rows, v_rows = _norm_matmul(x, w["norm_kv"], [w["w_kv"], w["w_kv"]], F32,
                                              row_tiles=row_tiles, col_windows=(d, [0, d]), name="kv_rows")
            (q16,) = _norm_matmul(x, w["norm_mix"][layer], [w["att_w_q"][j]], BF16,
                                  out_scale=(d // ATT_HEADS) ** -0.5 * LOG2E, name="q_proj")
            q3 = q16.reshape(batch, seq, d)
            bm = _bias_mask(w["att_rel_bias"][j], qb)
            kv3 = kv16.reshape(batch, seq, 2 * d)
            if cache_k is None:
                o = _attention(q3, kv3, bm)
            else:
                o = _attention(q3, kv3, bm, cache_k.reshape(batch, -1, d), cache_v.reshape(batch, -1, d))
            x = _matmul_res(o.reshape(batch * seq, d), w["att_w_out"][j], x, name="att_out")
        g_final = w["norm_final"] if layer == depth - 1 else None
        x = _mlp(x, w["norm_ffn"][layer], w["w_ff1"][layer], w["w_ff2"][layer], g_final)
    hd = d // ATT_HEADS
    k = k_rows.reshape(batch, keep, ATT_HEADS, hd)
    v = v_rows.reshape(batch, keep, ATT_HEADS, hd)
    return x.reshape(batch, seq, d), states, k, v


def kernel(x_prompt, x_sample, state_gla, cache_k, cache_v, norm_mix, norm_ffn, w_ff1, w_ff2, gla_w_in, gla_w_gk, gla_b_gk, gla_g_norm, gla_w_out, norm_kv, w_kv, att_w_q, att_rel_bias, att_w_out, norm_final):
    d = x_prompt.shape[-1]
    dk = d // 2
    n_main = 2 * dk + 2 * d
    rank = gla_w_in.shape[-1] - n_main
    pad = LANES - rank
    w = {
        "norm_mix": norm_mix, "norm_ffn": norm_ffn, "norm_kv": norm_kv, "norm_final": norm_final,
        "w_ff1": w_ff1.astype(BF16), "w_ff2": w_ff2.astype(BF16),
        "gla_w_main": gla_w_in[:, :, :n_main].astype(BF16),
        "gla_w_lr": jnp.pad(gla_w_in[:, :, n_main:], ((0, 0), (0, 0), (0, pad))).astype(BF16),
        "gla_w_gk": jnp.pad(gla_w_gk, ((0, 0), (0, pad), (0, 0))).astype(BF16),
        "gla_b_gk": gla_b_gk, "gla_g_norm": gla_g_norm,
        "gla_w_out": gla_w_out.astype(BF16),
        "w_kv": w_kv.astype(BF16), "att_w_q": att_w_q.astype(BF16),
        "att_rel_bias": att_rel_bias, "att_w_out": att_w_out.astype(BF16),
    }
    seq = x_prompt.shape[1]
    keep = min(LEFT_CHUNKS * CHUNK, seq)
    y_p, s_p, k_p, v_p = _trunk(x_prompt, None, None, None, w, keep)
    y_s, s_s, k_s, v_s = _trunk(x_sample, state_gla, cache_k, cache_v, w, x_sample.shape[1])
    return (y_p, y_s, s_p, k_p, v_p, s_s, k_s, v_s)
```

```python
import functools

import jax
import jax.numpy as jnp
from jax import lax
from jax.experimental import pallas as pl
from jax.experimental.pallas import tpu as pltpu

F32 = jnp.float32
BF16 = jnp.bfloat16

CHUNK = 64
GLA_HEADS = 4
GLA_SUB = 16
GLA_SUB_SHIFT = GLA_SUB.bit_length() - 1
GLA_GATE_NORM = 16.0
GLA_SINGLE_REF_LOG2_RANGE = 64.0
ATT_HEADS = 16
LEFT_CHUNKS = 8
PAST_LEN = 2048
MAX_REL = 128
EPS = 1e-6
NEG_INF = -1e30
LOG2E = 1.4426950408889634

LANES = 128
VMEM_LIMIT_BYTES = 60 * 2**20
ROW_TILE = 1024
COL_TILE = 2048
COL_TILE_MULTI = 1024
RES_ROW_TILE = 512
FF_TILE = 1024
NORM_ROWS = 256
ATT_QBLOCK = 4 * CHUNK
ATT_HEAD_GROUP = 2
GLA_ROWS = 8 * CHUNK
SUBLANES = 8


def _tile(n, pref):
    if n <= pref:
        return n
    t = pref
    while n % t:
        t //= 2
    return t


def _params(sem):
    return pltpu.CompilerParams(dimension_semantics=sem, vmem_limit_bytes=VMEM_LIMIT_BYTES)


def _rmsnorm_rows(x_ref, g_ref, h_ref):
    rows = x_ref.shape[0]
    rc = _tile(rows, NORM_ROWS)

    def body(r, carry):
        sl = pl.ds(pl.multiple_of(r * rc, rc), rc)
        x = x_ref[sl, :]
        ms = jnp.mean(x * x, axis=-1, keepdims=True)
        h_ref[sl, :] = (x * lax.rsqrt(ms + EPS) * g_ref[...]).astype(h_ref.dtype)
        return carry

    lax.fori_loop(0, rows // rc, body, 0)


def _norm_matmul_kernel(*refs, n_w, has_side, out_scale):
    x_ref, g_ref = refs[:2]
    w_refs = refs[2:2 + n_w]
    pos = 2 + n_w
    ws_ref = None
    if has_side:
        ws_ref = refs[pos]
        pos += 1
    out_refs = refs[pos:pos + n_w]
    pos += n_w
    side_ref = None
    if has_side:
        side_ref = refs[pos]
        pos += 1
    h_ref = refs[pos]

    @pl.when(pl.program_id(1) == 0)
    def _():
        _rmsnorm_rows(x_ref, g_ref, h_ref)
        if has_side:
            side_ref[...] = jnp.dot(h_ref[...], ws_ref[...], preferred_element_type=F32)

    for w_ref, o_ref in zip(w_refs, out_refs):
        y = jnp.dot(h_ref[...], w_ref[...], preferred_element_type=F32)
        if out_scale is not None:
            y = y * out_scale
        o_ref[...] = y.astype(o_ref.dtype)


def _norm_matmul(x, g, ws, out_dtype, w_side=None, out_scale=None, row_tiles=None, col_windows=None,
                 name="norm_matmul"):
    m, d = x.shape
    n, col0 = (ws[0].shape[1], [0] * len(ws)) if col_windows is None else col_windows
    if row_tiles is None:
        tm = _tile(m, ROW_TILE)
        n_tiles, block_of_tile = m // tm, lambda i: i
    else:
        tm, n_tiles, block_of_tile = row_tiles
    tn = _tile(n, COL_TILE if len(ws) == 1 else COL_TILE_MULTI)
    assert all(c0 % tn == 0 for c0 in col0)
    has_side = w_side is not None
    in_specs = [
        pl.BlockSpec((tm, d), lambda i, j: (block_of_tile(i), 0)),
        pl.BlockSpec((1, d), lambda i, j: (0, 0)),
    ] + [pl.BlockSpec((d, tn), functools.partial(lambda i, j, first: (0, first // tn + j), first=c0)) for c0 in col0]
    args = [x, g.reshape(1, d)] + list(ws)
    out_specs = [pl.BlockSpec((tm, tn), lambda i, j: (i, j)) for _ in ws]
    out_shape = [jax.ShapeDtypeStruct((n_tiles * tm, n), out_dtype) for _ in ws]
    if has_side:
        ns = w_side.shape[1]
        in_specs.append(pl.BlockSpec((d, ns), lambda i, j: (0, 0)))
        args.append(w_side)
        out_specs.append(pl.BlockSpec((tm, ns), lambda i, j: (i, 0)))
        out_shape.append(jax.ShapeDtypeStruct((n_tiles * tm, ns), F32))
    return pl.pallas_call(
        functools.partial(_norm_matmul_kernel, n_w=len(ws), has_side=has_side, out_scale=out_scale),
        grid=(n_tiles, n // tn),
        in_specs=in_specs,
        out_specs=out_specs,
        out_shape=out_shape,
        scratch_shapes=[pltpu.VMEM((tm, d), BF16)],
        compiler_params=_params(("parallel", "arbitrary")),
        name=name,
    )(*args)


def _matmul_res_kernel(a_ref, w_ref, x_ref, o_ref):
    o_ref[...] = x_ref[...] + jnp.dot(a_ref[...], w_ref[...], preferred_element_type=F32)


def _matmul_res(a, w, x, name="matmul_res"):
    m, k = a.shape
    n = w.shape[1]
    tm = _tile(m, RES_ROW_TILE)
    return pl.pallas_call(
        _matmul_res_kernel,
        grid=(m // tm,),
        in_specs=[
            pl.BlockSpec((tm, k), lambda i: (i, 0)),
            pl.BlockSpec((k, n), lambda i: (0, 0)),
            pl.BlockSpec((tm, n), lambda i: (i, 0)),
        ],
        out_specs=pl.BlockSpec((tm, n), lambda i: (i, 0)),
        out_shape=jax.ShapeDtypeStruct((m, n), F32),
        compiler_params=_params(("parallel",)),
        name=name,
    )(a, w, x)


def _mlp_kernel(*refs, final_norm):
    if final_norm:
        x_ref, g_ref, w1_ref, w2_ref, gf_ref, o_ref, h_ref = refs
    else:
        x_ref, g_ref, w1_ref, w2_ref, o_ref, h_ref = refs
        gf_ref = None
    f = pl.program_id(1)

    @pl.when(f == 0)
    def _():
        _rmsnorm_rows(x_ref, g_ref, h_ref)
        o_ref[...] = x_ref[...]

    a = jnp.dot(h_ref[...], w1_ref[...], preferred_element_type=F32)
    a = jnp.maximum(a, 0.0)
    a = (a * a).astype(BF16)
    o_ref[...] += jnp.dot(a, w2_ref[...], preferred_element_type=F32)

    if final_norm:
        @pl.when(f == pl.num_programs(1) - 1)
        def _():
            rows = o_ref.shape[0]
            rc = _tile(rows, NORM_ROWS)

            def body(r, carry):
                sl = pl.ds(pl.multiple_of(r * rc, rc), rc)
                y = o_ref[sl, :]
                ms = jnp.mean(y * y, axis=-1, keepdims=True)
                o_ref[sl, :] = y * lax.rsqrt(ms + EPS) * gf_ref[...]
                return carry

            lax.fori_loop(0, rows // rc, body, 0)


def _mlp(x, g, w1, w2, g_final=None, name="mlp"):
    m, d = x.shape
    ff = w1.shape[1]
    tm = _tile(m, ROW_TILE)
    tf = _tile(ff, FF_TILE)
    final_norm = g_final is not None
    in_specs = [
        pl.BlockSpec((tm, d), lambda i, f: (i, 0)),
        pl.BlockSpec((1, d), lambda i, f: (0, 0)),
        pl.BlockSpec((d, tf), lambda i, f: (0, f)),
        pl.BlockSpec((tf, d), lambda i, f: (f, 0)),
    ]
    args = [x, g.reshape(1, d), w1, w2]
    if final_norm:
        in_specs.append(pl.BlockSpec((1, d), lambda i, f: (0, 0)))
        args.append(g_final.reshape(1, d))
    return pl.pallas_call(
        functools.partial(_mlp_kernel, final_norm=final_norm),
        grid=(m // tm, ff // tf),
        in_specs=in_specs,
        out_specs=pl.BlockSpec((tm, d), lambda i, f: (i, 0)),
        out_shape=jax.ShapeDtypeStruct((m, d), F32),
        scratch_shapes=[pltpu.VMEM((tm, d), BF16)],
        compiler_params=_params(("parallel", "arbitrary")),
        name=name,
    )(*args)


def _log2_gates(lr, wgk, bgk):
    z = jnp.dot(lr.astype(BF16), wgk, preferred_element_type=F32) + bgk
    return (jnp.minimum(z, 0.0) - jnp.log(1.0 + jnp.exp(-jnp.abs(z)))) * (LOG2E / GLA_GATE_NORM)


def _chunk_cumsums(g, n_chunks):
    c = CHUNK
    row = lax.broadcasted_iota(jnp.int32, (c, c), 0)
    col = lax.broadcasted_iota(jnp.int32, (c, c), 1)
    tri = (col <= row).astype(BF16)
    g_hi = g.astype(BF16)
    g_r = g - g_hi.astype(F32)
    g_mid = g_r.astype(BF16)
    g_lo = (g_r - g_mid.astype(F32)).astype(BF16)
    out = []
    for ci in range(n_chunks):
        r = slice(ci * c, (ci + 1) * c)
        out.append(jnp.dot(tri, g_hi[r], preferred_element_type=F32)
                   + jnp.dot(tri, g_mid[r], preferred_element_type=F32)
                   + jnp.dot(tri, g_lo[r], preferred_element_type=F32))
    return out


def _gla_offdiag(q, k, b):
    c = q.shape[0]
    a_rows = [jnp.zeros((GLA_SUB, c), F32)]
    for l in range(1, c // GLA_SUB):
        lo = l * GLA_SUB
        ref = b[lo - 1:lo, :]
        q_ref = q[lo:lo + GLA_SUB, :] * jnp.exp2(b[lo:lo + GLA_SUB, :] - ref)
        k_ref = k * jnp.exp2(jnp.minimum(ref - b, 0.0))
        a_rows.append(lax.dot_general(q_ref.astype(BF16), k_ref.astype(BF16),
                                      (((1,), (1,)), ((), ())), preferred_element_type=F32))
    return jnp.concatenate(a_rows, axis=0)


def _gla_diag(q, k, b):
    c, hk = q.shape
    nsub = c // GLA_SUB
    q3 = q.reshape(nsub, GLA_SUB, hk)
    k3 = k.reshape(nsub, GLA_SUB, hk)
    b3 = b.reshape(nsub, GLA_SUB, hk)
    col3 = lax.broadcasted_iota(jnp.int32, (nsub, SUBLANES, c), 2)
    blk3 = lax.broadcasted_iota(jnp.int32, (nsub, SUBLANES, c), 0) * GLA_SUB
    top = jnp.zeros((nsub, SUBLANES, c), F32)
    bot = jnp.zeros((nsub, GLA_SUB - SUBLANES, c), F32)
    for e in range(GLA_SUB):
        lo = 0 if e < SUBLANES else SUBLANES
        k_e = jnp.broadcast_to(k3[:, e:e + 1, :], (nsub, GLA_SUB - lo, hk))
        b_e = jnp.broadcast_to(b3[:, e:e + 1, :], (nsub, GLA_SUB - lo, hk))
        t = q3[:, lo:, :] * k_e * jnp.exp2(jnp.minimum(b3[:, lo:, :] - b_e, 0.0))
        ts = jnp.sum(t, axis=-1, keepdims=True)
        hit = col3 == blk3 + e
        if lo == 0:
            top = jnp.where(hit, ts[:, :SUBLANES, :], top)
            bot = jnp.where(hit, ts[:, SUBLANES:, :], bot)
        else:
            bot = jnp.where(hit, ts, bot)
    return jnp.concatenate([top, bot], axis=1).reshape(c, c)


def _gla_kernel(*refs, has_state0, n_prev, n_chunks):
    q_ref, k_ref, v_ref, gate_ref, lr_ref, wgk_ref, bgk_ref, gn_ref = refs[:8]
    s0_ref = refs[8] if has_state0 else None
    first_prev = 9 if has_state0 else 8
    prev_refs = refs[first_prev:first_prev + n_prev]
    o_ref, sfin_ref, s_scr = refs[-3:]
    t = pl.program_id(2)
    hk = q_ref.shape[-1]
    scale = hk ** -0.5
    c = CHUNK

    @pl.when(t == 0)
    def _():
        if has_state0:
            s_scr[...] = s0_ref[...]
        else:
            s_scr[...] = jnp.zeros_like(s_scr)

    g = _log2_gates(lr_ref[...], wgk_ref[...], bgk_ref[...])
    bs = _chunk_cumsums(g, n_chunks)
    row = lax.broadcasted_iota(jnp.int32, (c, c), 0)
    col = lax.broadcasted_iota(jnp.int32, (c, c), 1)
    row_blk = jnp.right_shift(row, GLA_SUB_SHIFT)
    col_blk = jnp.right_shift(col, GLA_SUB_SHIFT)
    below = col_blk < row_blk
    on_diag = (col_blk == row_blk) & (col <= row)

    def prepare(ci, single_ref):
        r = slice(ci * c, (ci + 1) * c)
        q = q_ref[r, :].astype(F32) * scale
        k = k_ref[r, :].astype(F32)
        b = bs[ci]
        b_end = b[c - 1:c, :]
        decay_col = jnp.transpose(jnp.broadcast_to(jnp.exp2(b_end), (LANES, hk)))[:, :1]
        k_dec = (k * jnp.exp2(b_end - b)).astype(BF16)
        if single_ref:
            q_up = (q * jnp.exp2(b - b_end)).astype(BF16)
            scores = lax.dot_general(q_up, k_dec, (((1,), (1,)), ((), ())), preferred_element_type=F32)
        else:
            scores = _gla_offdiag(q, k, b)
        return dict(q=q, k=k, b=b, q_dec=(q * jnp.exp2(b)).astype(BF16), k_dec=k_dec,
                    decay_col=decay_col, scores=scores)

    def finish(ci, o):
        r = slice(ci * c, (ci + 1) * c)
        ms = jnp.mean(o * o, axis=-1, keepdims=True)
        y = o * lax.rsqrt(ms + EPS) * gn_ref[...]
        gate = gate_ref[r, :].astype(F32)
        o_ref[r, :] = (y * (gate * (1.0 / (1.0 + jnp.exp(-gate))))).astype(o_ref.dtype)

    def run(single_ref):
        s = s_scr[...]
        cur = prepare(0, single_ref)
        prev_out = None
        prev_upd = None
        for ci in range(n_chunks):
            if prev_upd is not None:
                s = prev_upd[0] * s + prev_upd[1]
            o_inter = jnp.dot(cur["q_dec"], s.astype(BF16), preferred_element_type=F32)
            if prev_out is not None:
                finish(ci - 1, prev_out[0] + prev_out[1])
            if single_ref:
                a = jnp.where(col <= row, cur["scores"], 0.0).astype(BF16)
            else:
                a_diag = _gla_diag(cur["q"], cur["k"], cur["b"])
                a = jnp.where(below, cur["scores"], jnp.where(on_diag, a_diag, 0.0)).astype(BF16)
            v16 = v_ref[ci * c:(ci + 1) * c, :]
            prev_out = (jnp.dot(a, v16, preferred_element_type=F32), o_inter)
            prev_upd = (cur["decay_col"],
                        lax.dot_general(cur["k_dec"], v16, (((0,), (0,)), ((), ())), preferred_element_type=F32))
            if ci + 1 < n_chunks:
                cur = prepare(ci + 1, single_ref)
        s_scr[...] = prev_upd[0] * s + prev_upd[1]
        finish(n_chunks - 1, prev_out[0] + prev_out[1])

    mild = jnp.min(g) >= -GLA_SINGLE_REF_LOG2_RANGE / c

    @pl.when(mild)
    def _():
        run(True)

    @pl.when(jnp.logical_not(mild))
    def _():
        run(False)

    @pl.when(t == pl.num_programs(2) - 1)
    def _():
        if len(sfin_ref.shape) == 2:
            sfin_ref[...] = s_scr[...]
        else:
            for i, p_ref in enumerate(prev_refs):
                sfin_ref[i] = p_ref[...]
            sfin_ref[n_prev] = s_scr[...]


def _gla(proj, lr, wgk, bgk, g_norm, state0, layer, n_layers, prev_states, batch, seq, name="gla"):
    h = GLA_HEADS
    stacked = layer == n_layers - 1
    assert len(prev_states) == (layer if stacked else 0)
    dv = proj.shape[1] // 3
    dk = dv // 2
    hk, hv = dk // h, dv // h
    tb = _tile(seq, GLA_ROWS)
    proj3 = proj.reshape(batch, seq, proj.shape[1])
    lr3 = lr.reshape(batch, seq, lr.shape[1])
    in_specs = [
        pl.BlockSpec((None, tb, hk), lambda b, hh, t: (b, t, hh)),
        pl.BlockSpec((None, tb, hk), lambda b, hh, t: (b, t, h + hh)),
        pl.BlockSpec((None, tb, hv), lambda b, hh, t: (b, t, 2 * dk // hv + hh)),
        pl.BlockSpec((None, tb, hv), lambda b, hh, t: (b, t, (2 * dk + dv) // hv + hh)),
        pl.BlockSpec((None, tb, lr.shape[1]), lambda b, hh, t: (b, t, 0)),
        pl.BlockSpec((lr.shape[1], hk), lambda b, hh, t: (0, hh)),
        pl.BlockSpec((1, hk), lambda b, hh, t: (0, hh)),
        pl.BlockSpec((1, hv), lambda b, hh, t: (0, 0)),
    ]
    args = [proj3, proj3, proj3, proj3, lr3, wgk, bgk.reshape(1, dk), g_norm.reshape(1, hv)]
    has_state0 = state0 is not None
    if has_state0:
        in_specs.append(pl.BlockSpec((None, None, None, hk, hv), lambda b, hh, t: (layer, b, hh, 0, 0)))
        args.append(state0)
    one_state = pl.BlockSpec((None, None, hk, hv), lambda b, hh, t: (b, hh, 0, 0))
    in_specs += [one_state] * len(prev_states)
    args += prev_states
    if stacked:
        state_spec = pl.BlockSpec((n_layers, None, None, hk, hv), lambda b, hh, t: (0, b, hh, 0, 0))
        state_shape = jax.ShapeDtypeStruct((n_layers, batch, h, hk, hv), F32)
    else:
        state_spec = one_state
        state_shape = jax.ShapeDtypeStruct((batch, h, hk, hv), F32)
    og, states = pl.pallas_call(
        functools.partial(_gla_kernel, has_state0=has_state0, n_prev=len(prev_states), n_chunks=tb // CHUNK),
        grid=(batch, h, seq // tb),
        in_specs=in_specs,
        out_specs=[pl.BlockSpec((None, tb, hv), lambda b, hh, t: (b, t, hh)), state_spec],
        out_shape=[jax.ShapeDtypeStruct((batch, seq, dv), BF16), state_shape],
        scratch_shapes=[pltpu.VMEM((hk, hv), F32)],
        compiler_params=_params(("parallel", "parallel", "arbitrary")),
        name=name,
    )(*args)
    return og.reshape(batch * seq, dv), states


def _attn_kernel(*refs, qb, past, hd, hg):
    if past:
        q_ref, k_ref, v_ref, kc_ref, vc_ref, bm_ref, o_ref = refs
    else:
        q_ref, k_ref, v_ref, bm_ref, o_ref = refs
        kc_ref = vc_ref = None
    tq = q_ref.shape[0]
    left = LEFT_CHUNKS * CHUNK
    wfull = bm_ref.shape[-1]
    blocks = [(hh, i) for hh in range(hg) for i in range(tq // qb)]

    def window(i):
        return max(0, i * qb + past - left), i * qb + past + qb

    def rows(new_ref, cache_ref, k0, k1, hh):
        cs = slice(hh * hd, (hh + 1) * hd)
        parts = []
        if k0 < past:
            parts.append(cache_ref[k0:min(k1, past), cs])
        if k1 > past:
            parts.append(new_ref[max(k0, past) - past:k1 - past, cs])
        return parts[0] if len(parts) == 1 else jnp.concatenate(parts, axis=0)

    def scores(hh, i):
        k0, k1 = window(i)
        s = lax.dot_general(q_ref[i * qb:(i + 1) * qb, hh * hd:(hh + 1) * hd], rows(k_ref, kc_ref, k0, k1, hh),
                            (((1,), (1,)), ((), ())), preferred_element_type=F32)
        return s + bm_ref[hh, :, wfull - (k1 - k0):]

    def store(hh, i, o, denom):
        o_ref[i * qb:(i + 1) * qb, hh * hd:(hh + 1) * hd] = (o * (1.0 / denom)).astype(o_ref.dtype)

    s_next = scores(*blocks[0])
    pending = None
    for n, (hh, i) in enumerate(blocks):
        s = s_next
        if n + 1 < len(blocks):
            s_next = scores(*blocks[n + 1])
        e = jnp.exp2(s - jnp.max(s, axis=-1, keepdims=True))
        k0, k1 = window(i)
        o = jnp.dot(e.astype(BF16), rows(v_ref, vc_ref, k0, k1, hh), preferred_element_type=F32)
        if pending is not None:
            store(*pending)
        pending = (hh, i, o, jnp.sum(e, axis=-1, keepdims=True))
    store(*pending)


def _attention(q, kv, bm, cache_k=None, cache_v=None, name="attn"):
    batch, tq, d = q.shape
    hd = d // ATT_HEADS
    hg = ATT_HEAD_GROUP if tq > bm.shape[1] else ATT_HEADS
    gw = hg * hd
    qb = bm.shape[1]
    past = 0 if cache_k is None else cache_k.shape[1]
    assert PAST_LEN % CHUNK == 0 and past in (0, min(LEFT_CHUNKS * CHUNK, PAST_LEN))
    in_specs = [
        pl.BlockSpec((None, tq, gw), lambda b, g: (b, 0, g)),
        pl.BlockSpec((None, tq, gw), lambda b, g: (b, 0, g)),
        pl.BlockSpec((None, tq, gw), lambda b, g: (b, 0, d // gw + g)),
    ]
    args = [q, kv, kv]
    if past:
        in_specs += [pl.BlockSpec((None, past, gw), lambda b, g: (b, 0, g))] * 2
        args += [cache_k, cache_v]
    in_specs.append(pl.BlockSpec((hg, qb, bm.shape[2]), lambda b, g: (g, 0, 0)))
    args.append(bm)
    return pl.pallas_call(
        functools.partial(_attn_kernel, qb=qb, past=past, hd=hd, hg=hg),
        grid=(batch, ATT_HEADS // hg),
        in_specs=in_specs,
        out_specs=pl.BlockSpec((None, tq, gw), lambda b, g: (b, 0, g)),
        out_shape=jax.ShapeDtypeStruct((batch, tq, d), BF16),
        compiler_params=_params(("parallel", "parallel")),
        name=name,
    )(*args)


def _bias_mask(table, qb):
    left = LEFT_CHUNKS * CHUNK
    h = table.shape[0]
    t = table.astype(F32) * LOG2E
    sat = left + 2 * qb
    ext = jnp.concatenate([jnp.broadcast_to(t[:, :1], (h, sat)), t, jnp.broadcast_to(t[:, -1:], (h, sat))], axis=1)
    top = left + qb - 1
    n_w = left + 2 * qb - 1
    start = ext.shape[1] - 1 - (top + MAX_REL + sat)
    desc = ext[:, ::-1][:, start:start + n_w]
    w = jnp.concatenate([desc[:, qb - 1:], desc[:, :qb - 1]], axis=1)
    bias = jnp.tile(w, (1, qb))[:, :qb * (n_w - 1)].reshape(h, qb, n_w - 1)[:, :, :left + qb]
    r = jnp.arange(qb)[:, None]
    c = jnp.arange(left + qb)[None, :]
    qc = r // CHUNK
    kc = c // CHUNK - LEFT_CHUNKS
    allowed = (kc <= qc) & (kc >= qc - LEFT_CHUNKS)
    return jnp.where(allowed[None], bias, NEG_INF)


def _trunk(x, state0, cache_k, cache_v, w, keep):
    batch, seq, d = x.shape
    depth = w["norm_mix"].shape[0]
    n_a = depth // 2
    x = x.reshape(batch * seq, d)
    states = []
    k_rows = None
    v_rows = None
    kv16 = None
    qb = _tile(seq, ATT_QBLOCK)
    if cache_k is not None:
        cache16 = (cache_k.astype(BF16).reshape(batch, -1, d), cache_v.astype(BF16).reshape(batch, -1, d))
    for layer in range(depth):
        if layer < n_a:
            (proj, lr) = _norm_matmul(x, w["norm_mix"][layer], [w["gla_w_in"][layer]], BF16,
                                      w_side=w["gla_w_lr"][layer], col_windows=(3 * d, [0]), name="gla_in")
            og, s_end = _gla(proj, lr, w["gla_w_gk"][layer], w["gla_b_gk"][layer], w["gla_g_norm"][layer],
                             state0, layer, n_a, states if layer == n_a - 1 else [], batch, seq)
            states = s_end if layer == n_a - 1 else states + [s_end]
            x = _matmul_res(og, w["gla_w_out"][layer], x, name="gla_out")
        else:
            j = layer - n_a
            if layer == n_a:
                (kv16,) = _norm_matmul(x, w["norm_kv"], [w["w_kv"]], BF16, name="kv_proj")
                if keep == seq:
                    row_tiles = None
                else:
                    assert seq % keep == 0
                    per_seq = seq // keep
                    row_tiles = (keep, batch, lambda i: i * per_seq + per_seq - 1)
                k_rows, v_rows = _norm_matmul(x, w["norm_kv"], [w["w_kv"], w["w_kv"]], F32,
                                              row_tiles=row_tiles, col_windows=(d, [0, d]), name="kv_rows")
            (q16,) = _norm_matmul(x, w["norm_mix"][layer], [w["att_w_q"][j]], BF16,
                                  out_scale=(d // ATT_HEADS) ** -0.5 * LOG2E, name="q_proj")
            q3 = q16.reshape(batch, seq, d)
            bm = _bias_mask(w["att_rel_bias"][j], qb)
            kv3 = kv16.reshape(batch, seq, 2 * d)
            if cache_k is None:
                o = _attention(q3, kv3, bm)
            else:
                o = _attention(q3, kv3, bm, *cache16)
            x = _matmul_res(o.reshape(batch * seq, d), w["att_w_out"][j], x, name="att_out")
        g_final = w["norm_final"] if layer == depth - 1 else None
        x = _mlp(x, w["norm_ffn"][layer], w["w_ff1"][layer], w["w_ff2"][layer], g_final)
    hd = d // ATT_HEADS
    k = k_rows.reshape(batch, keep, ATT_HEADS, hd)
    v = v_rows.reshape(batch, keep, ATT_HEADS, hd)
    return x.reshape(batch, seq, d), states, k, v


def kernel(x_prompt, x_sample, state_gla, cache_k, cache_v, norm_mix, norm_ffn, w_ff1, w_ff2, gla_w_in, gla_w_gk, gla_b_gk, gla_g_norm, gla_w_out, norm_kv, w_kv, att_w_q, att_rel_bias, att_w_out, norm_final):
    d = x_prompt.shape[-1]
    dk = d // 2
    n_main = 2 * dk + 2 * d
    rank = gla_w_in.shape[-1] - n_main
    pad = LANES - rank
    w = {
        "norm_mix": norm_mix, "norm_ffn": norm_ffn, "norm_kv": norm_kv, "norm_final": norm_final,
        "w_ff1": w_ff1.astype(BF16), "w_ff2": w_ff2.astype(BF16),
        "gla_w_in": gla_w_in.astype(BF16),
        "gla_w_lr": jnp.pad(gla_w_in[:, :, n_main:], ((0, 0), (0, 0), (0, pad))).astype(BF16),
        "gla_w_gk": jnp.pad(gla_w_gk, ((0, 0), (0, pad), (0, 0))).astype(BF16),
        "gla_b_gk": gla_b_gk, "gla_g_norm": gla_g_norm,
        "gla_w_out": gla_w_out.astype(BF16),
        "w_kv": w_kv.astype(BF16), "att_w_q": att_w_q.astype(BF16),
        "att_rel_bias": att_rel_bias, "att_w_out": att_w_out.astype(BF16),
    }
    seq = x_prompt.shape[1]
    keep = min(LEFT_CHUNKS * CHUNK, seq)
    y_p, s_p, k_p, v_p = _trunk(x_prompt, None, None, None, w, keep)
    y_s, s_s, k_s, v_s = _trunk(x_sample, state_gla, cache_k, cache_v, w, x_sample.shape[1])
    return (y_p, y_s, s_p, k_p, v_p, s_s, k_s, v_s)
```

```python
import functools

import jax
import jax.numpy as jnp
from jax import lax
from jax.experimental import pallas as pl
from jax.experimental.pallas import tpu as pltpu

F32 = jnp.float32
BF16 = jnp.bfloat16

CHUNK = 64
GLA_HEADS = 4
GLA_SUB = 16
GLA_SUB_SHIFT = GLA_SUB.bit_length() - 1
GLA_GATE_NORM = 16.0
GLA_SINGLE_REF_LOG2_RANGE = 64.0
ATT_HEADS = 16
LEFT_CHUNKS = 8
PAST_LEN = 2048
MAX_REL = 128
EPS = 1e-6
NEG_INF = -1e30
LOG2E = 1.4426950408889634

LANES = 128
VMEM_LIMIT_BYTES = 60 * 2**20
ROW_TILE = 1024
GLA_IN_STEPS = 4
COL_TILE = 2048
COL_TILE_MULTI = 1024
RES_ROW_TILE = 512
FF_TILE = 1024
NORM_ROWS = 256
ATT_QBLOCK = 4 * CHUNK
ATT_HEAD_GROUP = 2
GLA_ROWS = 8 * CHUNK
SUBLANES = 8


def _tile(n, pref):
    if n <= pref:
        return n
    t = pref
    while n % t:
        t //= 2
    return t


def _params(sem):
    return pltpu.CompilerParams(dimension_semantics=sem, vmem_limit_bytes=VMEM_LIMIT_BYTES)


def _rmsnorm_rows(x_ref, g_ref, h_ref):
    rows = x_ref.shape[0]
    rc = _tile(rows, NORM_ROWS)

    def body(r, carry):
        sl = pl.ds(pl.multiple_of(r * rc, rc), rc)
        x = x_ref[sl, :]
        ms = jnp.mean(x * x, axis=-1, keepdims=True)
        h_ref[sl, :] = (x * lax.rsqrt(ms + EPS) * g_ref[...]).astype(h_ref.dtype)
        return carry

    lax.fori_loop(0, rows // rc, body, 0)


def _norm_matmul_kernel(*refs, n_w, out_scale):
    x_ref, g_ref = refs[:2]
    w_refs = refs[2:2 + n_w]
    out_refs = refs[2 + n_w:2 + 2 * n_w]
    h_ref = refs[2 + 2 * n_w]

    @pl.when(pl.program_id(1) == 0)
    def _():
        _rmsnorm_rows(x_ref, g_ref, h_ref)

    for w_ref, o_ref in zip(w_refs, out_refs):
        y = jnp.dot(h_ref[...], w_ref[...], preferred_element_type=F32)
        if out_scale is not None:
            y = y * out_scale
        o_ref[...] = y.astype(o_ref.dtype)


def _norm_matmul(x, g, ws, out_dtype, out_scale=None, row_tiles=None, col_windows=None, name="norm_matmul"):
    m, d = x.shape
    n, col0 = (ws[0].shape[1], [0] * len(ws)) if col_windows is None else col_windows
    if row_tiles is None:
        tm = _tile(m, ROW_TILE)
        n_tiles, block_of_tile = m // tm, lambda i: i
    else:
        tm, n_tiles, block_of_tile = row_tiles
    tn = _tile(n, COL_TILE if len(ws) == 1 else COL_TILE_MULTI)
    assert all(c0 % tn == 0 for c0 in col0)
    in_specs = [
        pl.BlockSpec((tm, d), lambda i, j: (block_of_tile(i), 0)),
        pl.BlockSpec((1, d), lambda i, j: (0, 0)),
    ] + [pl.BlockSpec((d, tn), functools.partial(lambda i, j, first: (0, first // tn + j), first=c0)) for c0 in col0]
    return pl.pallas_call(
        functools.partial(_norm_matmul_kernel, n_w=len(ws), out_scale=out_scale),
        grid=(n_tiles, n // tn),
        in_specs=in_specs,
        out_specs=[pl.BlockSpec((tm, tn), lambda i, j: (i, j)) for _ in ws],
        out_shape=[jax.ShapeDtypeStruct((n_tiles * tm, n), out_dtype) for _ in ws],
        scratch_shapes=[pltpu.VMEM((tm, d), BF16)],
        compiler_params=_params(("parallel", "arbitrary")),
        name=name,
    )(x, g.reshape(1, d), *ws)


def _gla_in_kernel(x_ref, g_ref, w_ref, wlr_ref, wgk_ref, bgk_ref, proj_ref, gates_ref, gmin_ref, h_ref, lr_ref):
    @pl.when(pl.program_id(1) == 0)
    def _():
        _rmsnorm_rows(x_ref, g_ref, h_ref)
        lr_ref[...] = jnp.dot(h_ref[...], wlr_ref[...], preferred_element_type=F32).astype(lr_ref.dtype)

    proj_ref[...] = jnp.dot(h_ref[...], w_ref[...], preferred_element_type=F32).astype(proj_ref.dtype)
    z = jnp.dot(lr_ref[...], wgk_ref[...], preferred_element_type=F32) + bgk_ref[...]
    gates = (jnp.minimum(z, 0.0) - jnp.log(1.0 + jnp.exp(-jnp.abs(z)))) * (LOG2E / GLA_GATE_NORM)
    gates_ref[...] = gates
    gmin_ref[...] = jnp.min(gates, axis=0, keepdims=True)


def _gla_in(x, g, w_in, w_lr, wgk, bgk, n_main):
    m, d = x.shape
    dk = wgk.shape[1]
    tm = _tile(m, ROW_TILE)
    steps = GLA_IN_STEPS
    tn, tg = n_main // steps, dk // steps
    assert tn * steps == n_main and tg * steps == dk and tn % LANES == 0 and tg % LANES == 0
    return pl.pallas_call(
        _gla_in_kernel,
        grid=(m // tm, steps),
        in_specs=[
            pl.BlockSpec((tm, d), lambda i, j: (i, 0)),
            pl.BlockSpec((1, d), lambda i, j: (0, 0)),
            pl.BlockSpec((d, tn), lambda i, j: (0, j)),
            pl.BlockSpec((d, w_lr.shape[1]), lambda i, j: (0, 0)),
            pl.BlockSpec((wgk.shape[0], tg), lambda i, j: (0, j)),
            pl.BlockSpec((1, tg), lambda i, j: (0, j)),
        ],
        out_specs=[
            pl.BlockSpec((tm, tn), lambda i, j: (i, j)),
            pl.BlockSpec((tm, tg), lambda i, j: (i, j)),
            pl.BlockSpec((None, 1, tg), lambda i, j: (i, 0, j)),
        ],
        out_shape=[
            jax.ShapeDtypeStruct((m, n_main), BF16),
            jax.ShapeDtypeStruct((m, dk), F32),
            jax.ShapeDtypeStruct((m // tm, 1, dk), F32),
        ],
        scratch_shapes=[pltpu.VMEM((tm, d), BF16), pltpu.VMEM((tm, w_lr.shape[1]), BF16)],
        compiler_params=_params(("parallel", "arbitrary")),
        name="gla_in",
    )(x, g.reshape(1, d), w_in, w_lr, wgk, bgk.reshape(1, dk))


def _matmul_res_kernel(a_ref, w_ref, x_ref, o_ref):
    o_ref[...] = x_ref[...] + jnp.dot(a_ref[...], w_ref[...], preferred_element_type=F32)


def _matmul_res(a, w, x, name="matmul_res"):
    m, k = a.shape
    n = w.shape[1]
    tm = _tile(m, RES_ROW_TILE)
    return pl.pallas_call(
        _matmul_res_kernel,
        grid=(m // tm,),
        in_specs=[
            pl.BlockSpec((tm, k), lambda i: (i, 0)),
            pl.BlockSpec((k, n), lambda i: (0, 0)),
            pl.BlockSpec((tm, n), lambda i: (i, 0)),
        ],
        out_specs=pl.BlockSpec((tm, n), lambda i: (i, 0)),
        out_shape=jax.ShapeDtypeStruct((m, n), F32),
        compiler_params=_params(("parallel",)),
        name=name,
    )(a, w, x)


def _mlp_kernel(*refs, final_norm):
    if final_norm:
        x_ref, g_ref, w1_ref, w2_ref, gf_ref, o_ref, h_ref = refs
    else:
        x_ref, g_ref, w1_ref, w2_ref, o_ref, h_ref = refs
        gf_ref = None
    f = pl.program_id(1)

    @pl.when(f == 0)
    def _():
        _rmsnorm_rows(x_ref, g_ref, h_ref)
        o_ref[...] = x_ref[...]

    a = jnp.dot(h_ref[...], w1_ref[...], preferred_element_type=F32)
    a = jnp.maximum(a, 0.0)
    a = (a * a).astype(BF16)
    o_ref[...] += jnp.dot(a, w2_ref[...], preferred_element_type=F32)

    if final_norm:
        @pl.when(f == pl.num_programs(1) - 1)
        def _():
            rows = o_ref.shape[0]
            rc = _tile(rows, NORM_ROWS)

            def body(r, carry):
                sl = pl.ds(pl.multiple_of(r * rc, rc), rc)
                y = o_ref[sl, :]
                ms = jnp.mean(y * y, axis=-1, keepdims=True)
                o_ref[sl, :] = y * lax.rsqrt(ms + EPS) * gf_ref[...]
                return carry

            lax.fori_loop(0, rows // rc, body, 0)


def _mlp(x, g, w1, w2, g_final=None, name="mlp"):
    m, d = x.shape
    ff = w1.shape[1]
    tm = _tile(m, ROW_TILE)
    tf = _tile(ff, FF_TILE)
    final_norm = g_final is not None
    in_specs = [
        pl.BlockSpec((tm, d), lambda i, f: (i, 0)),
        pl.BlockSpec((1, d), lambda i, f: (0, 0)),
        pl.BlockSpec((d, tf), lambda i, f: (0, f)),
        pl.BlockSpec((tf, d), lambda i, f: (f, 0)),
    ]
    args = [x, g.reshape(1, d), w1, w2]
    if final_norm:
        in_specs.append(pl.BlockSpec((1, d), lambda i, f: (0, 0)))
        args.append(g_final.reshape(1, d))
    return pl.pallas_call(
        functools.partial(_mlp_kernel, final_norm=final_norm),
        grid=(m // tm, ff // tf),
        in_specs=in_specs,
        out_specs=pl.BlockSpec((tm, d), lambda i, f: (i, 0)),
        out_shape=jax.ShapeDtypeStruct((m, d), F32),
        scratch_shapes=[pltpu.VMEM((tm, d), BF16)],
        compiler_params=_params(("parallel", "arbitrary")),
        name=name,
    )(*args)


def _chunk_cumsums(g, n_chunks):
    c = CHUNK
    row = lax.broadcasted_iota(jnp.int32, (c, c), 0)
    col = lax.broadcasted_iota(jnp.int32, (c, c), 1)
    tri = (col <= row).astype(BF16)
    g_hi = g.astype(BF16)
    g_r = g - g_hi.astype(F32)
    g_mid = g_r.astype(BF16)
    g_lo = (g_r - g_mid.astype(F32)).astype(BF16)
    out = []
    for ci in range(n_chunks):
        r = slice(ci * c, (ci + 1) * c)
        out.append(jnp.dot(tri, g_hi[r], preferred_element_type=F32)
                   + jnp.dot(tri, g_mid[r], preferred_element_type=F32)
                   + jnp.dot(tri, g_lo[r], preferred_element_type=F32))
    return out


def _gla_offdiag(q, k, b):
    c = q.shape[0]
    a_rows = [jnp.zeros((GLA_SUB, c), F32)]
    for l in range(1, c // GLA_SUB):
        lo = l * GLA_SUB
        ref = b[lo - 1:lo, :]
        q_ref = q[lo:lo + GLA_SUB, :] * jnp.exp2(b[lo:lo + GLA_SUB, :] - ref)
        k_ref = k * jnp.exp2(jnp.minimum(ref - b, 0.0))
        a_rows.append(lax.dot_general(q_ref.astype(BF16), k_ref.astype(BF16),
                                      (((1,), (1,)), ((), ())), preferred_element_type=F32))
    return jnp.concatenate(a_rows, axis=0)


def _gla_diag(q, k, b):
    c, hk = q.shape
    nsub = c // GLA_SUB
    q3 = q.reshape(nsub, GLA_SUB, hk)
    k3 = k.reshape(nsub, GLA_SUB, hk)
    b3 = b.reshape(nsub, GLA_SUB, hk)
    col3 = lax.broadcasted_iota(jnp.int32, (nsub, SUBLANES, c), 2)
    blk3 = lax.broadcasted_iota(jnp.int32, (nsub, SUBLANES, c), 0) * GLA_SUB
    top = jnp.zeros((nsub, SUBLANES, c), F32)
    bot = jnp.zeros((nsub, GLA_SUB - SUBLANES, c), F32)
    for e in range(GLA_SUB):
        lo = 0 if e < SUBLANES else SUBLANES
        k_e = jnp.broadcast_to(k3[:, e:e + 1, :], (nsub, GLA_SUB - lo, hk))
        b_e = jnp.broadcast_to(b3[:, e:e + 1, :], (nsub, GLA_SUB - lo, hk))
        t = q3[:, lo:, :] * k_e * jnp.exp2(jnp.minimum(b3[:, lo:, :] - b_e, 0.0))
        ts = jnp.sum(t, axis=-1, keepdims=True)
        hit = col3 == blk3 + e
        if lo == 0:
            top = jnp.where(hit, ts[:, :SUBLANES, :], top)
            bot = jnp.where(hit, ts[:, SUBLANES:, :], bot)
        else:
            bot = jnp.where(hit, ts, bot)
    return jnp.concatenate([top, bot], axis=1).reshape(c, c)


def _gla_kernel(*refs, has_state0, n_prev, n_chunks, rows_per_flag):
    mild_ref, q_ref, k_ref, v_ref, gate_ref, g_ref, gn_ref = refs[:7]
    s0_ref = refs[7] if has_state0 else None
    first_prev = 8 if has_state0 else 7
    prev_refs = refs[first_prev:first_prev + n_prev]
    o_ref, sfin_ref, s_scr = refs[-3:]
    t = pl.program_id(2)
    hk = q_ref.shape[-1]
    scale = hk ** -0.5
    c = CHUNK

    @pl.when(t == 0)
    def _():
        if has_state0:
            s_scr[...] = s0_ref[...]
        else:
            s_scr[...] = jnp.zeros_like(s_scr)

    bs = _chunk_cumsums(g_ref[...], n_chunks)
    row = lax.broadcasted_iota(jnp.int32, (c, c), 0)
    col = lax.broadcasted_iota(jnp.int32, (c, c), 1)
    row_blk = jnp.right_shift(row, GLA_SUB_SHIFT)
    col_blk = jnp.right_shift(col, GLA_SUB_SHIFT)
    below = col_blk < row_blk
    on_diag = (col_blk == row_blk) & (col <= row)

    def prepare(ci, single_ref):
        r = slice(ci * c, (ci + 1) * c)
        q = q_ref[r, :].astype(F32) * scale
        k = k_ref[r, :].astype(F32)
        b = bs[ci]
        b_end = b[c - 1:c, :]
        decay_col = jnp.transpose(jnp.broadcast_to(jnp.exp2(b_end), (LANES, hk)))[:, :1]
        k_dec = (k * jnp.exp2(b_end - b)).astype(BF16)
        if single_ref:
            q_up = (q * jnp.exp2(b - b_end)).astype(BF16)
            scores = lax.dot_general(q_up, k_dec, (((1,), (1,)), ((), ())), preferred_element_type=F32)
        else:
            scores = _gla_offdiag(q, k, b)
        return dict(q=q, k=k, b=b, q_dec=(q * jnp.exp2(b)).astype(BF16), k_dec=k_dec,
                    decay_col=decay_col, scores=scores)

    def finish(ci, o):
        r = slice(ci * c, (ci + 1) * c)
        ms = jnp.mean(o * o, axis=-1, keepdims=True)
        y = o * lax.rsqrt(ms + EPS) * gn_ref[...]
        gate = gate_ref[r, :].astype(F32)
        o_ref[r, :] = (y * (gate * (1.0 / (1.0 + jnp.exp(-gate))))).astype(o_ref.dtype)

    def run(single_ref):
        s = s_scr[...]
        cur = prepare(0, single_ref)
        prev_out = None
        prev_upd = None
        for ci in range(n_chunks):
            if prev_upd is not None:
                s = prev_upd[0] * s + prev_upd[1]
            o_inter = jnp.dot(cur["q_dec"], s.astype(BF16), preferred_element_type=F32)
            if prev_out is not None:
                finish(ci - 1, prev_out[0] + prev_out[1])
            if single_ref:
                a = jnp.where(col <= row, cur["scores"], 0.0).astype(BF16)
            else:
                a_diag = _gla_diag(cur["q"], cur["k"], cur["b"])
                a = jnp.where(below, cur["scores"], jnp.where(on_diag, a_diag, 0.0)).astype(BF16)
            v16 = v_ref[ci * c:(ci + 1) * c, :]
            prev_out = (jnp.dot(a, v16, preferred_element_type=F32), o_inter)
            prev_upd = (cur["decay_col"],
                        lax.dot_general(cur["k_dec"], v16, (((0,), (0,)), ((), ())), preferred_element_type=F32))
            if ci + 1 < n_chunks:
                cur = prepare(ci + 1, single_ref)
        s_scr[...] = prev_upd[0] * s + prev_upd[1]
        finish(n_chunks - 1, prev_out[0] + prev_out[1])

    tile = (pl.program_id(0) * (pl.num_programs(2) * n_chunks * c) + t * (n_chunks * c)) // rows_per_flag
    mild = mild_ref[tile, pl.program_id(1)] != 0

    @pl.when(mild)
    def _():
        run(True)

    @pl.when(jnp.logical_not(mild))
    def _():
        run(False)

    @pl.when(t == pl.num_programs(2) - 1)
    def _():
        if len(sfin_ref.shape) == 2:
            sfin_ref[...] = s_scr[...]
        else:
            for i, p_ref in enumerate(prev_refs):
                sfin_ref[i] = p_ref[...]
            sfin_ref[n_prev] = s_scr[...]


def _gla(proj, gates, mild, rows_per_flag, g_norm, state0, layer, n_layers, prev_states, batch, seq, name="gla"):
    h = GLA_HEADS
    stacked = layer == n_layers - 1
    assert len(prev_states) == (layer if stacked else 0)
    dv = proj.shape[1] // 3
    dk = dv // 2
    hk, hv = dk // h, dv // h
    tb = _tile(seq, GLA_ROWS)
    assert rows_per_flag % tb == 0
    proj3 = proj.reshape(batch, seq, proj.shape[1])
    gates3 = gates.reshape(batch, seq, dk)
    in_specs = [
        pl.BlockSpec((None, tb, hk), lambda b, hh, t, *_: (b, t, hh)),
        pl.BlockSpec((None, tb, hk), lambda b, hh, t, *_: (b, t, h + hh)),
        pl.BlockSpec((None, tb, hv), lambda b, hh, t, *_: (b, t, 2 * dk // hv + hh)),
        pl.BlockSpec((None, tb, hv), lambda b, hh, t, *_: (b, t, (2 * dk + dv) // hv + hh)),
        pl.BlockSpec((None, tb, hk), lambda b, hh, t, *_: (b, t, hh)),
        pl.BlockSpec((1, hv), lambda b, hh, t, *_: (0, 0)),
    ]
    args = [proj3, proj3, proj3, proj3, gates3, g_norm.reshape(1, hv)]
    has_state0 = state0 is not None
    if has_state0:
        in_specs.append(pl.BlockSpec((None, None, None, hk, hv), lambda b, hh, t, *_: (layer, b, hh, 0, 0)))
        args.append(state0)
    one_state = pl.BlockSpec((None, None, hk, hv), lambda b, hh, t, *_: (b, hh, 0, 0))
    in_specs += [one_state] * len(prev_states)
    args += prev_states
    if stacked:
        state_spec = pl.BlockSpec((n_layers, None, None, hk, hv), lambda b, hh, t, *_: (0, b, hh, 0, 0))
        state_shape = jax.ShapeDtypeStruct((n_layers, batch, h, hk, hv), F32)
    else:
        state_spec = one_state
        state_shape = jax.ShapeDtypeStruct((batch, h, hk, hv), F32)
    og, states = pl.pallas_call(
        functools.partial(_gla_kernel, has_state0=has_state0, n_prev=len(prev_states), n_chunks=tb // CHUNK,
                          rows_per_flag=rows_per_flag),
        grid_spec=pltpu.PrefetchScalarGridSpec(
            num_scalar_prefetch=1,
            grid=(batch, h, seq // tb),
            in_specs=in_specs,
            out_specs=[pl.BlockSpec((None, tb, hv), lambda b, hh, t, *_: (b, t, hh)), state_spec],
            scratch_shapes=[pltpu.VMEM((hk, hv), F32)],
        ),
        out_shape=[jax.ShapeDtypeStruct((batch, seq, dv), BF16), state_shape],
        compiler_params=_params(("parallel", "parallel", "arbitrary")),
        name=name,
    )(mild, *args)
    return og.reshape(batch * seq, dv), states


def _attn_kernel(*refs, qb, past, hd, hg):
    if past:
        q_ref, k_ref, v_ref, kc_ref, vc_ref, bm_ref, o_ref = refs
    else:
        q_ref, k_ref, v_ref, bm_ref, o_ref = refs
        kc_ref = vc_ref = None
    tq = q_ref.shape[0]
    left = LEFT_CHUNKS * CHUNK
    wfull = bm_ref.shape[-1]
    blocks = [(hh, i) for hh in range(hg) for i in range(tq // qb)]

    def window(i):
        return max(0, i * qb + past - left), i * qb + past + qb

    def rows(new_ref, cache_ref, k0, k1, hh):
        cs = slice(hh * hd, (hh + 1) * hd)
        parts = []
        if k0 < past:
            parts.append(cache_ref[k0:min(k1, past), cs])
        if k1 > past:
            parts.append(new_ref[max(k0, past) - past:k1 - past, cs])
        return parts[0] if len(parts) == 1 else jnp.concatenate(parts, axis=0)

    def scores(hh, i):
        k0, k1 = window(i)
        s = lax.dot_general(q_ref[i * qb:(i + 1) * qb, hh * hd:(hh + 1) * hd], rows(k_ref, kc_ref, k0, k1, hh),
                            (((1,), (1,)), ((), ())), preferred_element_type=F32)
        return s + bm_ref[hh, :, wfull - (k1 - k0):]

    def store(hh, i, o, denom):
        o_ref[i * qb:(i + 1) * qb, hh * hd:(hh + 1) * hd] = (o * (1.0 / denom)).astype(o_ref.dtype)

    s_next = scores(*blocks[0])
    pending = None
    for n, (hh, i) in enumerate(blocks):
        s = s_next
        if n + 1 < len(blocks):
            s_next = scores(*blocks[n + 1])
        e = jnp.exp2(s - jnp.max(s, axis=-1, keepdims=True))
        k0, k1 = window(i)
        o = jnp.dot(e.astype(BF16), rows(v_ref, vc_ref, k0, k1, hh), preferred_element_type=F32)
        if pending is not None:
            store(*pending)
        pending = (hh, i, o, jnp.sum(e, axis=-1, keepdims=True))
    store(*pending)


def _attention(q, kv, bm, cache_k=None, cache_v=None, name="attn"):
    batch, tq, d = q.shape
    hd = d // ATT_HEADS
    hg = ATT_HEAD_GROUP if tq > bm.shape[1] else ATT_HEADS
    gw = hg * hd
    qb = bm.shape[1]
    past = 0 if cache_k is None else cache_k.shape[1]
    assert PAST_LEN % CHUNK == 0 and past in (0, min(LEFT_CHUNKS * CHUNK, PAST_LEN))
    in_specs = [
        pl.BlockSpec((None, tq, gw), lambda b, g: (b, 0, g)),
        pl.BlockSpec((None, tq, gw), lambda b, g: (b, 0, g)),
        pl.BlockSpec((None, tq, gw), lambda b, g: (b, 0, d // gw + g)),
    ]
    args = [q, kv, kv]
    if past:
        in_specs += [pl.BlockSpec((None, past, gw), lambda b, g: (b, 0, g))] * 2
        args += [cache_k, cache_v]
    in_specs.append(pl.BlockSpec((hg, qb, bm.shape[2]), lambda b, g: (g, 0, 0)))
    args.append(bm)
    return pl.pallas_call(
        functools.partial(_attn_kernel, qb=qb, past=past, hd=hd, hg=hg),
        grid=(batch, ATT_HEADS // hg),
        in_specs=in_specs,
        out_specs=pl.BlockSpec((None, tq, gw), lambda b, g: (b, 0, g)),
        out_shape=jax.ShapeDtypeStruct((batch, tq, d), BF16),
        compiler_params=_params(("parallel", "parallel")),
        name=name,
    )(*args)


def _bias_mask(table, qb):
    left = LEFT_CHUNKS * CHUNK
    h = table.shape[0]
    t = table.astype(F32) * LOG2E
    sat = left + 2 * qb
    ext = jnp.concatenate([jnp.broadcast_to(t[:, :1], (h, sat)), t, jnp.broadcast_to(t[:, -1:], (h, sat))], axis=1)
    top = left + qb - 1
    n_w = left + 2 * qb - 1
    start = ext.shape[1] - 1 - (top + MAX_REL + sat)
    desc = ext[:, ::-1][:, start:start + n_w]
    w = jnp.concatenate([desc[:, qb - 1:], desc[:, :qb - 1]], axis=1)
    bias = jnp.tile(w, (1, qb))[:, :qb * (n_w - 1)].reshape(h, qb, n_w - 1)[:, :, :left + qb]
    r = jnp.arange(qb)[:, None]
    c = jnp.arange(left + qb)[None, :]
    qc = r // CHUNK
    kc = c // CHUNK - LEFT_CHUNKS
    allowed = (kc <= qc) & (kc >= qc - LEFT_CHUNKS)
    return jnp.where(allowed[None], bias, NEG_INF)


def _trunk(x, state0, cache_k, cache_v, w, keep):
    batch, seq, d = x.shape
    depth = w["norm_mix"].shape[0]
    n_a = depth // 2
    x = x.reshape(batch * seq, d)
    states = []
    k_rows = None
    v_rows = None
    kv16 = None
    qb = _tile(seq, ATT_QBLOCK)
    if cache_k is not None:
        cache16 = (cache_k.astype(BF16).reshape(batch, -1, d), cache_v.astype(BF16).reshape(batch, -1, d))
    for layer in range(depth):
        if layer < n_a:
            proj, gates, gmin = _gla_in(x, w["norm_mix"][layer], w["gla_w_in"][layer], w["gla_w_lr"][layer],
                                        w["gla_w_gk"][layer], w["gla_b_gk"][layer], 3 * d)
            mild = (jnp.min(gmin.reshape(gmin.shape[0], GLA_HEADS, -1), axis=-1)
                    >= -GLA_SINGLE_REF_LOG2_RANGE / CHUNK).astype(jnp.int32)
            og, s_end = _gla(proj, gates, mild, x.shape[0] // gmin.shape[0], w["gla_g_norm"][layer],
                             state0, layer, n_a, states if layer == n_a - 1 else [], batch, seq)
            states = s_end if layer == n_a - 1 else states + [s_end]
            x = _matmul_res(og, w["gla_w_out"][layer], x, name="gla_out")
        else:
            j = layer - n_a
            if layer == n_a:
                (kv16,) = _norm_matmul(x, w["norm_kv"], [w["w_kv"]], BF16, name="kv_proj")
                if keep == seq:
                    row_tiles = None
                else:
                    assert seq % keep == 0
                    per_seq = seq // keep
                    row_tiles = (keep, batch, lambda i: i * per_seq + per_seq - 1)
                k_rows, v_rows = _norm_matmul(x, w["norm_kv"], [w["w_kv"], w["w_kv"]], F32,
                                              row_tiles=row_tiles, col_windows=(d, [0, d]), name="kv_rows")
            (q16,) = _norm_matmul(x, w["norm_mix"][layer], [w["att_w_q"][j]], BF16,
                                  out_scale=(d // ATT_HEADS) ** -0.5 * LOG2E, name="q_proj")
            q3 = q16.reshape(batch, seq, d)
            bm = _bias_mask(w["att_rel_bias"][j], qb)
            kv3 = kv16.reshape(batch, seq, 2 * d)
            if cache_k is None:
                o = _attention(q3, kv3, bm)
            else:
                o = _attention(q3, kv3, bm, *cache16)
            x = _matmul_res(o.reshape(batch * seq, d), w["att_w_out"][j], x, name="att_out")
        g_final = w["norm_final"] if layer == depth - 1 else None
        x = _mlp(x, w["norm_ffn"][layer], w["w_ff1"][layer], w["w_ff2"][layer], g_final)
    hd = d // ATT_HEADS
    k = k_rows.reshape(batch, keep, ATT_HEADS, hd)
    v = v_rows.reshape(batch, keep, ATT_HEADS, hd)
    return x.reshape(batch, seq, d), states, k, v


def kernel(x_prompt, x_sample, state_gla, cache_k, cache_v, norm_mix, norm_ffn, w_ff1, w_ff2, gla_w_in, gla_w_gk, gla_b_gk, gla_g_norm, gla_w_out, norm_kv, w_kv, att_w_q, att_rel_bias, att_w_out, norm_final):
    d = x_prompt.shape[-1]
    dk = d // 2
    n_main = 2 * dk + 2 * d
    rank = gla_w_in.shape[-1] - n_main
    pad = LANES - rank
    w = {
        "norm_mix": norm_mix, "norm_ffn": norm_ffn, "norm_kv": norm_kv, "norm_final": norm_final,
        "w_ff1": w_ff1.astype(BF16), "w_ff2": w_ff2.astype(BF16),
        "gla_w_in": gla_w_in.astype(BF16),
        "gla_w_lr": jnp.pad(gla_w_in[:, :, n_main:], ((0, 0), (0, 0), (0, pad))).astype(BF16),
        "gla_w_gk": jnp.pad(gla_w_gk, ((0, 0), (0, pad), (0, 0))).astype(BF16),
        "gla_b_gk": gla_b_gk, "gla_g_norm": gla_g_norm,
        "gla_w_out": gla_w_out.astype(BF16),
        "w_kv": w_kv.astype(BF16), "att_w_q": att_w_q.astype(BF16),
        "att_rel_bias": att_rel_bias, "att_w_out": att_w_out.astype(BF16),
    }
    seq = x_prompt.shape[1]
    keep = min(LEFT_CHUNKS * CHUNK, seq)
    y_p, s_p, k_p, v_p = _trunk(x_prompt, None, None, None, w, keep)
    y_s, s_s, k_s, v_s = _trunk(x_sample, state_gla, cache_k, cache_v, w, x_sample.shape[1])
    return (y_p, y_s, s_p, k_p, v_p, s_s, k_s, v_s)
```

```python
import functools

import jax
import jax.numpy as jnp
from jax import lax
from jax.experimental import pallas as pl
from jax.experimental.pallas import tpu as pltpu

F32 = jnp.float32
BF16 = jnp.bfloat16

CHUNK = 64
GLA_HEADS = 4
GLA_SUB = 16
GLA_SUB_SHIFT = GLA_SUB.bit_length() - 1
GLA_GATE_NORM = 16.0
GLA_SINGLE_REF_LOG2_RANGE = 64.0
ATT_HEADS = 16
LEFT_CHUNKS = 8
PAST_LEN = 2048
MAX_REL = 128
EPS = 1e-6
NEG_INF = -1e30
LOG2E = 1.4426950408889634

LANES = 128
VMEM_LIMIT_BYTES = 60 * 2**20
ROW_TILE = 1024
GLA_IN_STEPS = 4
COL_TILE = 2048
COL_TILE_MULTI = 1024
RES_ROW_TILE = 512
FF_TILE = 1024
NORM_ROWS = 256
ATT_QBLOCK = 4 * CHUNK
ATT_HEAD_GROUP = 2
GLA_ROWS = 8 * CHUNK
GLA_CHUNK = 2 * CHUNK
SUBLANES = 8


def _tile(n, pref):
    if n <= pref:
        return n
    t = pref
    while n % t:
        t //= 2
    return t


def _params(sem):
    return pltpu.CompilerParams(dimension_semantics=sem, vmem_limit_bytes=VMEM_LIMIT_BYTES)


def _rmsnorm_rows(x_ref, g_ref, h_ref):
    rows = x_ref.shape[0]
    rc = _tile(rows, NORM_ROWS)

    def body(r, carry):
        sl = pl.ds(pl.multiple_of(r * rc, rc), rc)
        x = x_ref[sl, :]
        ms = jnp.mean(x * x, axis=-1, keepdims=True)
        h_ref[sl, :] = (x * lax.rsqrt(ms + EPS) * g_ref[...]).astype(h_ref.dtype)
        return carry

    lax.fori_loop(0, rows // rc, body, 0)


def _norm_matmul_kernel(*refs, n_w, out_scale):
    x_ref, g_ref = refs[:2]
    w_refs = refs[2:2 + n_w]
    out_refs = refs[2 + n_w:2 + 2 * n_w]
    h_ref = refs[2 + 2 * n_w]

    @pl.when(pl.program_id(1) == 0)
    def _():
        _rmsnorm_rows(x_ref, g_ref, h_ref)

    for w_ref, o_ref in zip(w_refs, out_refs):
        y = jnp.dot(h_ref[...], w_ref[...], preferred_element_type=F32)
        if out_scale is not None:
            y = y * out_scale
        o_ref[...] = y.astype(o_ref.dtype)


def _norm_matmul(x, g, ws, out_dtype, out_scale=None, row_tiles=None, col_windows=None, name="norm_matmul"):
    m, d = x.shape
    n, col0 = (ws[0].shape[1], [0] * len(ws)) if col_windows is None else col_windows
    if row_tiles is None:
        tm = _tile(m, ROW_TILE)
        n_tiles, block_of_tile = m // tm, lambda i: i
    else:
        tm, n_tiles, block_of_tile = row_tiles
    tn = _tile(n, COL_TILE if len(ws) == 1 else COL_TILE_MULTI)
    assert all(c0 % tn == 0 for c0 in col0)
    in_specs = [
        pl.BlockSpec((tm, d), lambda i, j: (block_of_tile(i), 0)),
        pl.BlockSpec((1, d), lambda i, j: (0, 0)),
    ] + [pl.BlockSpec((d, tn), functools.partial(lambda i, j, first: (0, first // tn + j), first=c0)) for c0 in col0]
    return pl.pallas_call(
        functools.partial(_norm_matmul_kernel, n_w=len(ws), out_scale=out_scale),
        grid=(n_tiles, n // tn),
        in_specs=in_specs,
        out_specs=[pl.BlockSpec((tm, tn), lambda i, j: (i, j)) for _ in ws],
        out_shape=[jax.ShapeDtypeStruct((n_tiles * tm, n), out_dtype) for _ in ws],
        scratch_shapes=[pltpu.VMEM((tm, d), BF16)],
        compiler_params=_params(("parallel", "arbitrary")),
        name=name,
    )(x, g.reshape(1, d), *ws)


def _gla_in_kernel(x_ref, g_ref, w_ref, wlr_ref, wgk_ref, bgk_ref, proj_ref, gates_ref, gmin_ref, h_ref, lr_ref):
    @pl.when(pl.program_id(1) == 0)
    def _():
        _rmsnorm_rows(x_ref, g_ref, h_ref)
        lr_ref[...] = jnp.dot(h_ref[...], wlr_ref[...], preferred_element_type=F32).astype(lr_ref.dtype)

    proj_ref[...] = jnp.dot(h_ref[...], w_ref[...], preferred_element_type=F32).astype(proj_ref.dtype)
    z = jnp.dot(lr_ref[...], wgk_ref[...], preferred_element_type=F32) + bgk_ref[...]
    gates = (jnp.minimum(z, 0.0) - jnp.log(1.0 + jnp.exp(-jnp.abs(z)))) * (LOG2E / GLA_GATE_NORM)
    gates_ref[...] = gates
    gmin_ref[...] = jnp.min(gates, axis=0, keepdims=True)


def _gla_in(x, g, w_in, w_lr, wgk, bgk, n_main):
    m, d = x.shape
    dk = wgk.shape[1]
    tm = _tile(m, ROW_TILE)
    steps = GLA_IN_STEPS
    tn, tg = n_main // steps, dk // steps
    assert tn * steps == n_main and tg * steps == dk and tn % LANES == 0 and tg % LANES == 0
    return pl.pallas_call(
        _gla_in_kernel,
        grid=(m // tm, steps),
        in_specs=[
            pl.BlockSpec((tm, d), lambda i, j: (i, 0)),
            pl.BlockSpec((1, d), lambda i, j: (0, 0)),
            pl.BlockSpec((d, tn), lambda i, j: (0, j)),
            pl.BlockSpec((d, w_lr.shape[1]), lambda i, j: (0, 0)),
            pl.BlockSpec((wgk.shape[0], tg), lambda i, j: (0, j)),
            pl.BlockSpec((1, tg), lambda i, j: (0, j)),
        ],
        out_specs=[
            pl.BlockSpec((tm, tn), lambda i, j: (i, j)),
            pl.BlockSpec((tm, tg), lambda i, j: (i, j)),
            pl.BlockSpec((None, 1, tg), lambda i, j: (i, 0, j)),
        ],
        out_shape=[
            jax.ShapeDtypeStruct((m, n_main), BF16),
            jax.ShapeDtypeStruct((m, dk), F32),
            jax.ShapeDtypeStruct((m // tm, 1, dk), F32),
        ],
        scratch_shapes=[pltpu.VMEM((tm, d), BF16), pltpu.VMEM((tm, w_lr.shape[1]), BF16)],
        compiler_params=_params(("parallel", "arbitrary")),
        name="gla_in",
    )(x, g.reshape(1, d), w_in, w_lr, wgk, bgk.reshape(1, dk))


def _matmul_res_kernel(a_ref, w_ref, x_ref, o_ref):
    o_ref[...] = x_ref[...] + jnp.dot(a_ref[...], w_ref[...], preferred_element_type=F32)


def _matmul_res(a, w, x, name="matmul_res"):
    m, k = a.shape
    n = w.shape[1]
    tm = _tile(m, RES_ROW_TILE)
    return pl.pallas_call(
        _matmul_res_kernel,
        grid=(m // tm,),
        in_specs=[
            pl.BlockSpec((tm, k), lambda i: (i, 0)),
            pl.BlockSpec((k, n), lambda i: (0, 0)),
            pl.BlockSpec((tm, n), lambda i: (i, 0)),
        ],
        out_specs=pl.BlockSpec((tm, n), lambda i: (i, 0)),
        out_shape=jax.ShapeDtypeStruct((m, n), F32),
        compiler_params=_params(("parallel",)),
        name=name,
    )(a, w, x)


def _mlp_kernel(*refs, final_norm):
    if final_norm:
        x_ref, g_ref, w1_ref, w2_ref, gf_ref, o_ref, h_ref = refs
    else:
        x_ref, g_ref, w1_ref, w2_ref, o_ref, h_ref = refs
        gf_ref = None
    f = pl.program_id(1)

    @pl.when(f == 0)
    def _():
        _rmsnorm_rows(x_ref, g_ref, h_ref)
        o_ref[...] = x_ref[...]

    a = jnp.dot(h_ref[...], w1_ref[...], preferred_element_type=F32)
    a = jnp.maximum(a, 0.0)
    a = (a * a).astype(BF16)
    o_ref[...] += jnp.dot(a, w2_ref[...], preferred_element_type=F32)

    if final_norm:
        @pl.when(f == pl.num_programs(1) - 1)
        def _():
            rows = o_ref.shape[0]
            rc = _tile(rows, NORM_ROWS)

            def body(r, carry):
                sl = pl.ds(pl.multiple_of(r * rc, rc), rc)
                y = o_ref[sl, :]
                ms = jnp.mean(y * y, axis=-1, keepdims=True)
                o_ref[sl, :] = y * lax.rsqrt(ms + EPS) * gf_ref[...]
                return carry

            lax.fori_loop(0, rows // rc, body, 0)


def _mlp(x, g, w1, w2, g_final=None, name="mlp"):
    m, d = x.shape
    ff = w1.shape[1]
    tm = _tile(m, ROW_TILE)
    tf = _tile(ff, FF_TILE)
    final_norm = g_final is not None
    in_specs = [
        pl.BlockSpec((tm, d), lambda i, f: (i, 0)),
        pl.BlockSpec((1, d), lambda i, f: (0, 0)),
        pl.BlockSpec((d, tf), lambda i, f: (0, f)),
        pl.BlockSpec((tf, d), lambda i, f: (f, 0)),
    ]
    args = [x, g.reshape(1, d), w1, w2]
    if final_norm:
        in_specs.append(pl.BlockSpec((1, d), lambda i, f: (0, 0)))
        args.append(g_final.reshape(1, d))
    return pl.pallas_call(
        functools.partial(_mlp_kernel, final_norm=final_norm),
        grid=(m // tm, ff // tf),
        in_specs=in_specs,
        out_specs=pl.BlockSpec((tm, d), lambda i, f: (i, 0)),
        out_shape=jax.ShapeDtypeStruct((m, d), F32),
        scratch_shapes=[pltpu.VMEM((tm, d), BF16)],
        compiler_params=_params(("parallel", "arbitrary")),
        name=name,
    )(*args)


def _chunk_cumsums(g, n_chunks, c):
    row = lax.broadcasted_iota(jnp.int32, (c, c), 0)
    col = lax.broadcasted_iota(jnp.int32, (c, c), 1)
    tri = (col <= row).astype(BF16)
    g_hi = g.astype(BF16)
    g_r = g - g_hi.astype(F32)
    g_mid = g_r.astype(BF16)
    g_lo = (g_r - g_mid.astype(F32)).astype(BF16)
    out = []
    for ci in range(n_chunks):
        r = slice(ci * c, (ci + 1) * c)
        out.append(jnp.dot(tri, g_hi[r], preferred_element_type=F32)
                   + jnp.dot(tri, g_mid[r], preferred_element_type=F32)
                   + jnp.dot(tri, g_lo[r], preferred_element_type=F32))
    return out


def _gla_offdiag(q, k, b):
    c = q.shape[0]
    a_rows = [jnp.zeros((GLA_SUB, c), F32)]
    for l in range(1, c // GLA_SUB):
        lo = l * GLA_SUB
        ref = b[lo - 1:lo, :]
        q_ref = q[lo:lo + GLA_SUB, :] * jnp.exp2(b[lo:lo + GLA_SUB, :] - ref)
        k_ref = k * jnp.exp2(jnp.minimum(ref - b, 0.0))
        a_rows.append(lax.dot_general(q_ref.astype(BF16), k_ref.astype(BF16),
                                      (((1,), (1,)), ((), ())), preferred_element_type=F32))
    return jnp.concatenate(a_rows, axis=0)


def _gla_diag(q, k, b):
    c, hk = q.shape
    nsub = c // GLA_SUB
    q3 = q.reshape(nsub, GLA_SUB, hk)
    k3 = k.reshape(nsub, GLA_SUB, hk)
    b3 = b.reshape(nsub, GLA_SUB, hk)
    col3 = lax.broadcasted_iota(jnp.int32, (nsub, SUBLANES, c), 2)
    blk3 = lax.broadcasted_iota(jnp.int32, (nsub, SUBLANES, c), 0) * GLA_SUB
    top = jnp.zeros((nsub, SUBLANES, c), F32)
    bot = jnp.zeros((nsub, GLA_SUB - SUBLANES, c), F32)
    for e in range(GLA_SUB):
        lo = 0 if e < SUBLANES else SUBLANES
        k_e = jnp.broadcast_to(k3[:, e:e + 1, :], (nsub, GLA_SUB - lo, hk))
        b_e = jnp.broadcast_to(b3[:, e:e + 1, :], (nsub, GLA_SUB - lo, hk))
        t = q3[:, lo:, :] * k_e * jnp.exp2(jnp.minimum(b3[:, lo:, :] - b_e, 0.0))
        ts = jnp.sum(t, axis=-1, keepdims=True)
        hit = col3 == blk3 + e
        if lo == 0:
            top = jnp.where(hit, ts[:, :SUBLANES, :], top)
            bot = jnp.where(hit, ts[:, SUBLANES:, :], bot)
        else:
            bot = jnp.where(hit, ts, bot)
    return jnp.concatenate([top, bot], axis=1).reshape(c, c)


def _gla_kernel(*refs, has_state0, n_prev, n_chunks, chunk, rows_per_flag, stacked):
    mild_ref, q_ref, k_ref, v_ref, gate_ref, g_ref, gn_ref = refs[:7]
    s0_ref = refs[7] if has_state0 else None
    first_prev = 8 if has_state0 else 7
    prev_refs = refs[first_prev:first_prev + n_prev]
    o_ref, sfin_ref, s_scr = refs[-3:]
    t = pl.program_id(2)
    hk = q_ref.shape[-1]
    scale = hk ** -0.5
    c = chunk

    @pl.when(t == 0)
    def _():
        if has_state0:
            s_scr[...] = s0_ref[...]
        else:
            s_scr[...] = jnp.zeros_like(s_scr)

    bs = _chunk_cumsums(g_ref[...], n_chunks, c)
    row = lax.broadcasted_iota(jnp.int32, (c, c), 0)
    col = lax.broadcasted_iota(jnp.int32, (c, c), 1)
    row_blk = jnp.right_shift(row, GLA_SUB_SHIFT)
    col_blk = jnp.right_shift(col, GLA_SUB_SHIFT)
    below = col_blk < row_blk
    on_diag = (col_blk == row_blk) & (col <= row)

    def prepare(ci, single_ref):
        r = slice(ci * c, (ci + 1) * c)
        q = q_ref[r, :].astype(F32) * scale
        k = k_ref[r, :].astype(F32)
        b = bs[ci]
        b_end = b[c - 1:c, :]
        decay_col = jnp.transpose(jnp.broadcast_to(jnp.exp2(b_end), (LANES, hk)))[:, :1]
        k_dec = (k * jnp.exp2(b_end - b)).astype(BF16)
        if single_ref:
            q_up = (q * jnp.exp2(b - b_end)).astype(BF16)
            scores = lax.dot_general(q_up, k_dec, (((1,), (1,)), ((), ())), preferred_element_type=F32)
        else:
            scores = _gla_offdiag(q, k, b)
        return dict(q=q, k=k, b=b, q_dec=(q * jnp.exp2(b)).astype(BF16), k_dec=k_dec,
                    decay_col=decay_col, scores=scores)

    def finish(ci, o):
        r = slice(ci * c, (ci + 1) * c)
        ms = jnp.mean(o * o, axis=-1, keepdims=True)
        y = o * lax.rsqrt(ms + EPS) * gn_ref[...]
        gate = gate_ref[r, :].astype(F32)
        o_ref[r, :] = (y * (gate * (1.0 / (1.0 + jnp.exp(-gate))))).astype(o_ref.dtype)

    def run(single_ref):
        s = s_scr[...]
        cur = prepare(0, single_ref)
        prev_out = None
        prev_upd = None
        for ci in range(n_chunks):
            if prev_upd is not None:
                s = prev_upd[0] * s + prev_upd[1]
            o_inter = jnp.dot(cur["q_dec"], s.astype(BF16), preferred_element_type=F32)
            if prev_out is not None:
                finish(ci - 1, prev_out[0] + prev_out[1])
            if single_ref:
                a = jnp.where(col <= row, cur["scores"], 0.0).astype(BF16)
            else:
                a_diag = _gla_diag(cur["q"], cur["k"], cur["b"])
                a = jnp.where(below, cur["scores"], jnp.where(on_diag, a_diag, 0.0)).astype(BF16)
            v16 = v_ref[ci * c:(ci + 1) * c, :]
            prev_out = (jnp.dot(a, v16, preferred_element_type=F32), o_inter)
            prev_upd = (cur["decay_col"],
                        lax.dot_general(cur["k_dec"], v16, (((0,), (0,)), ((), ())), preferred_element_type=F32))
            if ci + 1 < n_chunks:
                cur = prepare(ci + 1, single_ref)
        s_scr[...] = prev_upd[0] * s + prev_upd[1]
        finish(n_chunks - 1, prev_out[0] + prev_out[1])

    tile = (pl.program_id(0) * (pl.num_programs(2) * n_chunks * c) + t * (n_chunks * c)) // rows_per_flag
    mild = mild_ref[tile, pl.program_id(1)] != 0

    @pl.when(mild)
    def _():
        run(True)

    @pl.when(jnp.logical_not(mild))
    def _():
        run(False)

    @pl.when(t == pl.num_programs(2) - 1)
    def _():
        if stacked:
            for i, p_ref in enumerate(prev_refs):
                sfin_ref[i] = p_ref[...]
            sfin_ref[n_prev] = s_scr[...]
        else:
            sfin_ref[...] = s_scr[...]


def _gla(proj, gates, mild, rows_per_flag, g_norm, state0, layer, n_layers, prev_states, batch, seq, name="gla"):
    h = GLA_HEADS
    stacked = layer == n_layers - 1
    assert len(prev_states) == (layer if stacked else 0)
    dv = proj.shape[1] // 3
    dk = dv // 2
    hk, hv = dk // h, dv // h
    tb = _tile(seq, GLA_ROWS)
    assert rows_per_flag % tb == 0
    proj3 = proj.reshape(batch, seq, proj.shape[1])
    gates3 = gates.reshape(batch, seq, dk)
    chunk = _tile(tb, GLA_CHUNK)
    in_specs = [
        pl.BlockSpec((None, tb, hk), lambda b, hh, t, *_: (b, t, hh)),
        pl.BlockSpec((None, tb, hk), lambda b, hh, t, *_: (b, t, h + hh)),
        pl.BlockSpec((None, tb, hv), lambda b, hh, t, *_: (b, t, 2 * dk // hv + hh)),
        pl.BlockSpec((None, tb, hv), lambda b, hh, t, *_: (b, t, (2 * dk + dv) // hv + hh)),
        pl.BlockSpec((None, tb, hk), lambda b, hh, t, *_: (b, t, hh)),
        pl.BlockSpec((1, hv), lambda b, hh, t, *_: (0, 0)),
    ]
    args = [proj3, proj3, proj3, proj3, gates3, g_norm.reshape(1, hv)]
    has_state0 = state0 is not None
    if has_state0:
        in_specs.append(pl.BlockSpec((None, None, None, hk, hv), lambda b, hh, t, *_: (layer, b, hh, 0, 0)))
        args.append(state0)
    one_state = pl.BlockSpec((None, None, hk, hv), lambda b, hh, t, *_: (b, hh, 0, 0))
    in_specs += [one_state] * len(prev_states)
    args += prev_states
    if stacked:
        state_spec = pl.BlockSpec((n_layers, None, None, hk, hv), lambda b, hh, t, *_: (0, b, hh, 0, 0))
        state_shape = jax.ShapeDtypeStruct((n_layers, batch, h, hk, hv), F32)
    else:
        state_spec = one_state
        state_shape = jax.ShapeDtypeStruct((batch, h, hk, hv), F32)
    og, states = pl.pallas_call(
        functools.partial(_gla_kernel, has_state0=has_state0, n_prev=len(prev_states), n_chunks=tb // chunk,
                          chunk=chunk, rows_per_flag=rows_per_flag, stacked=stacked),
        grid_spec=pltpu.PrefetchScalarGridSpec(
            num_scalar_prefetch=1,
            grid=(batch, h, seq // tb),
            in_specs=in_specs,
            out_specs=[pl.BlockSpec((None, tb, hv), lambda b, hh, t, *_: (b, t, hh)), state_spec],
            scratch_shapes=[pltpu.VMEM((hk, hv), F32)],
        ),
        out_shape=[jax.ShapeDtypeStruct((batch, seq, dv), BF16), state_shape],
        compiler_params=_params(("parallel", "parallel", "arbitrary")),
        name=name,
    )(mild, *args)
    return og.reshape(batch * seq, dv), states


def _attn_kernel(*refs, qb, past, hd, hg):
    if past:
        q_ref, k_ref, v_ref, kc_ref, vc_ref, bm_ref, o_ref = refs
    else:
        q_ref, k_ref, v_ref, bm_ref, o_ref = refs
        kc_ref = vc_ref = None
    tq = q_ref.shape[0]
    left = LEFT_CHUNKS * CHUNK
    wfull = bm_ref.shape[-1]
    blocks = [(hh, i) for hh in range(hg) for i in range(tq // qb)]

    def window(i):
        return max(0, i * qb + past - left), i * qb + past + qb

    def rows(new_ref, cache_ref, k0, k1, hh):
        cs = slice(hh * hd, (hh + 1) * hd)
        parts = []
        if k0 < past:
            parts.append(cache_ref[k0:min(k1, past), cs])
        if k1 > past:
            parts.append(new_ref[max(k0, past) - past:k1 - past, cs])
        return parts[0] if len(parts) == 1 else jnp.concatenate(parts, axis=0)

    def scores(hh, i):
        k0, k1 = window(i)
        s = lax.dot_general(q_ref[i * qb:(i + 1) * qb, hh * hd:(hh + 1) * hd], rows(k_ref, kc_ref, k0, k1, hh),
                            (((1,), (1,)), ((), ())), preferred_element_type=F32)
        return s + bm_ref[hh, :, wfull - (k1 - k0):]

    def store(hh, i, o, denom):
        o_ref[i * qb:(i + 1) * qb, hh * hd:(hh + 1) * hd] = (o * (1.0 / denom)).astype(o_ref.dtype)

    s_next = scores(*blocks[0])
    pending = None
    for n, (hh, i) in enumerate(blocks):
        s = s_next
        if n + 1 < len(blocks):
            s_next = scores(*blocks[n + 1])
        e = jnp.exp2(s - jnp.max(s, axis=-1, keepdims=True))
        k0, k1 = window(i)
        o = jnp.dot(e.astype(BF16), rows(v_ref, vc_ref, k0, k1, hh), preferred_element_type=F32)
        if pending is not None:
            store(*pending)
        pending = (hh, i, o, jnp.sum(e, axis=-1, keepdims=True))
    store(*pending)


def _attention(q, kv, bm, cache_k=None, cache_v=None, name="attn"):
    batch, tq, d = q.shape
    hd = d // ATT_HEADS
    hg = ATT_HEAD_GROUP if tq > bm.shape[1] else ATT_HEADS
    gw = hg * hd
    qb = bm.shape[1]
    past = 0 if cache_k is None else cache_k.shape[1]
    assert PAST_LEN % CHUNK == 0 and past in (0, min(LEFT_CHUNKS * CHUNK, PAST_LEN))
    in_specs = [
        pl.BlockSpec((None, tq, gw), lambda b, g: (b, 0, g)),
        pl.BlockSpec((None, tq, gw), lambda b, g: (b, 0, g)),
        pl.BlockSpec((None, tq, gw), lambda b, g: (b, 0, d // gw + g)),
    ]
    args = [q, kv, kv]
    if past:
        in_specs += [pl.BlockSpec((None, past, gw), lambda b, g: (b, 0, g))] * 2
        args += [cache_k, cache_v]
    in_specs.append(pl.BlockSpec((hg, qb, bm.shape[2]), lambda b, g: (g, 0, 0)))
    args.append(bm)
    return pl.pallas_call(
        functools.partial(_attn_kernel, qb=qb, past=past, hd=hd, hg=hg),
        grid=(batch, ATT_HEADS // hg),
        in_specs=in_specs,
        out_specs=pl.BlockSpec((None, tq, gw), lambda b, g: (b, 0, g)),
        out_shape=jax.ShapeDtypeStruct((batch, tq, d), BF16),
        compiler_params=_params(("parallel", "parallel")),
        name=name,
    )(*args)


def _bias_mask(table, qb):
    left = LEFT_CHUNKS * CHUNK
    h = table.shape[0]
    t = table.astype(F32) * LOG2E
    sat = left + 2 * qb
    ext = jnp.concatenate([jnp.broadcast_to(t[:, :1], (h, sat)), t, jnp.broadcast_to(t[:, -1:], (h, sat))], axis=1)
    top = left + qb - 1
    n_w = left + 2 * qb - 1
    start = ext.shape[1] - 1 - (top + MAX_REL + sat)
    desc = ext[:, ::-1][:, start:start + n_w]
    w = jnp.concatenate([desc[:, qb - 1:], desc[:, :qb - 1]], axis=1)
    bias = jnp.tile(w, (1, qb))[:, :qb * (n_w - 1)].reshape(h, qb, n_w - 1)[:, :, :left + qb]
    r = jnp.arange(qb)[:, None]
    c = jnp.arange(left + qb)[None, :]
    qc = r // CHUNK
    kc = c // CHUNK - LEFT_CHUNKS
    allowed = (kc <= qc) & (kc >= qc - LEFT_CHUNKS)
    return jnp.where(allowed[None], bias, NEG_INF)


def _trunk(x, state0, cache_k, cache_v, w, keep):
    batch, seq, d = x.shape
    depth = w["norm_mix"].shape[0]
    n_a = depth // 2
    x = x.reshape(batch * seq, d)
    states = []
    k_rows = None
    v_rows = None
    kv16 = None
    qb = _tile(seq, ATT_QBLOCK)
    if cache_k is not None:
        cache16 = (cache_k.astype(BF16).reshape(batch, -1, d), cache_v.astype(BF16).reshape(batch, -1, d))
    for layer in range(depth):
        if layer < n_a:
            proj, gates, gmin = _gla_in(x, w["norm_mix"][layer], w["gla_w_in"][layer], w["gla_w_lr"][layer],
                                        w["gla_w_gk"][layer], w["gla_b_gk"][layer], 3 * d)
            mild = (jnp.min(gmin.reshape(gmin.shape[0], GLA_HEADS, -1), axis=-1)
                    >= -GLA_SINGLE_REF_LOG2_RANGE / GLA_CHUNK).astype(jnp.int32)
            og, s_end = _gla(proj, gates, mild, x.shape[0] // gmin.shape[0], w["gla_g_norm"][layer],
                             state0, layer, n_a, states if layer == n_a - 1 else [], batch, seq)
            states = s_end if layer == n_a - 1 else states + [s_end]
            x = _matmul_res(og, w["gla_w_out"][layer], x, name="gla_out")
        else:
            j = layer - n_a
            if layer == n_a:
                (kv16,) = _norm_matmul(x, w["norm_kv"], [w["w_kv"]], BF16, name="kv_proj")
                if keep == seq:
                    row_tiles = None
                else:
                    assert seq % keep == 0
                    per_seq = seq // keep
                    row_tiles = (keep, batch, lambda i: i * per_seq + per_seq - 1)
                k_rows, v_rows = _norm_matmul(x, w["norm_kv"], [w["w_kv"], w["w_kv"]], F32,
                                              row_tiles=row_tiles, col_windows=(d, [0, d]), name="kv_rows")
            (q16,) = _norm_matmul(x, w["norm_mix"][layer], [w["att_w_q"][j]], BF16,
                                  out_scale=(d // ATT_HEADS) ** -0.5 * LOG2E, name="q_proj")
            q3 = q16.reshape(batch, seq, d)
            bm = _bias_mask(w["att_rel_bias"][j], qb)
            kv3 = kv16.reshape(batch, seq, 2 * d)
            if cache_k is None:
                o = _attention(q3, kv3, bm)
            else:
                o = _attention(q3, kv3, bm, *cache16)
            x = _matmul_res(o.reshape(batch * seq, d), w["att_w_out"][j], x, name="att_out")
        g_final = w["norm_final"] if layer == depth - 1 else None
        x = _mlp(x, w["norm_ffn"][layer], w["w_ff1"][layer], w["w_ff2"][layer], g_final)
    hd = d // ATT_HEADS
    k = k_rows.reshape(batch, keep, ATT_HEADS, hd)
    v = v_rows.reshape(batch, keep, ATT_HEADS, hd)
    return x.reshape(batch, seq, d), states, k, v


def kernel(x_prompt, x_sample, state_gla, cache_k, cache_v, norm_mix, norm_ffn, w_ff1, w_ff2, gla_w_in, gla_w_gk, gla_b_gk, gla_g_norm, gla_w_out, norm_kv, w_kv, att_w_q, att_rel_bias, att_w_out, norm_final):
    d = x_prompt.shape[-1]
    dk = d // 2
    n_main = 2 * dk + 2 * d
    rank = gla_w_in.shape[-1] - n_main
    pad = LANES - rank
    w = {
        "norm_mix": norm_mix, "norm_ffn": norm_ffn, "norm_kv": norm_kv, "norm_final": norm_final,
        "w_ff1": w_ff1.astype(BF16), "w_ff2": w_ff2.astype(BF16),
        "gla_w_in": gla_w_in.astype(BF16),
        "gla_w_lr": jnp.pad(gla_w_in[:, :, n_main:], ((0, 0), (0, 0), (0, pad))).astype(BF16),
        "gla_w_gk": jnp.pad(gla_w_gk, ((0, 0), (0, pad), (0, 0))).astype(BF16),
        "gla_b_gk": gla_b_gk, "gla_g_norm": gla_g_norm,
        "gla_w_out": gla_w_out.astype(BF16),
        "w_kv": w_kv.astype(BF16), "att_w_q": att_w_q.astype(BF16),
        "att_rel_bias": att_rel_bias, "att_w_out": att_w_out.astype(BF16),
    }
    seq = x_prompt.shape[1]
    keep = min(LEFT_CHUNKS * CHUNK, seq)
    y_p, s_p, k_p, v_p = _trunk(x_prompt, None, None, None, w, keep)
    y_s, s_s, k_s, v_s = _trunk(x_sample, state_gla, cache_k, cache_v, w, x_sample.shape[1])
    return (y_p, y_s, s_p, k_p, v_p, s_s, k_s, v_s)
```

```python
import functools

import jax
import jax.numpy as jnp
from jax import lax
from jax.experimental import pallas as pl
from jax.experimental.pallas import tpu as pltpu

F32 = jnp.float32
BF16 = jnp.bfloat16

CHUNK = 64
GLA_HEADS = 4
GLA_SUB = 16
GLA_SUB_SHIFT = GLA_SUB.bit_length() - 1
GLA_GATE_NORM = 16.0
GLA_SINGLE_REF_LOG2_RANGE = 64.0
ATT_HEADS = 16
LEFT_CHUNKS = 8
PAST_LEN = 2048
MAX_REL = 128
EPS = 1e-6
NEG_INF = -1e30
LOG2E = 1.4426950408889634

LANES = 128
VMEM_LIMIT_BYTES = 60 * 2**20
ROW_TILE = 1024
GLA_IN_STEPS = 4
COL_TILE = 2048
COL_TILE_MULTI = 1024
RES_ROW_TILE = 512
FF_TILE = 1024
NORM_ROWS = 256
ATT_QBLOCK = 4 * CHUNK
ATT_HEAD_GROUP = 2
GLA_ROWS = 8 * CHUNK
GLA_CHUNK = 2 * CHUNK
SUBLANES = 8


def _tile(n, pref):
    if n <= pref:
        return n
    t = pref
    while n % t:
        t //= 2
    return t


def _params(sem):
    return pltpu.CompilerParams(dimension_semantics=sem, vmem_limit_bytes=VMEM_LIMIT_BYTES)


def _rmsnorm_rows(x_ref, g_ref, h_ref):
    rows = x_ref.shape[0]
    rc = _tile(rows, NORM_ROWS)

    def body(r, carry):
        sl = pl.ds(pl.multiple_of(r * rc, rc), rc)
        x = x_ref[sl, :]
        ms = jnp.mean(x * x, axis=-1, keepdims=True)
        h_ref[sl, :] = (x * lax.rsqrt(ms + EPS) * g_ref[...]).astype(h_ref.dtype)
        return carry

    lax.fori_loop(0, rows // rc, body, 0)


def _norm_matmul_kernel(*refs, n_w, out_scale):
    x_ref, g_ref = refs[:2]
    w_refs = refs[2:2 + n_w]
    out_refs = refs[2 + n_w:2 + 2 * n_w]
    h_ref = refs[2 + 2 * n_w]

    @pl.when(pl.program_id(1) == 0)
    def _():
        _rmsnorm_rows(x_ref, g_ref, h_ref)

    for w_ref, o_ref in zip(w_refs, out_refs):
        y = jnp.dot(h_ref[...], w_ref[...], preferred_element_type=F32)
        if out_scale is not None:
            y = y * out_scale
        o_ref[...] = y.astype(o_ref.dtype)


def _norm_matmul(x, g, ws, out_dtype, out_scale=None, row_tiles=None, col_windows=None, name="norm_matmul"):
    m, d = x.shape
    n, col0 = (ws[0].shape[1], [0] * len(ws)) if col_windows is None else col_windows
    if row_tiles is None:
        tm = _tile(m, ROW_TILE)
        n_tiles, block_of_tile = m // tm, lambda i: i
    else:
        tm, n_tiles, block_of_tile = row_tiles
    tn = _tile(n, COL_TILE if len(ws) == 1 else COL_TILE_MULTI)
    assert all(c0 % tn == 0 for c0 in col0)
    in_specs = [
        pl.BlockSpec((tm, d), lambda i, j: (block_of_tile(i), 0)),
        pl.BlockSpec((1, d), lambda i, j: (0, 0)),
    ] + [pl.BlockSpec((d, tn), functools.partial(lambda i, j, first: (0, first // tn + j), first=c0)) for c0 in col0]
    return pl.pallas_call(
        functools.partial(_norm_matmul_kernel, n_w=len(ws), out_scale=out_scale),
        grid=(n_tiles, n // tn),
        in_specs=in_specs,
        out_specs=[pl.BlockSpec((tm, tn), lambda i, j: (i, j)) for _ in ws],
        out_shape=[jax.ShapeDtypeStruct((n_tiles * tm, n), out_dtype) for _ in ws],
        scratch_shapes=[pltpu.VMEM((tm, d), BF16)],
        compiler_params=_params(("parallel", "arbitrary")),
        name=name,
    )(x, g.reshape(1, d), *ws)


def _gla_in_kernel(x_ref, g_ref, w_ref, wlr_ref, wgk_ref, bgk_ref, proj_ref, gates_ref, gmin_ref, h_ref, lr_ref):
    @pl.when(pl.program_id(1) == 0)
    def _():
        _rmsnorm_rows(x_ref, g_ref, h_ref)
        lr_ref[...] = jnp.dot(h_ref[...], wlr_ref[...], preferred_element_type=F32).astype(lr_ref.dtype)

    proj_ref[...] = jnp.dot(h_ref[...], w_ref[...], preferred_element_type=F32).astype(proj_ref.dtype)
    z = jnp.dot(lr_ref[...], wgk_ref[...], preferred_element_type=F32) + bgk_ref[...]
    gates = (jnp.minimum(z, 0.0) - jnp.log(1.0 + jnp.exp(-jnp.abs(z)))) * (LOG2E / GLA_GATE_NORM)
    gates_ref[...] = gates
    rows, cols = gates.shape
    totals = jnp.sum(gates.reshape(rows // GLA_CHUNK, GLA_CHUNK, cols), axis=1)
    gmin_ref[...] = jnp.min(totals, axis=0, keepdims=True)


def _gla_in(x, g, w_in, w_lr, wgk, bgk, n_main):
    m, d = x.shape
    dk = wgk.shape[1]
    tm = _tile(m, ROW_TILE)
    steps = GLA_IN_STEPS
    tn, tg = n_main // steps, dk // steps
    assert tn * steps == n_main and tg * steps == dk and tn % LANES == 0 and tg % LANES == 0
    assert tm % GLA_CHUNK == 0
    return pl.pallas_call(
        _gla_in_kernel,
        grid=(m // tm, steps),
        in_specs=[
            pl.BlockSpec((tm, d), lambda i, j: (i, 0)),
            pl.BlockSpec((1, d), lambda i, j: (0, 0)),
            pl.BlockSpec((d, tn), lambda i, j: (0, j)),
            pl.BlockSpec((d, w_lr.shape[1]), lambda i, j: (0, 0)),
            pl.BlockSpec((wgk.shape[0], tg), lambda i, j: (0, j)),
            pl.BlockSpec((1, tg), lambda i, j: (0, j)),
        ],
        out_specs=[
            pl.BlockSpec((tm, tn), lambda i, j: (i, j)),
            pl.BlockSpec((tm, tg), lambda i, j: (i, j)),
            pl.BlockSpec((None, 1, tg), lambda i, j: (i, 0, j)),
        ],
        out_shape=[
            jax.ShapeDtypeStruct((m, n_main), BF16),
            jax.ShapeDtypeStruct((m, dk), F32),
            jax.ShapeDtypeStruct((m // tm, 1, dk), F32),
        ],
        scratch_shapes=[pltpu.VMEM((tm, d), BF16), pltpu.VMEM((tm, w_lr.shape[1]), BF16)],
        compiler_params=_params(("parallel", "arbitrary")),
        name="gla_in",
    )(x, g.reshape(1, d), w_in, w_lr, wgk, bgk.reshape(1, dk))


def _matmul_res_kernel(a_ref, w_ref, x_ref, o_ref):
    o_ref[...] = x_ref[...] + jnp.dot(a_ref[...], w_ref[...], preferred_element_type=F32)


def _matmul_res(a, w, x, name="matmul_res"):
    m, k = a.shape
    n = w.shape[1]
    tm = _tile(m, RES_ROW_TILE)
    return pl.pallas_call(
        _matmul_res_kernel,
        grid=(m // tm,),
        in_specs=[
            pl.BlockSpec((tm, k), lambda i: (i, 0)),
            pl.BlockSpec((k, n), lambda i: (0, 0)),
            pl.BlockSpec((tm, n), lambda i: (i, 0)),
        ],
        out_specs=pl.BlockSpec((tm, n), lambda i: (i, 0)),
        out_shape=jax.ShapeDtypeStruct((m, n), F32),
        compiler_params=_params(("parallel",)),
        name=name,
    )(a, w, x)


def _mlp_kernel(*refs, final_norm):
    if final_norm:
        x_ref, g_ref, w1_ref, w2_ref, gf_ref, o_ref, h_ref = refs
    else:
        x_ref, g_ref, w1_ref, w2_ref, o_ref, h_ref = refs
        gf_ref = None
    f = pl.program_id(1)

    @pl.when(f == 0)
    def _():
        _rmsnorm_rows(x_ref, g_ref, h_ref)
        o_ref[...] = x_ref[...]

    a = jnp.dot(h_ref[...], w1_ref[...], preferred_element_type=F32)
    a = jnp.maximum(a, 0.0)
    a = (a * a).astype(BF16)
    o_ref[...] += jnp.dot(a, w2_ref[...], preferred_element_type=F32)

    if final_norm:
        @pl.when(f == pl.num_programs(1) - 1)
        def _():
            rows = o_ref.shape[0]
            rc = _tile(rows, NORM_ROWS)

            def body(r, carry):
                sl = pl.ds(pl.multiple_of(r * rc, rc), rc)
                y = o_ref[sl, :]
                ms = jnp.mean(y * y, axis=-1, keepdims=True)
                o_ref[sl, :] = y * lax.rsqrt(ms + EPS) * gf_ref[...]
                return carry

            lax.fori_loop(0, rows // rc, body, 0)


def _mlp(x, g, w1, w2, g_final=None, name="mlp"):
    m, d = x.shape
    ff = w1.shape[1]
    tm = _tile(m, ROW_TILE)
    tf = _tile(ff, FF_TILE)
    final_norm = g_final is not None
    in_specs = [
        pl.BlockSpec((tm, d), lambda i, f: (i, 0)),
        pl.BlockSpec((1, d), lambda i, f: (0, 0)),
        pl.BlockSpec((d, tf), lambda i, f: (0, f)),
        pl.BlockSpec((tf, d), lambda i, f: (f, 0)),
    ]
    args = [x, g.reshape(1, d), w1, w2]
    if final_norm:
        in_specs.append(pl.BlockSpec((1, d), lambda i, f: (0, 0)))
        args.append(g_final.reshape(1, d))
    return pl.pallas_call(
        functools.partial(_mlp_kernel, final_norm=final_norm),
        grid=(m // tm, ff // tf),
        in_specs=in_specs,
        out_specs=pl.BlockSpec((tm, d), lambda i, f: (i, 0)),
        out_shape=jax.ShapeDtypeStruct((m, d), F32),
        scratch_shapes=[pltpu.VMEM((tm, d), BF16)],
        compiler_params=_params(("parallel", "arbitrary")),
        name=name,
    )(*args)


def _chunk_cumsums(g, n_chunks, c):
    row = lax.broadcasted_iota(jnp.int32, (c, c), 0)
    col = lax.broadcasted_iota(jnp.int32, (c, c), 1)
    tri = (col <= row).astype(BF16)
    g_hi = g.astype(BF16)
    g_r = g - g_hi.astype(F32)
    g_mid = g_r.astype(BF16)
    g_lo = (g_r - g_mid.astype(F32)).astype(BF16)
    out = []
    for ci in range(n_chunks):
        r = slice(ci * c, (ci + 1) * c)
        out.append(jnp.dot(tri, g_hi[r], preferred_element_type=F32)
                   + jnp.dot(tri, g_mid[r], preferred_element_type=F32)
                   + jnp.dot(tri, g_lo[r], preferred_element_type=F32))
    return out


def _gla_offdiag(q, k, b):
    c = q.shape[0]
    a_rows = [jnp.zeros((GLA_SUB, c), F32)]
    for l in range(1, c // GLA_SUB):
        lo = l * GLA_SUB
        ref = b[lo - 1:lo, :]
        q_ref = q[lo:lo + GLA_SUB, :] * jnp.exp2(b[lo:lo + GLA_SUB, :] - ref)
        k_ref = k * jnp.exp2(jnp.minimum(ref - b, 0.0))
        a_rows.append(lax.dot_general(q_ref.astype(BF16), k_ref.astype(BF16),
                                      (((1,), (1,)), ((), ())), preferred_element_type=F32))
    return jnp.concatenate(a_rows, axis=0)


def _gla_diag(q, k, b):
    c, hk = q.shape
    nsub = c // GLA_SUB
    q3 = q.reshape(nsub, GLA_SUB, hk)
    k3 = k.reshape(nsub, GLA_SUB, hk)
    b3 = b.reshape(nsub, GLA_SUB, hk)
    col3 = lax.broadcasted_iota(jnp.int32, (nsub, SUBLANES, c), 2)
    blk3 = lax.broadcasted_iota(jnp.int32, (nsub, SUBLANES, c), 0) * GLA_SUB
    top = jnp.zeros((nsub, SUBLANES, c), F32)
    bot = jnp.zeros((nsub, GLA_SUB - SUBLANES, c), F32)
    for e in range(GLA_SUB):
        lo = 0 if e < SUBLANES else SUBLANES
        k_e = jnp.broadcast_to(k3[:, e:e + 1, :], (nsub, GLA_SUB - lo, hk))
        b_e = jnp.broadcast_to(b3[:, e:e + 1, :], (nsub, GLA_SUB - lo, hk))
        t = q3[:, lo:, :] * k_e * jnp.exp2(jnp.minimum(b3[:, lo:, :] - b_e, 0.0))
        ts = jnp.sum(t, axis=-1, keepdims=True)
        hit = col3 == blk3 + e
        if lo == 0:
            top = jnp.where(hit, ts[:, :SUBLANES, :], top)
            bot = jnp.where(hit, ts[:, SUBLANES:, :], bot)
        else:
            bot = jnp.where(hit, ts, bot)
    return jnp.concatenate([top, bot], axis=1).reshape(c, c)


def _gla_kernel(*refs, has_state0, n_prev, n_chunks, chunk, rows_per_flag, stacked):
    mild_ref, q_ref, k_ref, v_ref, gate_ref, g_ref, gn_ref = refs[:7]
    s0_ref = refs[7] if has_state0 else None
    first_prev = 8 if has_state0 else 7
    prev_refs = refs[first_prev:first_prev + n_prev]
    o_ref, sfin_ref, s_scr = refs[-3:]
    t = pl.program_id(2)
    hk = q_ref.shape[-1]
    scale = hk ** -0.5
    c = chunk

    @pl.when(t == 0)
    def _():
        if has_state0:
            s_scr[...] = s0_ref[...]
        else:
            s_scr[...] = jnp.zeros_like(s_scr)

    bs = _chunk_cumsums(g_ref[...], n_chunks, c)
    row = lax.broadcasted_iota(jnp.int32, (c, c), 0)
    col = lax.broadcasted_iota(jnp.int32, (c, c), 1)
    row_blk = jnp.right_shift(row, GLA_SUB_SHIFT)
    col_blk = jnp.right_shift(col, GLA_SUB_SHIFT)
    below = col_blk < row_blk
    on_diag = (col_blk == row_blk) & (col <= row)

    def prepare(ci, single_ref):
        r = slice(ci * c, (ci + 1) * c)
        q = q_ref[r, :].astype(F32) * scale
        k = k_ref[r, :].astype(F32)
        b = bs[ci]
        b_end = b[c - 1:c, :]
        decay_col = jnp.transpose(jnp.broadcast_to(jnp.exp2(b_end), (LANES, hk)))[:, :1]
        k_dec = (k * jnp.exp2(b_end - b)).astype(BF16)
        if single_ref:
            q_up = (q * jnp.exp2(b - b_end)).astype(BF16)
            scores = lax.dot_general(q_up, k_dec, (((1,), (1,)), ((), ())), preferred_element_type=F32)
        else:
            scores = _gla_offdiag(q, k, b)
        return dict(q=q, k=k, b=b, q_dec=(q * jnp.exp2(b)).astype(BF16), k_dec=k_dec,
                    decay_col=decay_col, scores=scores)

    def finish(ci, o):
        r = slice(ci * c, (ci + 1) * c)
        ms = jnp.mean(o * o, axis=-1, keepdims=True)
        y = o * lax.rsqrt(ms + EPS) * gn_ref[...]
        gate = gate_ref[r, :].astype(F32)
        o_ref[r, :] = (y * (gate * (1.0 / (1.0 + jnp.exp(-gate))))).astype(o_ref.dtype)

    def run(single_ref):
        s = s_scr[...]
        cur = prepare(0, single_ref)
        prev_out = None
        prev_upd = None
        for ci in range(n_chunks):
            if prev_upd is not None:
                s = prev_upd[0] * s + prev_upd[1]
            o_inter = jnp.dot(cur["q_dec"], s.astype(BF16), preferred_element_type=F32)
            if prev_out is not None:
                finish(ci - 1, prev_out[0] + prev_out[1])
            if single_ref:
                a = jnp.where(col <= row, cur["scores"], 0.0).astype(BF16)
            else:
                a_diag = _gla_diag(cur["q"], cur["k"], cur["b"])
                a = jnp.where(below, cur["scores"], jnp.where(on_diag, a_diag, 0.0)).astype(BF16)
            v16 = v_ref[ci * c:(ci + 1) * c, :]
            prev_out = (jnp.dot(a, v16, preferred_element_type=F32), o_inter)
            prev_upd = (cur["decay_col"],
                        lax.dot_general(cur["k_dec"], v16, (((0,), (0,)), ((), ())), preferred_element_type=F32))
            if ci + 1 < n_chunks:
                cur = prepare(ci + 1, single_ref)
        s_scr[...] = prev_upd[0] * s + prev_upd[1]
        finish(n_chunks - 1, prev_out[0] + prev_out[1])

    tile = (pl.program_id(0) * (pl.num_programs(2) * n_chunks * c) + t * (n_chunks * c)) // rows_per_flag
    mild = mild_ref[tile, pl.program_id(1)] != 0

    @pl.when(mild)
    def _():
        run(True)

    @pl.when(jnp.logical_not(mild))
    def _():
        run(False)

    @pl.when(t == pl.num_programs(2) - 1)
    def _():
        if stacked:
            for i, p_ref in enumerate(prev_refs):
                sfin_ref[i] = p_ref[...]
            sfin_ref[n_prev] = s_scr[...]
        else:
            sfin_ref[...] = s_scr[...]


def _gla(proj, gates, mild, rows_per_flag, g_norm, state0, layer, n_layers, prev_states, batch, seq, name="gla"):
    h = GLA_HEADS
    stacked = layer == n_layers - 1
    assert len(prev_states) == (layer if stacked else 0)
    dv = proj.shape[1] // 3
    dk = dv // 2
    hk, hv = dk // h, dv // h
    tb = _tile(seq, GLA_ROWS)
    assert rows_per_flag % tb == 0
    proj3 = proj.reshape(batch, seq, proj.shape[1])
    gates3 = gates.reshape(batch, seq, dk)
    chunk = _tile(tb, GLA_CHUNK)
    in_specs = [
        pl.BlockSpec((None, tb, hk), lambda b, hh, t, *_: (b, t, hh)),
        pl.BlockSpec((None, tb, hk), lambda b, hh, t, *_: (b, t, h + hh)),
        pl.BlockSpec((None, tb, hv), lambda b, hh, t, *_: (b, t, 2 * dk // hv + hh)),
        pl.BlockSpec((None, tb, hv), lambda b, hh, t, *_: (b, t, (2 * dk + dv) // hv + hh)),
        pl.BlockSpec((None, tb, hk), lambda b, hh, t, *_: (b, t, hh)),
        pl.BlockSpec((1, hv), lambda b, hh, t, *_: (0, 0)),
    ]
    args = [proj3, proj3, proj3, proj3, gates3, g_norm.reshape(1, hv)]
    has_state0 = state0 is not None
    if has_state0:
        in_specs.append(pl.BlockSpec((None, None, None, hk, hv), lambda b, hh, t, *_: (layer, b, hh, 0, 0)))
        args.append(state0)
    one_state = pl.BlockSpec((None, None, hk, hv), lambda b, hh, t, *_: (b, hh, 0, 0))
    in_specs += [one_state] * len(prev_states)
    args += prev_states
    if stacked:
        state_spec = pl.BlockSpec((n_layers, None, None, hk, hv), lambda b, hh, t, *_: (0, b, hh, 0, 0))
        state_shape = jax.ShapeDtypeStruct((n_layers, batch, h, hk, hv), F32)
    else:
        state_spec = one_state
        state_shape = jax.ShapeDtypeStruct((batch, h, hk, hv), F32)
    og, states = pl.pallas_call(
        functools.partial(_gla_kernel, has_state0=has_state0, n_prev=len(prev_states), n_chunks=tb // chunk,
                          chunk=chunk, rows_per_flag=rows_per_flag, stacked=stacked),
        grid_spec=pltpu.PrefetchScalarGridSpec(
            num_scalar_prefetch=1,
            grid=(batch, h, seq // tb),
            in_specs=in_specs,
            out_specs=[pl.BlockSpec((None, tb, hv), lambda b, hh, t, *_: (b, t, hh)), state_spec],
            scratch_shapes=[pltpu.VMEM((hk, hv), F32)],
        ),
        out_shape=[jax.ShapeDtypeStruct((batch, seq, dv), BF16), state_shape],
        compiler_params=_params(("parallel", "parallel", "arbitrary")),
        name=name,
    )(mild, *args)
    return og.reshape(batch * seq, dv), states


def _attn_kernel(*refs, qb, past, hd, hg):
    if past:
        q_ref, k_ref, v_ref, kc_ref, vc_ref, bm_ref, o_ref = refs
    else:
        q_ref, k_ref, v_ref, bm_ref, o_ref = refs
        kc_ref = vc_ref = None
    tq = q_ref.shape[0]
    left = LEFT_CHUNKS * CHUNK
    wfull = bm_ref.shape[-1]
    blocks = [(hh, i) for hh in range(hg) for i in range(tq // qb)]

    def window(i):
        return max(0, i * qb + past - left), i * qb + past + qb

    def rows(new_ref, cache_ref, k0, k1, hh):
        cs = slice(hh * hd, (hh + 1) * hd)
        parts = []
        if k0 < past:
            parts.append(cache_ref[k0:min(k1, past), cs])
        if k1 > past:
            parts.append(new_ref[max(k0, past) - past:k1 - past, cs])
        return parts[0] if len(parts) == 1 else jnp.concatenate(parts, axis=0)

    def scores(hh, i):
        k0, k1 = window(i)
        s = lax.dot_general(q_ref[i * qb:(i + 1) * qb, hh * hd:(hh + 1) * hd], rows(k_ref, kc_ref, k0, k1, hh),
                            (((1,), (1,)), ((), ())), preferred_element_type=F32)
        return s + bm_ref[hh, :, wfull - (k1 - k0):]

    def store(hh, i, o, denom):
        o_ref[i * qb:(i + 1) * qb, hh * hd:(hh + 1) * hd] = (o * (1.0 / denom)).astype(o_ref.dtype)

    s_next = scores(*blocks[0])
    pending = None
    for n, (hh, i) in enumerate(blocks):
        s = s_next
        if n + 1 < len(blocks):
            s_next = scores(*blocks[n + 1])
        e = jnp.exp2(s - jnp.max(s, axis=-1, keepdims=True))
        k0, k1 = window(i)
        o = jnp.dot(e.astype(BF16), rows(v_ref, vc_ref, k0, k1, hh), preferred_element_type=F32)
        if pending is not None:
            store(*pending)
        pending = (hh, i, o, jnp.sum(e, axis=-1, keepdims=True))
    store(*pending)


def _attention(q, kv, bm, cache_k=None, cache_v=None, name="attn"):
    batch, tq, d = q.shape
    hd = d // ATT_HEADS
    hg = ATT_HEAD_GROUP if tq > bm.shape[1] else ATT_HEADS
    gw = hg * hd
    qb = bm.shape[1]
    past = 0 if cache_k is None else cache_k.shape[1]
    assert PAST_LEN % CHUNK == 0 and past in (0, min(LEFT_CHUNKS * CHUNK, PAST_LEN))
    in_specs = [
        pl.BlockSpec((None, tq, gw), lambda b, g: (b, 0, g)),
        pl.BlockSpec((None, tq, gw), lambda b, g: (b, 0, g)),
        pl.BlockSpec((None, tq, gw), lambda b, g: (b, 0, d // gw + g)),
    ]
    args = [q, kv, kv]
    if past:
        in_specs += [pl.BlockSpec((None, past, gw), lambda b, g: (b, 0, g))] * 2
        args += [cache_k, cache_v]
    in_specs.append(pl.BlockSpec((hg, qb, bm.shape[2]), lambda b, g: (g, 0, 0)))
    args.append(bm)
    return pl.pallas_call(
        functools.partial(_attn_kernel, qb=qb, past=past, hd=hd, hg=hg),
        grid=(batch, ATT_HEADS // hg),
        in_specs=in_specs,
        out_specs=pl.BlockSpec((None, tq, gw), lambda b, g: (b, 0, g)),
        out_shape=jax.ShapeDtypeStruct((batch, tq, d), BF16),
        compiler_params=_params(("parallel", "parallel")),
        name=name,
    )(*args)


def _bias_mask(table, qb):
    left = LEFT_CHUNKS * CHUNK
    h = table.shape[0]
    t = table.astype(F32) * LOG2E
    sat = left + 2 * qb
    ext = jnp.concatenate([jnp.broadcast_to(t[:, :1], (h, sat)), t, jnp.broadcast_to(t[:, -1:], (h, sat))], axis=1)
    top = left + qb - 1
    n_w = left + 2 * qb - 1
    start = ext.shape[1] - 1 - (top + MAX_REL + sat)
    desc = ext[:, ::-1][:, start:start + n_w]
    w = jnp.concatenate([desc[:, qb - 1:], desc[:, :qb - 1]], axis=1)
    bias = jnp.tile(w, (1, qb))[:, :qb * (n_w - 1)].reshape(h, qb, n_w - 1)[:, :, :left + qb]
    r = jnp.arange(qb)[:, None]
    c = jnp.arange(left + qb)[None, :]
    qc = r // CHUNK
    kc = c // CHUNK - LEFT_CHUNKS
    allowed = (kc <= qc) & (kc >= qc - LEFT_CHUNKS)
    return jnp.where(allowed[None], bias, NEG_INF)


def _trunk(x, state0, cache_k, cache_v, w, keep):
    batch, seq, d = x.shape
    depth = w["norm_mix"].shape[0]
    n_a = depth // 2
    x = x.reshape(batch * seq, d)
    states = []
    k_rows = None
    v_rows = None
    kv16 = None
    qb = _tile(seq, ATT_QBLOCK)
    if cache_k is not None:
        cache16 = (cache_k.astype(BF16).reshape(batch, -1, d), cache_v.astype(BF16).reshape(batch, -1, d))
    for layer in range(depth):
        if layer < n_a:
            proj, gates, gmin = _gla_in(x, w["norm_mix"][layer], w["gla_w_in"][layer], w["gla_w_lr"][layer],
                                        w["gla_w_gk"][layer], w["gla_b_gk"][layer], 3 * d)
            mild = (jnp.min(gmin.reshape(gmin.shape[0], GLA_HEADS, -1), axis=-1)
                    >= -GLA_SINGLE_REF_LOG2_RANGE).astype(jnp.int32)
            og, s_end = _gla(proj, gates, mild, x.shape[0] // gmin.shape[0], w["gla_g_norm"][layer],
                             state0, layer, n_a, states if layer == n_a - 1 else [], batch, seq)
            states = s_end if layer == n_a - 1 else states + [s_end]
            x = _matmul_res(og, w["gla_w_out"][layer], x, name="gla_out")
        else:
            j = layer - n_a
            if layer == n_a:
                (kv16,) = _norm_matmul(x, w["norm_kv"], [w["w_kv"]], BF16, name="kv_proj")
                if keep == seq:
                    row_tiles = None
                else:
                    assert seq % keep == 0
                    per_seq = seq // keep
                    row_tiles = (keep, batch, lambda i: i * per_seq + per_seq - 1)
                k_rows, v_rows = _norm_matmul(x, w["norm_kv"], [w["w_kv"], w["w_kv"]], F32,
                                              row_tiles=row_tiles, col_windows=(d, [0, d]), name="kv_rows")
            (q16,) = _norm_matmul(x, w["norm_mix"][layer], [w["att_w_q"][j]], BF16,
                                  out_scale=(d // ATT_HEADS) ** -0.5 * LOG2E, name="q_proj")
            q3 = q16.reshape(batch, seq, d)
            bm = _bias_mask(w["att_rel_bias"][j], qb)
            kv3 = kv16.reshape(batch, seq, 2 * d)
            if cache_k is None:
                o = _attention(q3, kv3, bm)
            else:
                o = _attention(q3, kv3, bm, *cache16)
            x = _matmul_res(o.reshape(batch * seq, d), w["att_w_out"][j], x, name="att_out")
        g_final = w["norm_final"] if layer == depth - 1 else None
        x = _mlp(x, w["norm_ffn"][layer], w["w_ff1"][layer], w["w_ff2"][layer], g_final)
    hd = d // ATT_HEADS
    k = k_rows.reshape(batch, keep, ATT_HEADS, hd)
    v = v_rows.reshape(batch, keep, ATT_HEADS, hd)
    return x.reshape(batch, seq, d), states, k, v


def kernel(x_prompt, x_sample, state_gla, cache_k, cache_v, norm_mix, norm_ffn, w_ff1, w_ff2, gla_w_in, gla_w_gk, gla_b_gk, gla_g_norm, gla_w_out, norm_kv, w_kv, att_w_q, att_rel_bias, att_w_out, norm_final):
    d = x_prompt.shape[-1]
    dk = d // 2
    n_main = 2 * dk + 2 * d
    rank = gla_w_in.shape[-1] - n_main
    pad = LANES - rank
    w = {
        "norm_mix": norm_mix, "norm_ffn": norm_ffn, "norm_kv": norm_kv, "norm_final": norm_final,
        "w_ff1": w_ff1.astype(BF16), "w_ff2": w_ff2.astype(BF16),
        "gla_w_in": gla_w_in.astype(BF16),
        "gla_w_lr": jnp.pad(gla_w_in[:, :, n_main:], ((0, 0), (0, 0), (0, pad))).astype(BF16),
        "gla_w_gk": jnp.pad(gla_w_gk, ((0, 0), (0, pad), (0, 0))).astype(BF16),
        "gla_b_gk": gla_b_gk, "gla_g_norm": gla_g_norm,
        "gla_w_out": gla_w_out.astype(BF16),
        "w_kv": w_kv.astype(BF16), "att_w_q": att_w_q.astype(BF16),
        "att_rel_bias": att_rel_bias, "att_w_out": att_w_out.astype(BF16),
    }
    seq = x_prompt.shape[1]
    keep = min(LEFT_CHUNKS * CHUNK, seq)
    y_p, s_p, k_p, v_p = _trunk(x_prompt, None, None, None, w, keep)
    y_s, s_s, k_s, v_s = _trunk(x_sample, state_gla, cache_k, cache_v, w, x_sample.shape[1])
    return (y_p, y_s, s_p, k_p, v_p, s_s, k_s, v_s)
```

```python
import functools

import jax
import jax.numpy as jnp
from jax import lax
from jax.experimental import pallas as pl
from jax.experimental.pallas import tpu as pltpu

F32 = jnp.float32
BF16 = jnp.bfloat16

CHUNK = 64
GLA_HEADS = 4
GLA_SUB = 16
GLA_SUB_SHIFT = GLA_SUB.bit_length() - 1
GLA_GATE_NORM = 16.0
GLA_SINGLE_REF_LOG2_RANGE = 64.0
ATT_HEADS = 16
LEFT_CHUNKS = 8
PAST_LEN = 2048
MAX_REL = 128
EPS = 1e-6
NEG_INF = -1e30
LOG2E = 1.4426950408889634

LANES = 128
VMEM_LIMIT_BYTES = 60 * 2**20
ROW_TILE = 1024
GLA_IN_STEPS = 4
COL_TILE = 2048
COL_TILE_MULTI = 1024
RES_ROW_TILE = 512
FF_TILE = 1024
NORM_ROWS = 256
ATT_QBLOCK = 4 * CHUNK
ATT_HEAD_GROUP = 2
GLA_ROWS = 8 * CHUNK
GLA_CHUNK = 4 * CHUNK
SUBLANES = 8


def _tile(n, pref):
    if n <= pref:
        return n
    t = pref
    while n % t:
        t //= 2
    return t


def _params(sem):
    return pltpu.CompilerParams(dimension_semantics=sem, vmem_limit_bytes=VMEM_LIMIT_BYTES)


def _rmsnorm_rows(x_ref, g_ref, h_ref):
    rows = x_ref.shape[0]
    rc = _tile(rows, NORM_ROWS)

    def body(r, carry):
        sl = pl.ds(pl.multiple_of(r * rc, rc), rc)
        x = x_ref[sl, :]
        ms = jnp.mean(x * x, axis=-1, keepdims=True)
        h_ref[sl, :] = (x * lax.rsqrt(ms + EPS) * g_ref[...]).astype(h_ref.dtype)
        return carry

    lax.fori_loop(0, rows // rc, body, 0)


def _norm_matmul_kernel(*refs, n_w, out_scale):
    x_ref, g_ref = refs[:2]
    w_refs = refs[2:2 + n_w]
    out_refs = refs[2 + n_w:2 + 2 * n_w]
    h_ref = refs[2 + 2 * n_w]

    @pl.when(pl.program_id(1) == 0)
    def _():
        _rmsnorm_rows(x_ref, g_ref, h_ref)

    for w_ref, o_ref in zip(w_refs, out_refs):
        y = jnp.dot(h_ref[...], w_ref[...], preferred_element_type=F32)
        if out_scale is not None:
            y = y * out_scale
        o_ref[...] = y.astype(o_ref.dtype)


def _norm_matmul(x, g, ws, out_dtype, out_scale=None, row_tiles=None, col_windows=None, name="norm_matmul"):
    m, d = x.shape
    n, col0 = (ws[0].shape[1], [0] * len(ws)) if col_windows is None else col_windows
    if row_tiles is None:
        tm = _tile(m, ROW_TILE)
        n_tiles, block_of_tile = m // tm, lambda i: i
    else:
        tm, n_tiles, block_of_tile = row_tiles
    tn = _tile(n, COL_TILE if len(ws) == 1 else COL_TILE_MULTI)
    assert all(c0 % tn == 0 for c0 in col0)
    in_specs = [
        pl.BlockSpec((tm, d), lambda i, j: (block_of_tile(i), 0)),
        pl.BlockSpec((1, d), lambda i, j: (0, 0)),
    ] + [pl.BlockSpec((d, tn), functools.partial(lambda i, j, first: (0, first // tn + j), first=c0)) for c0 in col0]
    return pl.pallas_call(
        functools.partial(_norm_matmul_kernel, n_w=len(ws), out_scale=out_scale),
        grid=(n_tiles, n // tn),
        in_specs=in_specs,
        out_specs=[pl.BlockSpec((tm, tn), lambda i, j: (i, j)) for _ in ws],
        out_shape=[jax.ShapeDtypeStruct((n_tiles * tm, n), out_dtype) for _ in ws],
        scratch_shapes=[pltpu.VMEM((tm, d), BF16)],
        compiler_params=_params(("parallel", "arbitrary")),
        name=name,
    )(x, g.reshape(1, d), *ws)


def _gla_in_kernel(x_ref, g_ref, w_ref, wlr_ref, wgk_ref, bgk_ref, proj_ref, gates_ref, gmin_ref, h_ref, lr_ref):
    @pl.when(pl.program_id(1) == 0)
    def _():
        _rmsnorm_rows(x_ref, g_ref, h_ref)
        lr_ref[...] = jnp.dot(h_ref[...], wlr_ref[...], preferred_element_type=F32).astype(lr_ref.dtype)

    proj_ref[...] = jnp.dot(h_ref[...], w_ref[...], preferred_element_type=F32).astype(proj_ref.dtype)
    z = jnp.dot(lr_ref[...], wgk_ref[...], preferred_element_type=F32) + bgk_ref[...]
    gates = (jnp.minimum(z, 0.0) - jnp.log(1.0 + jnp.exp(-jnp.abs(z)))) * (LOG2E / GLA_GATE_NORM)
    gates_ref[...] = gates
    rows, cols = gates.shape
    totals = jnp.sum(gates.reshape(rows // GLA_CHUNK, GLA_CHUNK, cols), axis=1)
    gmin_ref[...] = jnp.min(totals, axis=0, keepdims=True)


def _gla_in(x, g, w_in, w_lr, wgk, bgk, n_main):
    m, d = x.shape
    dk = wgk.shape[1]
    tm = _tile(m, ROW_TILE)
    steps = GLA_IN_STEPS
    tn, tg = n_main // steps, dk // steps
    assert tn * steps == n_main and tg * steps == dk and tn % LANES == 0 and tg % LANES == 0
    assert tm % GLA_CHUNK == 0
    return pl.pallas_call(
        _gla_in_kernel,
        grid=(m // tm, steps),
        in_specs=[
            pl.BlockSpec((tm, d), lambda i, j: (i, 0)),
            pl.BlockSpec((1, d), lambda i, j: (0, 0)),
            pl.BlockSpec((d, tn), lambda i, j: (0, j)),
            pl.BlockSpec((d, w_lr.shape[1]), lambda i, j: (0, 0)),
            pl.BlockSpec((wgk.shape[0], tg), lambda i, j: (0, j)),
            pl.BlockSpec((1, tg), lambda i, j: (0, j)),
        ],
        out_specs=[
            pl.BlockSpec((tm, tn), lambda i, j: (i, j)),
            pl.BlockSpec((tm, tg), lambda i, j: (i, j)),
            pl.BlockSpec((None, 1, tg), lambda i, j: (i, 0, j)),
        ],
        out_shape=[
            jax.ShapeDtypeStruct((m, n_main), BF16),
            jax.ShapeDtypeStruct((m, dk), F32),
            jax.ShapeDtypeStruct((m // tm, 1, dk), F32),
        ],
        scratch_shapes=[pltpu.VMEM((tm, d), BF16), pltpu.VMEM((tm, w_lr.shape[1]), BF16)],
        compiler_params=_params(("parallel", "arbitrary")),
        name="gla_in",
    )(x, g.reshape(1, d), w_in, w_lr, wgk, bgk.reshape(1, dk))


def _matmul_res_kernel(a_ref, w_ref, x_ref, o_ref):
    o_ref[...] = x_ref[...] + jnp.dot(a_ref[...], w_ref[...], preferred_element_type=F32)


def _matmul_res(a, w, x, name="matmul_res"):
    m, k = a.shape
    n = w.shape[1]
    tm = _tile(m, RES_ROW_TILE)
    return pl.pallas_call(
        _matmul_res_kernel,
        grid=(m // tm,),
        in_specs=[
            pl.BlockSpec((tm, k), lambda i: (i, 0)),
            pl.BlockSpec((k, n), lambda i: (0, 0)),
            pl.BlockSpec((tm, n), lambda i: (i, 0)),
        ],
        out_specs=pl.BlockSpec((tm, n), lambda i: (i, 0)),
        out_shape=jax.ShapeDtypeStruct((m, n), F32),
        compiler_params=_params(("parallel",)),
        name=name,
    )(a, w, x)


def _mlp_kernel(*refs, final_norm):
    if final_norm:
        x_ref, g_ref, w1_ref, w2_ref, gf_ref, o_ref, h_ref = refs
    else:
        x_ref, g_ref, w1_ref, w2_ref, o_ref, h_ref = refs
        gf_ref = None
    f = pl.program_id(1)

    def hidden_tile():
        a = jnp.dot(h_ref[...], w1_ref[...], preferred_element_type=F32)
        a = jnp.maximum(a, 0.0)
        return jnp.dot((a * a).astype(BF16), w2_ref[...], preferred_element_type=F32)

    @pl.when(f == 0)
    def _():
        _rmsnorm_rows(x_ref, g_ref, h_ref)
        o_ref[...] = x_ref[...] + hidden_tile()

    @pl.when(f > 0)
    def _():
        o_ref[...] += hidden_tile()

    if final_norm:
        @pl.when(f == pl.num_programs(1) - 1)
        def _():
            rows = o_ref.shape[0]
            rc = _tile(rows, NORM_ROWS)

            def body(r, carry):
                sl = pl.ds(pl.multiple_of(r * rc, rc), rc)
                y = o_ref[sl, :]
                ms = jnp.mean(y * y, axis=-1, keepdims=True)
                o_ref[sl, :] = y * lax.rsqrt(ms + EPS) * gf_ref[...]
                return carry

            lax.fori_loop(0, rows // rc, body, 0)


def _mlp(x, g, w1, w2, g_final=None, name="mlp"):
    m, d = x.shape
    ff = w1.shape[1]
    tm = _tile(m, ROW_TILE)
    tf = _tile(ff, FF_TILE)
    final_norm = g_final is not None
    in_specs = [
        pl.BlockSpec((tm, d), lambda i, f: (i, 0)),
        pl.BlockSpec((1, d), lambda i, f: (0, 0)),
        pl.BlockSpec((d, tf), lambda i, f: (0, f)),
        pl.BlockSpec((tf, d), lambda i, f: (f, 0)),
    ]
    args = [x, g.reshape(1, d), w1, w2]
    if final_norm:
        in_specs.append(pl.BlockSpec((1, d), lambda i, f: (0, 0)))
        args.append(g_final.reshape(1, d))
    return pl.pallas_call(
        functools.partial(_mlp_kernel, final_norm=final_norm),
        grid=(m // tm, ff // tf),
        in_specs=in_specs,
        out_specs=pl.BlockSpec((tm, d), lambda i, f: (i, 0)),
        out_shape=jax.ShapeDtypeStruct((m, d), F32),
        scratch_shapes=[pltpu.VMEM((tm, d), BF16)],
        compiler_params=_params(("parallel", "arbitrary")),
        name=name,
    )(*args)


def _chunk_cumsums(g, n_chunks, c):
    row = lax.broadcasted_iota(jnp.int32, (c, c), 0)
    col = lax.broadcasted_iota(jnp.int32, (c, c), 1)
    tri = (col <= row).astype(BF16)
    g_hi = g.astype(BF16)
    g_r = g - g_hi.astype(F32)
    g_mid = g_r.astype(BF16)
    g_lo = (g_r - g_mid.astype(F32)).astype(BF16)
    out = []
    for ci in range(n_chunks):
        r = slice(ci * c, (ci + 1) * c)
        out.append(jnp.dot(tri, g_hi[r], preferred_element_type=F32)
                   + jnp.dot(tri, g_mid[r], preferred_element_type=F32)
                   + jnp.dot(tri, g_lo[r], preferred_element_type=F32))
    return out


def _gla_offdiag(q, k, b):
    c = q.shape[0]
    a_rows = [jnp.zeros((GLA_SUB, c), F32)]
    for l in range(1, c // GLA_SUB):
        lo = l * GLA_SUB
        ref = b[lo - 1:lo, :]
        q_ref = q[lo:lo + GLA_SUB, :] * jnp.exp2(b[lo:lo + GLA_SUB, :] - ref)
        k_ref = k * jnp.exp2(jnp.minimum(ref - b, 0.0))
        a_rows.append(lax.dot_general(q_ref.astype(BF16), k_ref.astype(BF16),
                                      (((1,), (1,)), ((), ())), preferred_element_type=F32))
    return jnp.concatenate(a_rows, axis=0)


def _gla_diag(q, k, b):
    c, hk = q.shape
    nsub = c // GLA_SUB
    q3 = q.reshape(nsub, GLA_SUB, hk)
    k3 = k.reshape(nsub, GLA_SUB, hk)
    b3 = b.reshape(nsub, GLA_SUB, hk)
    col3 = lax.broadcasted_iota(jnp.int32, (nsub, SUBLANES, c), 2)
    blk3 = lax.broadcasted_iota(jnp.int32, (nsub, SUBLANES, c), 0) * GLA_SUB
    top = jnp.zeros((nsub, SUBLANES, c), F32)
    bot = jnp.zeros((nsub, GLA_SUB - SUBLANES, c), F32)
    for e in range(GLA_SUB):
        lo = 0 if e < SUBLANES else SUBLANES
        k_e = jnp.broadcast_to(k3[:, e:e + 1, :], (nsub, GLA_SUB - lo, hk))
        b_e = jnp.broadcast_to(b3[:, e:e + 1, :], (nsub, GLA_SUB - lo, hk))
        t = q3[:, lo:, :] * k_e * jnp.exp2(jnp.minimum(b3[:, lo:, :] - b_e, 0.0))
        ts = jnp.sum(t, axis=-1, keepdims=True)
        hit = col3 == blk3 + e
        if lo == 0:
            top = jnp.where(hit, ts[:, :SUBLANES, :], top)
            bot = jnp.where(hit, ts[:, SUBLANES:, :], bot)
        else:
            bot = jnp.where(hit, ts, bot)
    return jnp.concatenate([top, bot], axis=1).reshape(c, c)


def _gla_kernel(*refs, has_state0, n_prev, n_chunks, chunk, rows_per_flag, stacked):
    mild_ref, q_ref, k_ref, v_ref, gate_ref, g_ref, gn_ref = refs[:7]
    s0_ref = refs[7] if has_state0 else None
    first_prev = 8 if has_state0 else 7
    prev_refs = refs[first_prev:first_prev + n_prev]
    o_ref, sfin_ref, s_scr = refs[-3:]
    t = pl.program_id(2)
    hk = q_ref.shape[-1]
    scale = hk ** -0.5
    c = chunk

    @pl.when(t == 0)
    def _():
        if has_state0:
            s_scr[...] = s0_ref[...]
        else:
            s_scr[...] = jnp.zeros_like(s_scr)

    bs = _chunk_cumsums(g_ref[...], n_chunks, c)
    row = lax.broadcasted_iota(jnp.int32, (c, c), 0)
    col = lax.broadcasted_iota(jnp.int32, (c, c), 1)
    row_blk = jnp.right_shift(row, GLA_SUB_SHIFT)
    col_blk = jnp.right_shift(col, GLA_SUB_SHIFT)
    below = col_blk < row_blk
    on_diag = (col_blk == row_blk) & (col <= row)

    def prepare(ci, single_ref):
        r = slice(ci * c, (ci + 1) * c)
        q = q_ref[r, :].astype(F32) * scale
        k = k_ref[r, :].astype(F32)
        b = bs[ci]
        b_end = b[c - 1:c, :]
        decay_col = jnp.transpose(jnp.broadcast_to(jnp.exp2(b_end), (LANES, hk)))[:, :1]
        k_dec = (k * jnp.exp2(b_end - b)).astype(BF16)
        if single_ref:
            q_up = (q * jnp.exp2(b - b_end)).astype(BF16)
            scores = lax.dot_general(q_up, k_dec, (((1,), (1,)), ((), ())), preferred_element_type=F32)
        else:
            scores = _gla_offdiag(q, k, b)
        return dict(q=q, k=k, b=b, q_dec=(q * jnp.exp2(b)).astype(BF16), k_dec=k_dec,
                    decay_col=decay_col, scores=scores)

    def finish(ci, o):
        r = slice(ci * c, (ci + 1) * c)
        ms = jnp.mean(o * o, axis=-1, keepdims=True)
        y = o * lax.rsqrt(ms + EPS) * gn_ref[...]
        gate = gate_ref[r, :].astype(F32)
        o_ref[r, :] = (y * (gate * (1.0 / (1.0 + jnp.exp(-gate))))).astype(o_ref.dtype)

    def run(single_ref):
        s = s_scr[...]
        cur = prepare(0, single_ref)
        prev_out = None
        prev_upd = None
        for ci in range(n_chunks):
            if prev_upd is not None:
                s = prev_upd[0] * s + prev_upd[1]
            o_inter = jnp.dot(cur["q_dec"], s.astype(BF16), preferred_element_type=F32)
            if prev_out is not None:
                finish(ci - 1, prev_out[0] + prev_out[1])
            if single_ref:
                a = jnp.where(col <= row, cur["scores"], 0.0).astype(BF16)
            else:
                a_diag = _gla_diag(cur["q"], cur["k"], cur["b"])
                a = jnp.where(below, cur["scores"], jnp.where(on_diag, a_diag, 0.0)).astype(BF16)
            v16 = v_ref[ci * c:(ci + 1) * c, :]
            prev_out = (jnp.dot(a, v16, preferred_element_type=F32), o_inter)
            prev_upd = (cur["decay_col"],
                        lax.dot_general(cur["k_dec"], v16, (((0,), (0,)), ((), ())), preferred_element_type=F32))
            if ci + 1 < n_chunks:
                cur = prepare(ci + 1, single_ref)
        s_scr[...] = prev_upd[0] * s + prev_upd[1]
        finish(n_chunks - 1, prev_out[0] + prev_out[1])

    tile = (pl.program_id(0) * (pl.num_programs(2) * n_chunks * c) + t * (n_chunks * c)) // rows_per_flag
    mild = mild_ref[tile, pl.program_id(1)] != 0

    @pl.when(mild)
    def _():
        run(True)

    @pl.when(jnp.logical_not(mild))
    def _():
        run(False)

    @pl.when(t == pl.num_programs(2) - 1)
    def _():
        if stacked:
            for i, p_ref in enumerate(prev_refs):
                sfin_ref[i] = p_ref[...]
            sfin_ref[n_prev] = s_scr[...]
        else:
            sfin_ref[...] = s_scr[...]


def _gla(proj, gates, mild, rows_per_flag, g_norm, state0, layer, n_layers, prev_states, batch, seq, name="gla"):
    h = GLA_HEADS
    stacked = layer == n_layers - 1
    assert len(prev_states) == (layer if stacked else 0)
    dv = proj.shape[1] // 3
    dk = dv // 2
    hk, hv = dk // h, dv // h
    tb = _tile(seq, GLA_ROWS)
    assert rows_per_flag % tb == 0
    proj3 = proj.reshape(batch, seq, proj.shape[1])
    gates3 = gates.reshape(batch, seq, dk)
    chunk = _tile(tb, GLA_CHUNK)
    in_specs = [
        pl.BlockSpec((None, tb, hk), lambda b, hh, t, *_: (b, t, hh)),
        pl.BlockSpec((None, tb, hk), lambda b, hh, t, *_: (b, t, h + hh)),
        pl.BlockSpec((None, tb, hv), lambda b, hh, t, *_: (b, t, 2 * dk // hv + hh)),
        pl.BlockSpec((None, tb, hv), lambda b, hh, t, *_: (b, t, (2 * dk + dv) // hv + hh)),
        pl.BlockSpec((None, tb, hk), lambda b, hh, t, *_: (b, t, hh)),
        pl.BlockSpec((1, hv), lambda b, hh, t, *_: (0, 0)),
    ]
    args = [proj3, proj3, proj3, proj3, gates3, g_norm.reshape(1, hv)]
    has_state0 = state0 is not None
    if has_state0:
        in_specs.append(pl.BlockSpec((None, None, None, hk, hv), lambda b, hh, t, *_: (layer, b, hh, 0, 0)))
        args.append(state0)
    one_state = pl.BlockSpec((None, None, hk, hv), lambda b, hh, t, *_: (b, hh, 0, 0))
    in_specs += [one_state] * len(prev_states)
    args += prev_states
    if stacked:
        state_spec = pl.BlockSpec((n_layers, None, None, hk, hv), lambda b, hh, t, *_: (0, b, hh, 0, 0))
        state_shape = jax.ShapeDtypeStruct((n_layers, batch, h, hk, hv), F32)
    else:
        state_spec = one_state
        state_shape = jax.ShapeDtypeStruct((batch, h, hk, hv), F32)
    og, states = pl.pallas_call(
        functools.partial(_gla_kernel, has_state0=has_state0, n_prev=len(prev_states), n_chunks=tb // chunk,
                          chunk=chunk, rows_per_flag=rows_per_flag, stacked=stacked),
        grid_spec=pltpu.PrefetchScalarGridSpec(
            num_scalar_prefetch=1,
            grid=(batch, h, seq // tb),
            in_specs=in_specs,
            out_specs=[pl.BlockSpec((None, tb, hv), lambda b, hh, t, *_: (b, t, hh)), state_spec],
            scratch_shapes=[pltpu.VMEM((hk, hv), F32)],
        ),
        out_shape=[jax.ShapeDtypeStruct((batch, seq, dv), BF16), state_shape],
        compiler_params=_params(("parallel", "parallel", "arbitrary")),
        name=name,
    )(mild, *args)
    return og.reshape(batch * seq, dv), states


def _attn_kernel(*refs, qb, past, hd, hg):
    if past:
        q_ref, k_ref, v_ref, kc_ref, vc_ref, bm_ref, o_ref = refs
    else:
        q_ref, k_ref, v_ref, bm_ref, o_ref = refs
        kc_ref = vc_ref = None
    tq = q_ref.shape[0]
    left = LEFT_CHUNKS * CHUNK
    wfull = bm_ref.shape[-1]
    blocks = [(hh, i) for hh in range(hg) for i in range(tq // qb)]

    def window(i):
        return max(0, i * qb + past - left), i * qb + past + qb

    def rows(new_ref, cache_ref, k0, k1, hh):
        cs = slice(hh * hd, (hh + 1) * hd)
        parts = []
        if k0 < past:
            parts.append(cache_ref[k0:min(k1, past), cs])
        if k1 > past:
            parts.append(new_ref[max(k0, past) - past:k1 - past, cs])
        return parts[0] if len(parts) == 1 else jnp.concatenate(parts, axis=0)

    def scores(hh, i):
        k0, k1 = window(i)
        s = lax.dot_general(q_ref[i * qb:(i + 1) * qb, hh * hd:(hh + 1) * hd], rows(k_ref, kc_ref, k0, k1, hh),
                            (((1,), (1,)), ((), ())), preferred_element_type=F32)
        return s + bm_ref[hh, :, wfull - (k1 - k0):]

    def store(hh, i, o, denom):
        o_ref[i * qb:(i + 1) * qb, hh * hd:(hh + 1) * hd] = (o * (1.0 / denom)).astype(o_ref.dtype)

    s_next = scores(*blocks[0])
    pending = None
    for n, (hh, i) in enumerate(blocks):
        s = s_next
        if n + 1 < len(blocks):
            s_next = scores(*blocks[n + 1])
        e = jnp.exp2(s - jnp.max(s, axis=-1, keepdims=True))
        k0, k1 = window(i)
        o = jnp.dot(e.astype(BF16), rows(v_ref, vc_ref, k0, k1, hh), preferred_element_type=F32)
        if pending is not None:
            store(*pending)
        pending = (hh, i, o, jnp.sum(e, axis=-1, keepdims=True))
    store(*pending)


def _attention(q, kv, bm, cache_k=None, cache_v=None, name="attn"):
    batch, tq, d = q.shape
    hd = d // ATT_HEADS
    hg = ATT_HEAD_GROUP if tq > bm.shape[1] else ATT_HEADS
    gw = hg * hd
    qb = bm.shape[1]
    past = 0 if cache_k is None else cache_k.shape[1]
    assert PAST_LEN % CHUNK == 0 and past in (0, min(LEFT_CHUNKS * CHUNK, PAST_LEN))
    in_specs = [
        pl.BlockSpec((None, tq, gw), lambda b, g: (b, 0, g)),
        pl.BlockSpec((None, tq, gw), lambda b, g: (b, 0, g)),
        pl.BlockSpec((None, tq, gw), lambda b, g: (b, 0, d // gw + g)),
    ]
    args = [q, kv, kv]
    if past:
        in_specs += [pl.BlockSpec((None, past, gw), lambda b, g: (b, 0, g))] * 2
        args += [cache_k, cache_v]
    in_specs.append(pl.BlockSpec((hg, qb, bm.shape[2]), lambda b, g: (g, 0, 0)))
    args.append(bm)
    return pl.pallas_call(
        functools.partial(_attn_kernel, qb=qb, past=past, hd=hd, hg=hg),
        grid=(batch, ATT_HEADS // hg),
        in_specs=in_specs,
        out_specs=pl.BlockSpec((None, tq, gw), lambda b, g: (b, 0, g)),
        out_shape=jax.ShapeDtypeStruct((batch, tq, d), BF16),
        compiler_params=_params(("parallel", "parallel")),
        name=name,
    )(*args)


def _bias_mask(table, qb):
    left = LEFT_CHUNKS * CHUNK
    h = table.shape[0]
    t = table.astype(F32) * LOG2E
    sat = left + 2 * qb
    ext = jnp.concatenate([jnp.broadcast_to(t[:, :1], (h, sat)), t, jnp.broadcast_to(t[:, -1:], (h, sat))], axis=1)
    top = left + qb - 1
    n_w = left + 2 * qb - 1
    start = ext.shape[1] - 1 - (top + MAX_REL + sat)
    desc = ext[:, ::-1][:, start:start + n_w]
    w = jnp.concatenate([desc[:, qb - 1:], desc[:, :qb - 1]], axis=1)
    bias = jnp.tile(w, (1, qb))[:, :qb * (n_w - 1)].reshape(h, qb, n_w - 1)[:, :, :left + qb]
    r = jnp.arange(qb)[:, None]
    c = jnp.arange(left + qb)[None, :]
    qc = r // CHUNK
    kc = c // CHUNK - LEFT_CHUNKS
    allowed = (kc <= qc) & (kc >= qc - LEFT_CHUNKS)
    return jnp.where(allowed[None], bias, NEG_INF)


def _trunk(x, state0, cache_k, cache_v, w, keep):
    batch, seq, d = x.shape
    depth = w["norm_mix"].shape[0]
    n_a = depth // 2
    x = x.reshape(batch * seq, d)
    states = []
    k_rows = None
    v_rows = None
    kv16 = None
    qb = _tile(seq, ATT_QBLOCK)
    if cache_k is not None:
        cache16 = (cache_k.astype(BF16).reshape(batch, -1, d), cache_v.astype(BF16).reshape(batch, -1, d))
    for layer in range(depth):
        if layer < n_a:
            proj, gates, gmin = _gla_in(x, w["norm_mix"][layer], w["gla_w_in"][layer], w["gla_w_lr"][layer],
                                        w["gla_w_gk"][layer], w["gla_b_gk"][layer], 3 * d)
            mild = (jnp.min(gmin.reshape(gmin.shape[0], GLA_HEADS, -1), axis=-1)
                    >= -GLA_SINGLE_REF_LOG2_RANGE).astype(jnp.int32)
            og, s_end = _gla(proj, gates, mild, x.shape[0] // gmin.shape[0], w["gla_g_norm"][layer],
                             state0, layer, n_a, states if layer == n_a - 1 else [], batch, seq)
            states = s_end if layer == n_a - 1 else states + [s_end]
            x = _matmul_res(og, w["gla_w_out"][layer], x, name="gla_out")
        else:
            j = layer - n_a
            if layer == n_a:
                (kv16,) = _norm_matmul(x, w["norm_kv"], [w["w_kv"]], BF16, name="kv_proj")
                if keep == seq:
                    row_tiles = None
                else:
                    assert seq % keep == 0
                    per_seq = seq // keep
                    row_tiles = (keep, batch, lambda i: i * per_seq + per_seq - 1)
                k_rows, v_rows = _norm_matmul(x, w["norm_kv"], [w["w_kv"], w["w_kv"]], F32,
                                              row_tiles=row_tiles, col_windows=(d, [0, d]), name="kv_rows")
            (q16,) = _norm_matmul(x, w["norm_mix"][layer], [w["att_w_q"][j]], BF16,
                                  out_scale=(d // ATT_HEADS) ** -0.5 * LOG2E, name="q_proj")
            q3 = q16.reshape(batch, seq, d)
            bm = _bias_mask(w["att_rel_bias"][j], qb)
            kv3 = kv16.reshape(batch, seq, 2 * d)
            if cache_k is None:
                o = _attention(q3, kv3, bm)
            else:
                o = _attention(q3, kv3, bm, *cache16)
            x = _matmul_res(o.reshape(batch * seq, d), w["att_w_out"][j], x, name="att_out")
        g_final = w["norm_final"] if layer == depth - 1 else None
        x = _mlp(x, w["norm_ffn"][layer], w["w_ff1"][layer], w["w_ff2"][layer], g_final)
    hd = d // ATT_HEADS
    k = k_rows.reshape(batch, keep, ATT_HEADS, hd)
    v = v_rows.reshape(batch, keep, ATT_HEADS, hd)
    return x.reshape(batch, seq, d), states, k, v


def kernel(x_prompt, x_sample, state_gla, cache_k, cache_v, norm_mix, norm_ffn, w_ff1, w_ff2, gla_w_in, gla_w_gk, gla_b_gk, gla_g_norm, gla_w_out, norm_kv, w_kv, att_w_q, att_rel_bias, att_w_out, norm_final):
    d = x_prompt.shape[-1]
    dk = d // 2
    n_main = 2 * dk + 2 * d
    rank = gla_w_in.shape[-1] - n_main
    pad = LANES - rank
    w = {
        "norm_mix": norm_mix, "norm_ffn": norm_ffn, "norm_kv": norm_kv, "norm_final": norm_final,
        "w_ff1": w_ff1.astype(BF16), "w_ff2": w_ff2.astype(BF16),
        "gla_w_in": gla_w_in.astype(BF16),
        "gla_w_lr": jnp.pad(gla_w_in[:, :, n_main:], ((0, 0), (0, 0), (0, pad))).astype(BF16),
        "gla_w_gk": jnp.pad(gla_w_gk, ((0, 0), (0, pad), (0, 0))).astype(BF16),
        "gla_b_gk": gla_b_gk, "gla_g_norm": gla_g_norm,
        "gla_w_out": gla_w_out.astype(BF16),
        "w_kv": w_kv.astype(BF16), "att_w_q": att_w_q.astype(BF16),
        "att_rel_bias": att_rel_bias, "att_w_out": att_w_out.astype(BF16),
    }
    seq = x_prompt.shape[1]
    keep = min(LEFT_CHUNKS * CHUNK, seq)
    y_p, s_p, k_p, v_p = _trunk(x_prompt, None, None, None, w, keep)
    y_s, s_s, k_s, v_s = _trunk(x_sample, state_gla, cache_k, cache_v, w, x_sample.shape[1])
    return (y_p, y_s, s_p, k_p, v_p, s_s, k_s, v_s)
```

```python
import functools

import jax
import jax.numpy as jnp
from jax import lax
from jax.experimental import pallas as pl
from jax.experimental.pallas import tpu as pltpu

F32 = jnp.float32
BF16 = jnp.bfloat16

CHUNK = 64
GLA_HEADS = 4
GLA_SUB = 16
GLA_SUB_SHIFT = GLA_SUB.bit_length() - 1
GLA_GATE_NORM = 16.0
GLA_SINGLE_REF_LOG2_RANGE = 64.0
ATT_HEADS = 16
LEFT_CHUNKS = 8
PAST_LEN = 2048
MAX_REL = 128
EPS = 1e-6
NEG_INF = -1e30
LOG2E = 1.4426950408889634

LANES = 128
VMEM_LIMIT_BYTES = 60 * 2**20
ROW_TILE = 1024
GLA_IN_STEPS = 4
COL_TILE = 2048
COL_TILE_MULTI = 1024
RES_ROW_TILE = 512
FF_TILE = 1024
NORM_ROWS = 256
ATT_QBLOCK = 4 * CHUNK
ATT_HEAD_GROUP = 2
GLA_ROWS = 8 * CHUNK
GLA_CHUNK = 4 * CHUNK
SUBLANES = 8


def _tile(n, pref):
    if n <= pref:
        return n
    t = pref
    while n % t:
        t //= 2
    return t


def _params(sem):
    return pltpu.CompilerParams(dimension_semantics=sem, vmem_limit_bytes=VMEM_LIMIT_BYTES)


def _rmsnorm(x, g):
    ms = jnp.mean(x * x, axis=-1, keepdims=True)
    return (x * lax.rsqrt(ms + EPS) * g).astype(BF16)


def _norm_matmul_kernel(*refs, n_w, out_scale):
    x_ref, g_ref = refs[:2]
    w_refs = refs[2:2 + n_w]
    out_refs = refs[2 + n_w:2 + 2 * n_w]
    h_ref = refs[2 + 2 * n_w]

    def project(h, rows):
        for w_ref, o_ref in zip(w_refs, out_refs):
            y = jnp.dot(h, w_ref[...], preferred_element_type=F32)
            if out_scale is not None:
                y = y * out_scale
            o_ref[rows, :] = y.astype(o_ref.dtype)

    @pl.when(pl.program_id(1) == 0)
    def _():
        rs = _tile(x_ref.shape[0], NORM_ROWS)
        n_blocks = x_ref.shape[0] // rs
        h_next = _rmsnorm(x_ref[0:rs, :], g_ref[...])
        for s in range(n_blocks):
            h = h_next
            if s + 1 < n_blocks:
                h_next = _rmsnorm(x_ref[(s + 1) * rs:(s + 2) * rs, :], g_ref[...])
            h_ref[s * rs:(s + 1) * rs, :] = h
            project(h, slice(s * rs, (s + 1) * rs))

    @pl.when(pl.program_id(1) > 0)
    def _():
        project(h_ref[...], slice(None))


def _norm_matmul(x, g, ws, out_dtype, out_scale=None, row_tiles=None, col_windows=None, name="norm_matmul"):
    m, d = x.shape
    n, col0 = (ws[0].shape[1], [0] * len(ws)) if col_windows is None else col_windows
    if row_tiles is None:
        tm = _tile(m, ROW_TILE)
        n_tiles, block_of_tile = m // tm, lambda i: i
    else:
        tm, n_tiles, block_of_tile = row_tiles
    tn = _tile(n, COL_TILE if len(ws) == 1 else COL_TILE_MULTI)
    assert all(c0 % tn == 0 for c0 in col0)
    in_specs = [
        pl.BlockSpec((tm, d), lambda i, j: (block_of_tile(i), 0)),
        pl.BlockSpec((1, d), lambda i, j: (0, 0)),
    ] + [pl.BlockSpec((d, tn), functools.partial(lambda i, j, first: (0, first // tn + j), first=c0)) for c0 in col0]
    return pl.pallas_call(
        functools.partial(_norm_matmul_kernel, n_w=len(ws), out_scale=out_scale),
        grid=(n_tiles, n // tn),
        in_specs=in_specs,
        out_specs=[pl.BlockSpec((tm, tn), lambda i, j: (i, j)) for _ in ws],
        out_shape=[jax.ShapeDtypeStruct((n_tiles * tm, n), out_dtype) for _ in ws],
        scratch_shapes=[pltpu.VMEM((tm, d), BF16)],
        compiler_params=_params(("parallel", "arbitrary")),
        name=name,
    )(x, g.reshape(1, d), *ws)


def _gla_in_kernel(x_ref, g_ref, w_ref, wlr_ref, wgk_ref, bgk_ref, proj_ref, gates_ref, gmin_ref, h_ref, lr_ref):
    def project(h, lr, rows):
        proj_ref[rows, :] = jnp.dot(h, w_ref[...], preferred_element_type=F32).astype(proj_ref.dtype)
        z = jnp.dot(lr, wgk_ref[...], preferred_element_type=F32) + bgk_ref[...]
        gates_ref[rows, :] = (jnp.minimum(z, 0.0) - jnp.log(1.0 + jnp.exp(-jnp.abs(z)))) * (LOG2E / GLA_GATE_NORM)

    @pl.when(pl.program_id(1) == 0)
    def _():
        rs = _tile(x_ref.shape[0], NORM_ROWS)
        n_blocks = x_ref.shape[0] // rs
        h_next = _rmsnorm(x_ref[0:rs, :], g_ref[...])
        for s in range(n_blocks):
            rows = slice(s * rs, (s + 1) * rs)
            h = h_next
            if s + 1 < n_blocks:
                h_next = _rmsnorm(x_ref[(s + 1) * rs:(s + 2) * rs, :], g_ref[...])
            lr = jnp.dot(h, wlr_ref[...], preferred_element_type=F32).astype(lr_ref.dtype)
            h_ref[rows, :] = h
            lr_ref[rows, :] = lr
            project(h, lr, rows)

    @pl.when(pl.program_id(1) > 0)
    def _():
        project(h_ref[...], lr_ref[...], slice(None))

    gates = gates_ref[...]
    rows, cols = gates.shape
    totals = jnp.sum(gates.reshape(rows // GLA_CHUNK, GLA_CHUNK, cols), axis=1)
    gmin_ref[...] = jnp.min(totals, axis=0, keepdims=True)


def _gla_in(x, g, w_in, w_lr, wgk, bgk, n_main):
    m, d = x.shape
    dk = wgk.shape[1]
    tm = _tile(m, ROW_TILE)
    steps = GLA_IN_STEPS
    tn, tg = n_main // steps, dk // steps
    assert tn * steps == n_main and tg * steps == dk and tn % LANES == 0 and tg % LANES == 0
    assert tm % GLA_CHUNK == 0
    return pl.pallas_call(
        _gla_in_kernel,
        grid=(m // tm, steps),
        in_specs=[
            pl.BlockSpec((tm, d), lambda i, j: (i, 0)),
            pl.BlockSpec((1, d), lambda i, j: (0, 0)),
            pl.BlockSpec((d, tn), lambda i, j: (0, j)),
            pl.BlockSpec((d, w_lr.shape[1]), lambda i, j: (0, 0)),
            pl.BlockSpec((wgk.shape[0], tg), lambda i, j: (0, j)),
            pl.BlockSpec((1, tg), lambda i, j: (0, j)),
        ],
        out_specs=[
            pl.BlockSpec((tm, tn), lambda i, j: (i, j)),
            pl.BlockSpec((tm, tg), lambda i, j: (i, j)),
            pl.BlockSpec((None, 1, tg), lambda i, j: (i, 0, j)),
        ],
        out_shape=[
            jax.ShapeDtypeStruct((m, n_main), BF16),
            jax.ShapeDtypeStruct((m, dk), F32),
            jax.ShapeDtypeStruct((m // tm, 1, dk), F32),
        ],
        scratch_shapes=[pltpu.VMEM((tm, d), BF16), pltpu.VMEM((tm, w_lr.shape[1]), BF16)],
        compiler_params=_params(("parallel", "arbitrary")),
        name="gla_in",
    )(x, g.reshape(1, d), w_in, w_lr, wgk, bgk.reshape(1, dk))


def _matmul_res_kernel(a_ref, w_ref, x_ref, o_ref):
    o_ref[...] = x_ref[...] + jnp.dot(a_ref[...], w_ref[...], preferred_element_type=F32)


def _matmul_res(a, w, x, name="matmul_res"):
    m, k = a.shape
    n = w.shape[1]
    tm = _tile(m, RES_ROW_TILE)
    return pl.pallas_call(
        _matmul_res_kernel,
        grid=(m // tm,),
        in_specs=[
            pl.BlockSpec((tm, k), lambda i: (i, 0)),
            pl.BlockSpec((k, n), lambda i: (0, 0)),
            pl.BlockSpec((tm, n), lambda i: (i, 0)),
        ],
        out_specs=pl.BlockSpec((tm, n), lambda i: (i, 0)),
        out_shape=jax.ShapeDtypeStruct((m, n), F32),
        compiler_params=_params(("parallel",)),
        name=name,
    )(a, w, x)


def _mlp_kernel(*refs, final_norm):
    if final_norm:
        x_ref, g_ref, w1_ref, w2_ref, gf_ref, o_ref, h_ref = refs
    else:
        x_ref, g_ref, w1_ref, w2_ref, o_ref, h_ref = refs
        gf_ref = None
    f = pl.program_id(1)

    def hidden_tile(h):
        a = jnp.dot(h, w1_ref[...], preferred_element_type=F32)
        a = jnp.maximum(a, 0.0)
        return jnp.dot((a * a).astype(BF16), w2_ref[...], preferred_element_type=F32)

    @pl.when(f == 0)
    def _():
        rs = _tile(x_ref.shape[0], NORM_ROWS)
        n_blocks = x_ref.shape[0] // rs
        h_next = _rmsnorm(x_ref[0:rs, :], g_ref[...])
        for s in range(n_blocks):
            rows = slice(s * rs, (s + 1) * rs)
            h = h_next
            if s + 1 < n_blocks:
                h_next = _rmsnorm(x_ref[(s + 1) * rs:(s + 2) * rs, :], g_ref[...])
            h_ref[rows, :] = h
            o_ref[rows, :] = x_ref[rows, :] + hidden_tile(h)

    @pl.when(f > 0)
    def _():
        o_ref[...] += hidden_tile(h_ref[...])

    if final_norm:
        @pl.when(f == pl.num_programs(1) - 1)
        def _():
            rows = o_ref.shape[0]
            rc = _tile(rows, NORM_ROWS)

            def body(r, carry):
                sl = pl.ds(pl.multiple_of(r * rc, rc), rc)
                y = o_ref[sl, :]
                ms = jnp.mean(y * y, axis=-1, keepdims=True)
                o_ref[sl, :] = y * lax.rsqrt(ms + EPS) * gf_ref[...]
                return carry

            lax.fori_loop(0, rows // rc, body, 0)


def _mlp(x, g, w1, w2, g_final=None, name="mlp"):
    m, d = x.shape
    ff = w1.shape[1]
    tm = _tile(m, ROW_TILE)
    tf = _tile(ff, FF_TILE)
    final_norm = g_final is not None
    in_specs = [
        pl.BlockSpec((tm, d), lambda i, f: (i, 0)),
        pl.BlockSpec((1, d), lambda i, f: (0, 0)),
        pl.BlockSpec((d, tf), lambda i, f: (0, f)),
        pl.BlockSpec((tf, d), lambda i, f: (f, 0)),
    ]
    args = [x, g.reshape(1, d), w1, w2]
    if final_norm:
        in_specs.append(pl.BlockSpec((1, d), lambda i, f: (0, 0)))
        args.append(g_final.reshape(1, d))
    return pl.pallas_call(
        functools.partial(_mlp_kernel, final_norm=final_norm),
        grid=(m // tm, ff // tf),
        in_specs=in_specs,
        out_specs=pl.BlockSpec((tm, d), lambda i, f: (i, 0)),
        out_shape=jax.ShapeDtypeStruct((m, d), F32),
        scratch_shapes=[pltpu.VMEM((tm, d), BF16)],
        compiler_params=_params(("parallel", "arbitrary")),
        name=name,
    )(*args)


def _chunk_cumsums(g, n_chunks, c):
    row = lax.broadcasted_iota(jnp.int32, (c, c), 0)
    col = lax.broadcasted_iota(jnp.int32, (c, c), 1)
    tri = (col <= row).astype(BF16)
    g_hi = g.astype(BF16)
    g_r = g - g_hi.astype(F32)
    g_mid = g_r.astype(BF16)
    g_lo = (g_r - g_mid.astype(F32)).astype(BF16)
    out = []
    for ci in range(n_chunks):
        r = slice(ci * c, (ci + 1) * c)
        out.append(jnp.dot(tri, g_hi[r], preferred_element_type=F32)
                   + jnp.dot(tri, g_mid[r], preferred_element_type=F32)
                   + jnp.dot(tri, g_lo[r], preferred_element_type=F32))
    return out


def _gla_offdiag(q, k, b):
    c = q.shape[0]
    a_rows = [jnp.zeros((GLA_SUB, c), F32)]
    for l in range(1, c // GLA_SUB):
        lo = l * GLA_SUB
        ref = b[lo - 1:lo, :]
        q_ref = q[lo:lo + GLA_SUB, :] * jnp.exp2(b[lo:lo + GLA_SUB, :] - ref)
        k_ref = k * jnp.exp2(jnp.minimum(ref - b, 0.0))
        a_rows.append(lax.dot_general(q_ref.astype(BF16), k_ref.astype(BF16),
                                      (((1,), (1,)), ((), ())), preferred_element_type=F32))
    return jnp.concatenate(a_rows, axis=0)


def _gla_diag(q, k, b):
    c, hk = q.shape
    nsub = c // GLA_SUB
    q3 = q.reshape(nsub, GLA_SUB, hk)
    k3 = k.reshape(nsub, GLA_SUB, hk)
    b3 = b.reshape(nsub, GLA_SUB, hk)
    col3 = lax.broadcasted_iota(jnp.int32, (nsub, SUBLANES, c), 2)
    blk3 = lax.broadcasted_iota(jnp.int32, (nsub, SUBLANES, c), 0) * GLA_SUB
    top = jnp.zeros((nsub, SUBLANES, c), F32)
    bot = jnp.zeros((nsub, GLA_SUB - SUBLANES, c), F32)
    for e in range(GLA_SUB):
        lo = 0 if e < SUBLANES else SUBLANES
        k_e = jnp.broadcast_to(k3[:, e:e + 1, :], (nsub, GLA_SUB - lo, hk))
        b_e = jnp.broadcast_to(b3[:, e:e + 1, :], (nsub, GLA_SUB - lo, hk))
        t = q3[:, lo:, :] * k_e * jnp.exp2(jnp.minimum(b3[:, lo:, :] - b_e, 0.0))
        ts = jnp.sum(t, axis=-1, keepdims=True)
        hit = col3 == blk3 + e
        if lo == 0:
            top = jnp.where(hit, ts[:, :SUBLANES, :], top)
            bot = jnp.where(hit, ts[:, SUBLANES:, :], bot)
        else:
            bot = jnp.where(hit, ts, bot)
    return jnp.concatenate([top, bot], axis=1).reshape(c, c)


def _gla_kernel(*refs, has_state0, n_prev, n_chunks, chunk, rows_per_flag, stacked):
    mild_ref, q_ref, k_ref, v_ref, gate_ref, g_ref, gn_ref = refs[:7]
    s0_ref = refs[7] if has_state0 else None
    first_prev = 8 if has_state0 else 7
    prev_refs = refs[first_prev:first_prev + n_prev]
    o_ref, sfin_ref, s_scr = refs[-3:]
    t = pl.program_id(2)
    hk = q_ref.shape[-1]
    scale = hk ** -0.5
    c = chunk

    @pl.when(t == 0)
    def _():
        if has_state0:
            s_scr[...] = s0_ref[...]
        else:
            s_scr[...] = jnp.zeros_like(s_scr)

    bs = _chunk_cumsums(g_ref[...], n_chunks, c)
    row = lax.broadcasted_iota(jnp.int32, (c, c), 0)
    col = lax.broadcasted_iota(jnp.int32, (c, c), 1)
    row_blk = jnp.right_shift(row, GLA_SUB_SHIFT)
    col_blk = jnp.right_shift(col, GLA_SUB_SHIFT)
    below = col_blk < row_blk
    on_diag = (col_blk == row_blk) & (col <= row)

    def prepare(ci, single_ref):
        r = slice(ci * c, (ci + 1) * c)
        q = q_ref[r, :].astype(F32) * scale
        k = k_ref[r, :].astype(F32)
        b = bs[ci]
        b_end = b[c - 1:c, :]
        decay_col = jnp.transpose(jnp.broadcast_to(jnp.exp2(b_end), (LANES, hk)))[:, :1]
        k_dec = (k * jnp.exp2(b_end - b)).astype(BF16)
        if single_ref:
            q_up = (q * jnp.exp2(b - b_end)).astype(BF16)
            scores = lax.dot_general(q_up, k_dec, (((1,), (1,)), ((), ())), preferred_element_type=F32)
        else:
            scores = _gla_offdiag(q, k, b)
        return dict(q=q, k=k, b=b, q_dec=(q * jnp.exp2(b)).astype(BF16), k_dec=k_dec,
                    decay_col=decay_col, scores=scores)

    def finish(ci, o):
        r = slice(ci * c, (ci + 1) * c)
        ms = jnp.mean(o * o, axis=-1, keepdims=True)
        y = o * lax.rsqrt(ms + EPS) * gn_ref[...]
        gate = gate_ref[r, :].astype(F32)
        o_ref[r, :] = (y * (gate * (1.0 / (1.0 + jnp.exp(-gate))))).astype(o_ref.dtype)

    def run(single_ref):
        s = s_scr[...]
        cur = prepare(0, single_ref)
        prev_out = None
        prev_upd = None
        for ci in range(n_chunks):
            if prev_upd is not None:
                s = prev_upd[0] * s + prev_upd[1]
            o_inter = jnp.dot(cur["q_dec"], s.astype(BF16), preferred_element_type=F32)
            if prev_out is not None:
                finish(ci - 1, prev_out[0] + prev_out[1])
            if single_ref:
                a = jnp.where(col <= row, cur["scores"], 0.0).astype(BF16)
            else:
                a_diag = _gla_diag(cur["q"], cur["k"], cur["b"])
                a = jnp.where(below, cur["scores"], jnp.where(on_diag, a_diag, 0.0)).astype(BF16)
            v16 = v_ref[ci * c:(ci + 1) * c, :]
            prev_out = (jnp.dot(a, v16, preferred_element_type=F32), o_inter)
            prev_upd = (cur["decay_col"],
                        lax.dot_general(cur["k_dec"], v16, (((0,), (0,)), ((), ())), preferred_element_type=F32))
            if ci + 1 < n_chunks:
                cur = prepare(ci + 1, single_ref)
        s_scr[...] = prev_upd[0] * s + prev_upd[1]
        finish(n_chunks - 1, prev_out[0] + prev_out[1])

    tile = (pl.program_id(0) * (pl.num_programs(2) * n_chunks * c) + t * (n_chunks * c)) // rows_per_flag
    mild = mild_ref[tile, pl.program_id(1)] != 0

    @pl.when(mild)
    def _():
        run(True)

    @pl.when(jnp.logical_not(mild))
    def _():
        run(False)

    @pl.when(t == pl.num_programs(2) - 1)
    def _():
        if stacked:
            for i, p_ref in enumerate(prev_refs):
                sfin_ref[i] = p_ref[...]
            sfin_ref[n_prev] = s_scr[...]
        else:
            sfin_ref[...] = s_scr[...]


def _gla(proj, gates, mild, rows_per_flag, g_norm, state0, layer, n_layers, prev_states, batch, seq, name="gla"):
    h = GLA_HEADS
    stacked = layer == n_layers - 1
    assert len(prev_states) == (layer if stacked else 0)
    dv = proj.shape[1] // 3
    dk = dv // 2
    hk, hv = dk // h, dv // h
    tb = _tile(seq, GLA_ROWS)
    assert rows_per_flag % tb == 0
    proj3 = proj.reshape(batch, seq, proj.shape[1])
    gates3 = gates.reshape(batch, seq, dk)
    chunk = _tile(tb, GLA_CHUNK)
    in_specs = [
        pl.BlockSpec((None, tb, hk), lambda b, hh, t, *_: (b, t, hh)),
        pl.BlockSpec((None, tb, hk), lambda b, hh, t, *_: (b, t, h + hh)),
        pl.BlockSpec((None, tb, hv), lambda b, hh, t, *_: (b, t, 2 * dk // hv + hh)),
        pl.BlockSpec((None, tb, hv), lambda b, hh, t, *_: (b, t, (2 * dk + dv) // hv + hh)),
        pl.BlockSpec((None, tb, hk), lambda b, hh, t, *_: (b, t, hh)),
        pl.BlockSpec((1, hv), lambda b, hh, t, *_: (0, 0)),
    ]
    args = [proj3, proj3, proj3, proj3, gates3, g_norm.reshape(1, hv)]
    has_state0 = state0 is not None
    if has_state0:
        in_specs.append(pl.BlockSpec((None, None, None, hk, hv), lambda b, hh, t, *_: (layer, b, hh, 0, 0)))
        args.append(state0)
    one_state = pl.BlockSpec((None, None, hk, hv), lambda b, hh, t, *_: (b, hh, 0, 0))
    in_specs += [one_state] * len(prev_states)
    args += prev_states
    if stacked:
        state_spec = pl.BlockSpec((n_layers, None, None, hk, hv), lambda b, hh, t, *_: (0, b, hh, 0, 0))
        state_shape = jax.ShapeDtypeStruct((n_layers, batch, h, hk, hv), F32)
    else:
        state_spec = one_state
        state_shape = jax.ShapeDtypeStruct((batch, h, hk, hv), F32)
    og, states = pl.pallas_call(
        functools.partial(_gla_kernel, has_state0=has_state0, n_prev=len(prev_states), n_chunks=tb // chunk,
                          chunk=chunk, rows_per_flag=rows_per_flag, stacked=stacked),
        grid_spec=pltpu.PrefetchScalarGridSpec(
            num_scalar_prefetch=1,
            grid=(batch, h, seq // tb),
            in_specs=in_specs,
            out_specs=[pl.BlockSpec((None, tb, hv), lambda b, hh, t, *_: (b, t, hh)), state_spec],
            scratch_shapes=[pltpu.VMEM((hk, hv), F32)],
        ),
        out_shape=[jax.ShapeDtypeStruct((batch, seq, dv), BF16), state_shape],
        compiler_params=_params(("parallel", "parallel", "arbitrary")),
        name=name,
    )(mild, *args)
    return og.reshape(batch * seq, dv), states


def _attn_kernel(*refs, qb, past, hd, hg):
    if past:
        q_ref, k_ref, v_ref, kc_ref, vc_ref, bm_ref, o_ref = refs
    else:
        q_ref, k_ref, v_ref, bm_ref, o_ref = refs
        kc_ref = vc_ref = None
    tq = q_ref.shape[0]
    left = LEFT_CHUNKS * CHUNK
    wfull = bm_ref.shape[-1]
    blocks = [(hh, i) for hh in range(hg) for i in range(tq // qb)]

    def window(i):
        return max(0, i * qb + past - left), i * qb + past + qb

    def rows(new_ref, cache_ref, k0, k1, hh):
        cs = slice(hh * hd, (hh + 1) * hd)
        parts = []
        if k0 < past:
            parts.append(cache_ref[k0:min(k1, past), cs])
        if k1 > past:
            parts.append(new_ref[max(k0, past) - past:k1 - past, cs])
        return parts[0] if len(parts) == 1 else jnp.concatenate(parts, axis=0)

    def scores(hh, i):
        k0, k1 = window(i)
        s = lax.dot_general(q_ref[i * qb:(i + 1) * qb, hh * hd:(hh + 1) * hd], rows(k_ref, kc_ref, k0, k1, hh),
                            (((1,), (1,)), ((), ())), preferred_element_type=F32)
        return s + bm_ref[hh, :, wfull - (k1 - k0):]

    def store(hh, i, o, denom):
        o_ref[i * qb:(i + 1) * qb, hh * hd:(hh + 1) * hd] = (o * (1.0 / denom)).astype(o_ref.dtype)

    s_next = scores(*blocks[0])
    pending = None
    for n, (hh, i) in enumerate(blocks):
        s = s_next
        if n + 1 < len(blocks):
            s_next = scores(*blocks[n + 1])
        e = jnp.exp2(s - jnp.max(s, axis=-1, keepdims=True))
        k0, k1 = window(i)
        o = jnp.dot(e.astype(BF16), rows(v_ref, vc_ref, k0, k1, hh), preferred_element_type=F32)
        if pending is not None:
            store(*pending)
        pending = (hh, i, o, jnp.sum(e, axis=-1, keepdims=True))
    store(*pending)


def _attention(q, kv, bm, cache_k=None, cache_v=None, name="attn"):
    batch, tq, d = q.shape
    hd = d // ATT_HEADS
    hg = ATT_HEAD_GROUP if tq > bm.shape[1] else ATT_HEADS
    gw = hg * hd
    qb = bm.shape[1]
    past = 0 if cache_k is None else cache_k.shape[1]
    assert PAST_LEN % CHUNK == 0 and past in (0, min(LEFT_CHUNKS * CHUNK, PAST_LEN))
    in_specs = [
        pl.BlockSpec((None, tq, gw), lambda b, g: (b, 0, g)),
        pl.BlockSpec((None, tq, gw), lambda b, g: (b, 0, g)),
        pl.BlockSpec((None, tq, gw), lambda b, g: (b, 0, d // gw + g)),
    ]
    args = [q, kv, kv]
    if past:
        in_specs += [pl.BlockSpec((None, past, gw), lambda b, g: (b, 0, g))] * 2
        args += [cache_k, cache_v]
    in_specs.append(pl.BlockSpec((hg, qb, bm.shape[2]), lambda b, g: (g, 0, 0)))
    args.append(bm)
    return pl.pallas_call(
        functools.partial(_attn_kernel, qb=qb, past=past, hd=hd, hg=hg),
        grid=(batch, ATT_HEADS // hg),
        in_specs=in_specs,
        out_specs=pl.BlockSpec((None, tq, gw), lambda b, g: (b, 0, g)),
        out_shape=jax.ShapeDtypeStruct((batch, tq, d), BF16),
        compiler_params=_params(("parallel", "parallel")),
        name=name,
    )(*args)


def _bias_mask(table, qb):
    left = LEFT_CHUNKS * CHUNK
    h = table.shape[0]
    t = table.astype(F32) * LOG2E
    sat = left + 2 * qb
    ext = jnp.concatenate([jnp.broadcast_to(t[:, :1], (h, sat)), t, jnp.broadcast_to(t[:, -1:], (h, sat))], axis=1)
    top = left + qb - 1
    n_w = left + 2 * qb - 1
    start = ext.shape[1] - 1 - (top + MAX_REL + sat)
    desc = ext[:, ::-1][:, start:start + n_w]
    w = jnp.concatenate([desc[:, qb - 1:], desc[:, :qb - 1]], axis=1)
    bias = jnp.tile(w, (1, qb))[:, :qb * (n_w - 1)].reshape(h, qb, n_w - 1)[:, :, :left + qb]
    r = jnp.arange(qb)[:, None]
    c = jnp.arange(left + qb)[None, :]
    qc = r // CHUNK
    kc = c // CHUNK - LEFT_CHUNKS
    allowed = (kc <= qc) & (kc >= qc - LEFT_CHUNKS)
    return jnp.where(allowed[None], bias, NEG_INF)


def _trunk(x, state0, cache_k, cache_v, w, keep):
    batch, seq, d = x.shape
    depth = w["norm_mix"].shape[0]
    n_a = depth // 2
    x = x.reshape(batch * seq, d)
    states = []
    k_rows = None
    v_rows = None
    kv16 = None
    qb = _tile(seq, ATT_QBLOCK)
    if cache_k is not None:
        cache16 = (cache_k.astype(BF16).reshape(batch, -1, d), cache_v.astype(BF16).reshape(batch, -1, d))
    for layer in range(depth):
        if layer < n_a:
            proj, gates, gmin = _gla_in(x, w["norm_mix"][layer], w["gla_w_in"][layer], w["gla_w_lr"][layer],
                                        w["gla_w_gk"][layer], w["gla_b_gk"][layer], 3 * d)
            mild = (jnp.min(gmin.reshape(gmin.shape[0], GLA_HEADS, -1), axis=-1)
                    >= -GLA_SINGLE_REF_LOG2_RANGE).astype(jnp.int32)
            og, s_end = _gla(proj, gates, mild, x.shape[0] // gmin.shape[0], w["gla_g_norm"][layer],
                             state0, layer, n_a, states if layer == n_a - 1 else [], batch, seq)
            states = s_end if layer == n_a - 1 else states + [s_end]
            x = _matmul_res(og, w["gla_w_out"][layer], x, name="gla_out")
        else:
            j = layer - n_a
            if layer == n_a:
                (kv16,) = _norm_matmul(x, w["norm_kv"], [w["w_kv"]], BF16, name="kv_proj")
                if keep == seq:
                    row_tiles = None
                else:
                    assert seq % keep == 0
                    per_seq = seq // keep
                    row_tiles = (keep, batch, lambda i: i * per_seq + per_seq - 1)
                k_rows, v_rows = _norm_matmul(x, w["norm_kv"], [w["w_kv"], w["w_kv"]], F32,
                                              row_tiles=row_tiles, col_windows=(d, [0, d]), name="kv_rows")
            (q16,) = _norm_matmul(x, w["norm_mix"][layer], [w["att_w_q"][j]], BF16,
                                  out_scale=(d // ATT_HEADS) ** -0.5 * LOG2E, name="q_proj")
            q3 = q16.reshape(batch, seq, d)
            bm = _bias_mask(w["att_rel_bias"][j], qb)
            kv3 = kv16.reshape(batch, seq, 2 * d)
            if cache_k is None:
                o = _attention(q3, kv3, bm)
            else:
                o = _attention(q3, kv3, bm, *cache16)
            x = _matmul_res(o.reshape(batch * seq, d), w["att_w_out"][j], x, name="att_out")
        g_final = w["norm_final"] if layer == depth - 1 else None
        x = _mlp(x, w["norm_ffn"][layer], w["w_ff1"][layer], w["w_ff2"][layer], g_final)
    hd = d // ATT_HEADS
    k = k_rows.reshape(batch, keep, ATT_HEADS, hd)
    v = v_rows.reshape(batch, keep, ATT_HEADS, hd)
    return x.reshape(batch, seq, d), states, k, v


def kernel(x_prompt, x_sample, state_gla, cache_k, cache_v, norm_mix, norm_ffn, w_ff1, w_ff2, gla_w_in, gla_w_gk, gla_b_gk, gla_g_norm, gla_w_out, norm_kv, w_kv, att_w_q, att_rel_bias, att_w_out, norm_final):
    d = x_prompt.shape[-1]
    dk = d // 2
    n_main = 2 * dk + 2 * d
    rank = gla_w_in.shape[-1] - n_main
    pad = LANES - rank
    w = {
        "norm_mix": norm_mix, "norm_ffn": norm_ffn, "norm_kv": norm_kv, "norm_final": norm_final,
        "w_ff1": w_ff1.astype(BF16), "w_ff2": w_ff2.astype(BF16),
        "gla_w_in": gla_w_in.astype(BF16),
        "gla_w_lr": jnp.pad(gla_w_in[:, :, n_main:], ((0, 0), (0, 0), (0, pad))).astype(BF16),
        "gla_w_gk": jnp.pad(gla_w_gk, ((0, 0), (0, pad), (0, 0))).astype(BF16),
        "gla_b_gk": gla_b_gk, "gla_g_norm": gla_g_norm,
        "gla_w_out": gla_w_out.astype(BF16),
        "w_kv": w_kv.astype(BF16), "att_w_q": att_w_q.astype(BF16),
        "att_rel_bias": att_rel_bias, "att_w_out": att_w_out.astype(BF16),
    }
    seq = x_prompt.shape[1]
    keep = min(LEFT_CHUNKS * CHUNK, seq)
    y_p, s_p, k_p, v_p = _trunk(x_prompt, None, None, None, w, keep)
    y_s, s_s, k_s, v_s = _trunk(x_sample, state_gla, cache_k, cache_v, w, x_sample.shape[1])
    return (y_p, y_s, s_p, k_p, v_p, s_s, k_s, v_s)
```

```python
import functools

import jax
import jax.numpy as jnp
from jax import lax
from jax.experimental import pallas as pl
from jax.experimental.pallas import tpu as pltpu

F32 = jnp.float32
BF16 = jnp.bfloat16

CHUNK = 64
GLA_HEADS = 4
GLA_SUB = 16
GLA_SUB_SHIFT = GLA_SUB.bit_length() - 1
GLA_GATE_NORM = 16.0
GLA_SINGLE_REF_LOG2_RANGE = 64.0
ATT_HEADS = 16
LEFT_CHUNKS = 8
PAST_LEN = 2048
MAX_REL = 128
EPS = 1e-6
NEG_INF = -1e30
LOG2E = 1.4426950408889634

LANES = 128
VMEM_LIMIT_BYTES = 60 * 2**20
ROW_TILE = 1024
GLA_IN_STEPS = 4
COL_TILE = 2048
COL_TILE_MULTI = 1024
RES_ROW_TILE = 512
FF_TILE = 1024
NORM_ROWS = 256
ATT_QBLOCK = 4 * CHUNK
ATT_HEAD_GROUP = 4
GLA_ROWS = 16 * CHUNK
GLA_CHUNK = 4 * CHUNK
SUBLANES = 8


def _tile(n, pref):
    if n <= pref:
        return n
    t = pref
    while n % t:
        t //= 2
    return t


def _params(sem):
    return pltpu.CompilerParams(dimension_semantics=sem, vmem_limit_bytes=VMEM_LIMIT_BYTES)


def _rmsnorm(x, g):
    ms = jnp.mean(x * x, axis=-1, keepdims=True)
    return (x * lax.rsqrt(ms + EPS) * g).astype(BF16)


def _norm_matmul_kernel(*refs, n_w, out_scale):
    x_ref, g_ref = refs[:2]
    w_refs = refs[2:2 + n_w]
    out_refs = refs[2 + n_w:2 + 2 * n_w]
    h_ref = refs[2 + 2 * n_w]

    def project(h, rows):
        for w_ref, o_ref in zip(w_refs, out_refs):
            y = jnp.dot(h, w_ref[...], preferred_element_type=F32)
            if out_scale is not None:
                y = y * out_scale
            o_ref[rows, :] = y.astype(o_ref.dtype)

    @pl.when(pl.program_id(1) == 0)
    def _():
        rs = _tile(x_ref.shape[0], NORM_ROWS)
        n_blocks = x_ref.shape[0] // rs
        h_next = _rmsnorm(x_ref[0:rs, :], g_ref[...])
        for s in range(n_blocks):
            h = h_next
            if s + 1 < n_blocks:
                h_next = _rmsnorm(x_ref[(s + 1) * rs:(s + 2) * rs, :], g_ref[...])
            h_ref[s * rs:(s + 1) * rs, :] = h
            project(h, slice(s * rs, (s + 1) * rs))

    @pl.when(pl.program_id(1) > 0)
    def _():
        project(h_ref[...], slice(None))


def _norm_matmul(x, g, ws, out_dtype, out_scale=None, row_tiles=None, col_windows=None, name="norm_matmul"):
    m, d = x.shape
    n, col0 = (ws[0].shape[1], [0] * len(ws)) if col_windows is None else col_windows
    if row_tiles is None:
        tm = _tile(m, ROW_TILE)
        n_tiles, block_of_tile = m // tm, lambda i: i
    else:
        tm, n_tiles, block_of_tile = row_tiles
    tn = _tile(n, COL_TILE if len(ws) == 1 else COL_TILE_MULTI)
    assert all(c0 % tn == 0 for c0 in col0)
    in_specs = [
        pl.BlockSpec((tm, d), lambda i, j: (block_of_tile(i), 0)),
        pl.BlockSpec((1, d), lambda i, j: (0, 0)),
    ] + [pl.BlockSpec((d, tn), functools.partial(lambda i, j, first: (0, first // tn + j), first=c0)) for c0 in col0]
    return pl.pallas_call(
        functools.partial(_norm_matmul_kernel, n_w=len(ws), out_scale=out_scale),
        grid=(n_tiles, n // tn),
        in_specs=in_specs,
        out_specs=[pl.BlockSpec((tm, tn), lambda i, j: (i, j)) for _ in ws],
        out_shape=[jax.ShapeDtypeStruct((n_tiles * tm, n), out_dtype) for _ in ws],
        scratch_shapes=[pltpu.VMEM((tm, d), BF16)],
        compiler_params=_params(("parallel", "arbitrary")),
        name=name,
    )(x, g.reshape(1, d), *ws)


def _gla_in_kernel(x_ref, g_ref, w_ref, wlr_ref, wgk_ref, bgk_ref, proj_ref, gates_ref, gmin_ref, h_ref, lr_ref):
    def project(h, lr, rows):
        proj_ref[rows, :] = jnp.dot(h, w_ref[...], preferred_element_type=F32).astype(proj_ref.dtype)
        z = jnp.dot(lr, wgk_ref[...], preferred_element_type=F32) + bgk_ref[...]
        gates_ref[rows, :] = (jnp.minimum(z, 0.0) - jnp.log(1.0 + jnp.exp(-jnp.abs(z)))) * (LOG2E / GLA_GATE_NORM)

    @pl.when(pl.program_id(1) == 0)
    def _():
        rs = _tile(x_ref.shape[0], NORM_ROWS)
        n_blocks = x_ref.shape[0] // rs
        h_next = _rmsnorm(x_ref[0:rs, :], g_ref[...])
        for s in range(n_blocks):
            rows = slice(s * rs, (s + 1) * rs)
            h = h_next
            if s + 1 < n_blocks:
                h_next = _rmsnorm(x_ref[(s + 1) * rs:(s + 2) * rs, :], g_ref[...])
            lr = jnp.dot(h, wlr_ref[...], preferred_element_type=F32).astype(lr_ref.dtype)
            h_ref[rows, :] = h
            lr_ref[rows, :] = lr
            project(h, lr, rows)

    @pl.when(pl.program_id(1) > 0)
    def _():
        project(h_ref[...], lr_ref[...], slice(None))

    gates = gates_ref[...]
    rows, cols = gates.shape
    totals = jnp.sum(gates.reshape(rows // GLA_CHUNK, GLA_CHUNK, cols), axis=1)
    gmin_ref[...] = jnp.min(totals, axis=0, keepdims=True)


def _gla_in(x, g, w_in, w_lr, wgk, bgk, n_main):
    m, d = x.shape
    dk = wgk.shape[1]
    tm = _tile(m, ROW_TILE)
    steps = GLA_IN_STEPS
    tn, tg = n_main // steps, dk // steps
    assert tn * steps == n_main and tg * steps == dk and tn % LANES == 0 and tg % LANES == 0
    assert tm % GLA_CHUNK == 0
    return pl.pallas_call(
        _gla_in_kernel,
        grid=(m // tm, steps),
        in_specs=[
            pl.BlockSpec((tm, d), lambda i, j: (i, 0)),
            pl.BlockSpec((1, d), lambda i, j: (0, 0)),
            pl.BlockSpec((d, tn), lambda i, j: (0, j)),
            pl.BlockSpec((d, w_lr.shape[1]), lambda i, j: (0, 0)),
            pl.BlockSpec((wgk.shape[0], tg), lambda i, j: (0, j)),
            pl.BlockSpec((1, tg), lambda i, j: (0, j)),
        ],
        out_specs=[
            pl.BlockSpec((tm, tn), lambda i, j: (i, j)),
            pl.BlockSpec((tm, tg), lambda i, j: (i, j)),
            pl.BlockSpec((None, 1, tg), lambda i, j: (i, 0, j)),
        ],
        out_shape=[
            jax.ShapeDtypeStruct((m, n_main), BF16),
            jax.ShapeDtypeStruct((m, dk), F32),
            jax.ShapeDtypeStruct((m // tm, 1, dk), F32),
        ],
        scratch_shapes=[pltpu.VMEM((tm, d), BF16), pltpu.VMEM((tm, w_lr.shape[1]), BF16)],
        compiler_params=_params(("parallel", "arbitrary")),
        name="gla_in",
    )(x, g.reshape(1, d), w_in, w_lr, wgk, bgk.reshape(1, dk))


def _matmul_res_kernel(a_ref, w_ref, x_ref, o_ref):
    o_ref[...] = x_ref[...] + jnp.dot(a_ref[...], w_ref[...], preferred_element_type=F32)


def _matmul_res(a, w, x, name="matmul_res"):
    m, k = a.shape
    n = w.shape[1]
    tm = _tile(m, RES_ROW_TILE)
    return pl.pallas_call(
        _matmul_res_kernel,
        grid=(m // tm,),
        in_specs=[
            pl.BlockSpec((tm, k), lambda i: (i, 0)),
            pl.BlockSpec((k, n), lambda i: (0, 0)),
            pl.BlockSpec((tm, n), lambda i: (i, 0)),
        ],
        out_specs=pl.BlockSpec((tm, n), lambda i: (i, 0)),
        out_shape=jax.ShapeDtypeStruct((m, n), F32),
        compiler_params=_params(("parallel",)),
        name=name,
    )(a, w, x)


def _mlp_kernel(*refs, final_norm):
    if final_norm:
        x_ref, g_ref, w1_ref, w2_ref, gf_ref, o_ref, h_ref = refs
    else:
        x_ref, g_ref, w1_ref, w2_ref, o_ref, h_ref = refs
        gf_ref = None
    f = pl.program_id(1)

    def hidden_tile(h):
        a = jnp.dot(h, w1_ref[...], preferred_element_type=F32)
        a = jnp.maximum(a, 0.0)
        return jnp.dot((a * a).astype(BF16), w2_ref[...], preferred_element_type=F32)

    @pl.when(f == 0)
    def _():
        rs = _tile(x_ref.shape[0], NORM_ROWS)
        n_blocks = x_ref.shape[0] // rs
        h_next = _rmsnorm(x_ref[0:rs, :], g_ref[...])
        for s in range(n_blocks):
            rows = slice(s * rs, (s + 1) * rs)
            h = h_next
            if s + 1 < n_blocks:
                h_next = _rmsnorm(x_ref[(s + 1) * rs:(s + 2) * rs, :], g_ref[...])
            h_ref[rows, :] = h
            o_ref[rows, :] = x_ref[rows, :] + hidden_tile(h)

    @pl.when(f > 0)
    def _():
        o_ref[...] += hidden_tile(h_ref[...])

    if final_norm:
        @pl.when(f == pl.num_programs(1) - 1)
        def _():
            rows = o_ref.shape[0]
            rc = _tile(rows, NORM_ROWS)

            def body(r, carry):
                sl = pl.ds(pl.multiple_of(r * rc, rc), rc)
                y = o_ref[sl, :]
                ms = jnp.mean(y * y, axis=-1, keepdims=True)
                o_ref[sl, :] = y * lax.rsqrt(ms + EPS) * gf_ref[...]
                return carry

            lax.fori_loop(0, rows // rc, body, 0)


def _mlp(x, g, w1, w2, g_final=None, name="mlp"):
    m, d = x.shape
    ff = w1.shape[1]
    tm = _tile(m, ROW_TILE)
    tf = _tile(ff, FF_TILE)
    final_norm = g_final is not None
    in_specs = [
        pl.BlockSpec((tm, d), lambda i, f: (i, 0)),
        pl.BlockSpec((1, d), lambda i, f: (0, 0)),
        pl.BlockSpec((d, tf), lambda i, f: (0, f)),
        pl.BlockSpec((tf, d), lambda i, f: (f, 0)),
    ]
    args = [x, g.reshape(1, d), w1, w2]
    if final_norm:
        in_specs.append(pl.BlockSpec((1, d), lambda i, f: (0, 0)))
        args.append(g_final.reshape(1, d))
    return pl.pallas_call(
        functools.partial(_mlp_kernel, final_norm=final_norm),
        grid=(m // tm, ff // tf),
        in_specs=in_specs,
        out_specs=pl.BlockSpec((tm, d), lambda i, f: (i, 0)),
        out_shape=jax.ShapeDtypeStruct((m, d), F32),
        scratch_shapes=[pltpu.VMEM((tm, d), BF16)],
        compiler_params=_params(("parallel", "arbitrary")),
        name=name,
    )(*args)


def _chunk_cumsums(g, n_chunks, c):
    row = lax.broadcasted_iota(jnp.int32, (c, c), 0)
    col = lax.broadcasted_iota(jnp.int32, (c, c), 1)
    tri = (col <= row).astype(BF16)
    g_hi = g.astype(BF16)
    g_r = g - g_hi.astype(F32)
    g_mid = g_r.astype(BF16)
    g_lo = (g_r - g_mid.astype(F32)).astype(BF16)
    out = []
    for ci in range(n_chunks):
        r = slice(ci * c, (ci + 1) * c)
        out.append(jnp.dot(tri, g_hi[r], preferred_element_type=F32)
                   + jnp.dot(tri, g_mid[r], preferred_element_type=F32)
                   + jnp.dot(tri, g_lo[r], preferred_element_type=F32))
    return out


def _gla_offdiag(q, k, b):
    c = q.shape[0]
    a_rows = [jnp.zeros((GLA_SUB, c), F32)]
    for l in range(1, c // GLA_SUB):
        lo = l * GLA_SUB
        ref = b[lo - 1:lo, :]
        q_ref = q[lo:lo + GLA_SUB, :] * jnp.exp2(b[lo:lo + GLA_SUB, :] - ref)
        k_ref = k * jnp.exp2(jnp.minimum(ref - b, 0.0))
        a_rows.append(lax.dot_general(q_ref.astype(BF16), k_ref.astype(BF16),
                                      (((1,), (1,)), ((), ())), preferred_element_type=F32))
    return jnp.concatenate(a_rows, axis=0)


def _gla_diag(q, k, b):
    c, hk = q.shape
    nsub = c // GLA_SUB
    q3 = q.reshape(nsub, GLA_SUB, hk)
    k3 = k.reshape(nsub, GLA_SUB, hk)
    b3 = b.reshape(nsub, GLA_SUB, hk)
    col3 = lax.broadcasted_iota(jnp.int32, (nsub, SUBLANES, c), 2)
    blk3 = lax.broadcasted_iota(jnp.int32, (nsub, SUBLANES, c), 0) * GLA_SUB
    top = jnp.zeros((nsub, SUBLANES, c), F32)
    bot = jnp.zeros((nsub, GLA_SUB - SUBLANES, c), F32)
    for e in range(GLA_SUB):
        lo = 0 if e < SUBLANES else SUBLANES
        k_e = jnp.broadcast_to(k3[:, e:e + 1, :], (nsub, GLA_SUB - lo, hk))
        b_e = jnp.broadcast_to(b3[:, e:e + 1, :], (nsub, GLA_SUB - lo, hk))
        t = q3[:, lo:, :] * k_e * jnp.exp2(jnp.minimum(b3[:, lo:, :] - b_e, 0.0))
        ts = jnp.sum(t, axis=-1, keepdims=True)
        hit = col3 == blk3 + e
        if lo == 0:
            top = jnp.where(hit, ts[:, :SUBLANES, :], top)
            bot = jnp.where(hit, ts[:, SUBLANES:, :], bot)
        else:
            bot = jnp.where(hit, ts, bot)
    return jnp.concatenate([top, bot], axis=1).reshape(c, c)


def _gla_kernel(*refs, has_state0, n_prev, n_chunks, chunk, rows_per_flag, stacked):
    mild_ref, q_ref, k_ref, v_ref, gate_ref, g_ref, gn_ref = refs[:7]
    s0_ref = refs[7] if has_state0 else None
    first_prev = 8 if has_state0 else 7
    prev_refs = refs[first_prev:first_prev + n_prev]
    o_ref, sfin_ref, s_scr = refs[-3:]
    t = pl.program_id(2)
    hk = q_ref.shape[-1]
    scale = hk ** -0.5
    c = chunk

    @pl.when(t == 0)
    def _():
        if has_state0:
            s_scr[...] = s0_ref[...]
        else:
            s_scr[...] = jnp.zeros_like(s_scr)

    bs = _chunk_cumsums(g_ref[...], n_chunks, c)
    row = lax.broadcasted_iota(jnp.int32, (c, c), 0)
    col = lax.broadcasted_iota(jnp.int32, (c, c), 1)
    row_blk = jnp.right_shift(row, GLA_SUB_SHIFT)
    col_blk = jnp.right_shift(col, GLA_SUB_SHIFT)
    below = col_blk < row_blk
    on_diag = (col_blk == row_blk) & (col <= row)

    def prepare(ci, single_ref):
        r = slice(ci * c, (ci + 1) * c)
        q = q_ref[r, :].astype(F32) * scale
        k = k_ref[r, :].astype(F32)
        b = bs[ci]
        b_end = b[c - 1:c, :]
        decay_col = jnp.transpose(jnp.broadcast_to(jnp.exp2(b_end), (LANES, hk)))[:, :1]
        k_dec = (k * jnp.exp2(b_end - b)).astype(BF16)
        if single_ref:
            q_up = (q * jnp.exp2(b - b_end)).astype(BF16)
            scores = lax.dot_general(q_up, k_dec, (((1,), (1,)), ((), ())), preferred_element_type=F32)
        else:
            scores = _gla_offdiag(q, k, b)
        return dict(q=q, k=k, b=b, q_dec=(q * jnp.exp2(b)).astype(BF16), k_dec=k_dec,
                    decay_col=decay_col, scores=scores)

    def finish(ci, o):
        r = slice(ci * c, (ci + 1) * c)
        ms = jnp.mean(o * o, axis=-1, keepdims=True)
        y = o * lax.rsqrt(ms + EPS) * gn_ref[...]
        gate = gate_ref[r, :].astype(F32)
        o_ref[r, :] = (y * (gate * (1.0 / (1.0 + jnp.exp(-gate))))).astype(o_ref.dtype)

    def run(single_ref):
        s = s_scr[...]
        cur = prepare(0, single_ref)
        prev_out = None
        prev_upd = None
        for ci in range(n_chunks):
            if prev_upd is not None:
                s = prev_upd[0] * s + prev_upd[1]
            o_inter = jnp.dot(cur["q_dec"], s.astype(BF16), preferred_element_type=F32)
            if prev_out is not None:
                finish(ci - 1, prev_out[0] + prev_out[1])
            if single_ref:
                a = jnp.where(col <= row, cur["scores"], 0.0).astype(BF16)
            else:
                a_diag = _gla_diag(cur["q"], cur["k"], cur["b"])
                a = jnp.where(below, cur["scores"], jnp.where(on_diag, a_diag, 0.0)).astype(BF16)
            v16 = v_ref[ci * c:(ci + 1) * c, :]
            prev_out = (jnp.dot(a, v16, preferred_element_type=F32), o_inter)
            prev_upd = (cur["decay_col"],
                        lax.dot_general(cur["k_dec"], v16, (((0,), (0,)), ((), ())), preferred_element_type=F32))
            if ci + 1 < n_chunks:
                cur = prepare(ci + 1, single_ref)
        s_scr[...] = prev_upd[0] * s + prev_upd[1]
        finish(n_chunks - 1, prev_out[0] + prev_out[1])

    tile = (pl.program_id(0) * (pl.num_programs(2) * n_chunks * c) + t * (n_chunks * c)) // rows_per_flag
    mild = mild_ref[tile, pl.program_id(1)] != 0

    @pl.when(mild)
    def _():
        run(True)

    @pl.when(jnp.logical_not(mild))
    def _():
        run(False)

    @pl.when(t == pl.num_programs(2) - 1)
    def _():
        if stacked:
            for i, p_ref in enumerate(prev_refs):
                sfin_ref[i] = p_ref[...]
            sfin_ref[n_prev] = s_scr[...]
        else:
            sfin_ref[...] = s_scr[...]


def _gla(proj, gates, mild, rows_per_flag, g_norm, state0, layer, n_layers, prev_states, batch, seq, name="gla"):
    h = GLA_HEADS
    stacked = layer == n_layers - 1
    assert len(prev_states) == (layer if stacked else 0)
    dv = proj.shape[1] // 3
    dk = dv // 2
    hk, hv = dk // h, dv // h
    tb = _tile(seq, GLA_ROWS)
    assert rows_per_flag % tb == 0
    proj3 = proj.reshape(batch, seq, proj.shape[1])
    gates3 = gates.reshape(batch, seq, dk)
    chunk = _tile(tb, GLA_CHUNK)
    in_specs = [
        pl.BlockSpec((None, tb, hk), lambda b, hh, t, *_: (b, t, hh)),
        pl.BlockSpec((None, tb, hk), lambda b, hh, t, *_: (b, t, h + hh)),
        pl.BlockSpec((None, tb, hv), lambda b, hh, t, *_: (b, t, 2 * dk // hv + hh)),
        pl.BlockSpec((None, tb, hv), lambda b, hh, t, *_: (b, t, (2 * dk + dv) // hv + hh)),
        pl.BlockSpec((None, tb, hk), lambda b, hh, t, *_: (b, t, hh)),
        pl.BlockSpec((1, hv), lambda b, hh, t, *_: (0, 0)),
    ]
    args = [proj3, proj3, proj3, proj3, gates3, g_norm.reshape(1, hv)]
    has_state0 = state0 is not None
    if has_state0:
        in_specs.append(pl.BlockSpec((None, None, None, hk, hv), lambda b, hh, t, *_: (layer, b, hh, 0, 0)))
        args.append(state0)
    one_state = pl.BlockSpec((None, None, hk, hv), lambda b, hh, t, *_: (b, hh, 0, 0))
    in_specs += [one_state] * len(prev_states)
    args += prev_states
    if stacked:
        state_spec = pl.BlockSpec((n_layers, None, None, hk, hv), lambda b, hh, t, *_: (0, b, hh, 0, 0))
        state_shape = jax.ShapeDtypeStruct((n_layers, batch, h, hk, hv), F32)
    else:
        state_spec = one_state
        state_shape = jax.ShapeDtypeStruct((batch, h, hk, hv), F32)
    og, states = pl.pallas_call(
        functools.partial(_gla_kernel, has_state0=has_state0, n_prev=len(prev_states), n_chunks=tb // chunk,
                          chunk=chunk, rows_per_flag=rows_per_flag, stacked=stacked),
        grid_spec=pltpu.PrefetchScalarGridSpec(
            num_scalar_prefetch=1,
            grid=(batch, h, seq // tb),
            in_specs=in_specs,
            out_specs=[pl.BlockSpec((None, tb, hv), lambda b, hh, t, *_: (b, t, hh)), state_spec],
            scratch_shapes=[pltpu.VMEM((hk, hv), F32)],
        ),
        out_shape=[jax.ShapeDtypeStruct((batch, seq, dv), BF16), state_shape],
        compiler_params=_params(("parallel", "parallel", "arbitrary")),
        name=name,
    )(mild, *args)
    return og.reshape(batch * seq, dv), states


def _attn_kernel(*refs, qb, past, hd, hg):
    if past:
        q_ref, k_ref, v_ref, kc_ref, vc_ref, bm_ref, o_ref = refs
    else:
        q_ref, k_ref, v_ref, bm_ref, o_ref = refs
        kc_ref = vc_ref = None
    tq = q_ref.shape[0]
    left = LEFT_CHUNKS * CHUNK
    wfull = bm_ref.shape[-1]
    blocks = [(hh, i) for hh in range(hg) for i in range(tq // qb)]

    def window(i):
        return max(0, i * qb + past - left), i * qb + past + qb

    def rows(new_ref, cache_ref, k0, k1, hh):
        cs = slice(hh * hd, (hh + 1) * hd)
        parts = []
        if k0 < past:
            parts.append(cache_ref[k0:min(k1, past), cs])
        if k1 > past:
            parts.append(new_ref[max(k0, past) - past:k1 - past, cs])
        return parts[0] if len(parts) == 1 else jnp.concatenate(parts, axis=0)

    def scores(hh, i):
        k0, k1 = window(i)
        s = lax.dot_general(q_ref[i * qb:(i + 1) * qb, hh * hd:(hh + 1) * hd], rows(k_ref, kc_ref, k0, k1, hh),
                            (((1,), (1,)), ((), ())), preferred_element_type=F32)
        return s + bm_ref[hh, :, wfull - (k1 - k0):]

    def store(hh, i, o, denom):
        o_ref[i * qb:(i + 1) * qb, hh * hd:(hh + 1) * hd] = (o * (1.0 / denom)).astype(o_ref.dtype)

    s_next = scores(*blocks[0])
    pending = None
    for n, (hh, i) in enumerate(blocks):
        s = s_next
        if n + 1 < len(blocks):
            s_next = scores(*blocks[n + 1])
        e = jnp.exp2(s - jnp.max(s, axis=-1, keepdims=True))
        k0, k1 = window(i)
        o = jnp.dot(e.astype(BF16), rows(v_ref, vc_ref, k0, k1, hh), preferred_element_type=F32)
        if pending is not None:
            store(*pending)
        pending = (hh, i, o, jnp.sum(e, axis=-1, keepdims=True))
    store(*pending)


def _attention(q, kv, bm, cache_k=None, cache_v=None, name="attn"):
    batch, tq, d = q.shape
    hd = d // ATT_HEADS
    hg = ATT_HEAD_GROUP if tq > bm.shape[1] else ATT_HEADS
    gw = hg * hd
    qb = bm.shape[1]
    past = 0 if cache_k is None else cache_k.shape[1]
    assert PAST_LEN % CHUNK == 0 and past in (0, min(LEFT_CHUNKS * CHUNK, PAST_LEN))
    in_specs = [
        pl.BlockSpec((None, tq, gw), lambda b, g: (b, 0, g)),
        pl.BlockSpec((None, tq, gw), lambda b, g: (b, 0, g)),
        pl.BlockSpec((None, tq, gw), lambda b, g: (b, 0, d // gw + g)),
    ]
    args = [q, kv, kv]
    if past:
        in_specs += [pl.BlockSpec((None, past, gw), lambda b, g: (b, 0, g))] * 2
        args += [cache_k, cache_v]
    in_specs.append(pl.BlockSpec((hg, qb, bm.shape[2]), lambda b, g: (g, 0, 0)))
    args.append(bm)
    return pl.pallas_call(
        functools.partial(_attn_kernel, qb=qb, past=past, hd=hd, hg=hg),
        grid=(batch, ATT_HEADS // hg),
        in_specs=in_specs,
        out_specs=pl.BlockSpec((None, tq, gw), lambda b, g: (b, 0, g)),
        out_shape=jax.ShapeDtypeStruct((batch, tq, d), BF16),
        compiler_params=_params(("parallel", "parallel")),
        name=name,
    )(*args)


def _bias_mask(table, qb):
    left = LEFT_CHUNKS * CHUNK
    h = table.shape[0]
    t = table.astype(F32) * LOG2E
    sat = left + 2 * qb
    ext = jnp.concatenate([jnp.broadcast_to(t[:, :1], (h, sat)), t, jnp.broadcast_to(t[:, -1:], (h, sat))], axis=1)
    top = left + qb - 1
    n_w = left + 2 * qb - 1
    start = ext.shape[1] - 1 - (top + MAX_REL + sat)
    desc = ext[:, ::-1][:, start:start + n_w]
    w = jnp.concatenate([desc[:, qb - 1:], desc[:, :qb - 1]], axis=1)
    bias = jnp.tile(w, (1, qb))[:, :qb * (n_w - 1)].reshape(h, qb, n_w - 1)[:, :, :left + qb]
    r = jnp.arange(qb)[:, None]
    c = jnp.arange(left + qb)[None, :]
    qc = r // CHUNK
    kc = c // CHUNK - LEFT_CHUNKS
    allowed = (kc <= qc) & (kc >= qc - LEFT_CHUNKS)
    return jnp.where(allowed[None], bias, NEG_INF)


def _trunk(x, state0, cache_k, cache_v, w, keep):
    batch, seq, d = x.shape
    depth = w["norm_mix"].shape[0]
    n_a = depth // 2
    x = x.reshape(batch * seq, d)
    states = []
    k_rows = None
    v_rows = None
    kv16 = None
    qb = _tile(seq, ATT_QBLOCK)
    if cache_k is not None:
        cache16 = (cache_k.astype(BF16).reshape(batch, -1, d), cache_v.astype(BF16).reshape(batch, -1, d))
    for layer in range(depth):
        if layer < n_a:
            proj, gates, gmin = _gla_in(x, w["norm_mix"][layer], w["gla_w_in"][layer], w["gla_w_lr"][layer],
                                        w["gla_w_gk"][layer], w["gla_b_gk"][layer], 3 * d)
            mild = (jnp.min(gmin.reshape(gmin.shape[0], GLA_HEADS, -1), axis=-1)
                    >= -GLA_SINGLE_REF_LOG2_RANGE).astype(jnp.int32)
            og, s_end = _gla(proj, gates, mild, x.shape[0] // gmin.shape[0], w["gla_g_norm"][layer],
                             state0, layer, n_a, states if layer == n_a - 1 else [], batch, seq)
            states = s_end if layer == n_a - 1 else states + [s_end]
            x = _matmul_res(og, w["gla_w_out"][layer], x, name="gla_out")
        else:
            j = layer - n_a
            if layer == n_a:
                (kv16,) = _norm_matmul(x, w["norm_kv"], [w["w_kv"]], BF16, name="kv_proj")
                if keep == seq:
                    row_tiles = None
                else:
                    assert seq % keep == 0
                    per_seq = seq // keep
                    row_tiles = (keep, batch, lambda i: i * per_seq + per_seq - 1)
                k_rows, v_rows = _norm_matmul(x, w["norm_kv"], [w["w_kv"], w["w_kv"]], F32,
                                              row_tiles=row_tiles, col_windows=(d, [0, d]), name="kv_rows")
            (q16,) = _norm_matmul(x, w["norm_mix"][layer], [w["att_w_q"][j]], BF16,
                                  out_scale=(d // ATT_HEADS) ** -0.5 * LOG2E, name="q_proj")
            q3 = q16.reshape(batch, seq, d)
            bm = _bias_mask(w["att_rel_bias"][j], qb)
            kv3 = kv16.reshape(batch, seq, 2 * d)
            if cache_k is None:
                o = _attention(q3, kv3, bm)
            else:
                o = _attention(q3, kv3, bm, *cache16)
            x = _matmul_res(o.reshape(batch * seq, d), w["att_w_out"][j], x, name="att_out")
        g_final = w["norm_final"] if layer == depth - 1 else None
        x = _mlp(x, w["norm_ffn"][layer], w["w_ff1"][layer], w["w_ff2"][layer], g_final)
    hd = d // ATT_HEADS
    k = k_rows.reshape(batch, keep, ATT_HEADS, hd)
    v = v_rows.reshape(batch, keep, ATT_HEADS, hd)
    return x.reshape(batch, seq, d), states, k, v


def kernel(x_prompt, x_sample, state_gla, cache_k, cache_v, norm_mix, norm_ffn, w_ff1, w_ff2, gla_w_in, gla_w_gk, gla_b_gk, gla_g_norm, gla_w_out, norm_kv, w_kv, att_w_q, att_rel_bias, att_w_out, norm_final):
    d = x_prompt.shape[-1]
    dk = d // 2
    n_main = 2 * dk + 2 * d
    rank = gla_w_in.shape[-1] - n_main
    pad = LANES - rank
    w = {
        "norm_mix": norm_mix, "norm_ffn": norm_ffn, "norm_kv": norm_kv, "norm_final": norm_final,
        "w_ff1": w_ff1.astype(BF16), "w_ff2": w_ff2.astype(BF16),
        "gla_w_in": gla_w_in.astype(BF16),
        "gla_w_lr": jnp.pad(gla_w_in[:, :, n_main:], ((0, 0), (0, 0), (0, pad))).astype(BF16),
        "gla_w_gk": jnp.pad(gla_w_gk, ((0, 0), (0, pad), (0, 0))).astype(BF16),
        "gla_b_gk": gla_b_gk, "gla_g_norm": gla_g_norm,
        "gla_w_out": gla_w_out.astype(BF16),
        "w_kv": w_kv.astype(BF16), "att_w_q": att_w_q.astype(BF16),
        "att_rel_bias": att_rel_bias, "att_w_out": att_w_out.astype(BF16),
    }
    seq = x_prompt.shape[1]
    keep = min(LEFT_CHUNKS * CHUNK, seq)
    y_p, s_p, k_p, v_p = _trunk(x_prompt, None, None, None, w, keep)
    y_s, s_s, k_s, v_s = _trunk(x_sample, state_gla, cache_k, cache_v, w, x_sample.shape[1])
    return (y_p, y_s, s_p, k_p, v_p, s_s, k_s, v_s)
```

```python
import functools

import jax
import jax.numpy as jnp
from jax import lax
from jax.experimental import pallas as pl
from jax.experimental.pallas import tpu as pltpu

F32 = jnp.float32
BF16 = jnp.bfloat16

CHUNK = 64
GLA_HEADS = 4
GLA_SUB = 16
GLA_SUB_SHIFT = GLA_SUB.bit_length() - 1
GLA_GATE_NORM = 16.0
GLA_SINGLE_REF_LOG2_RANGE = 64.0
ATT_HEADS = 16
LEFT_CHUNKS = 8
PAST_LEN = 2048
MAX_REL = 128
EPS = 1e-6
NEG_INF = -1e30
LOG2E = 1.4426950408889634

LANES = 128
VMEM_LIMIT_BYTES = 60 * 2**20
ROW_TILE = 1024
GLA_IN_STEPS = 4
COL_TILE = 2048
COL_TILE_MULTI = 1024
RES_ROW_TILE = 512
FF_TILE = 1024
NORM_ROWS = 256
ATT_QBLOCK = 4 * CHUNK
ATT_HEAD_GROUP = 4
GLA_ROWS = 32 * CHUNK
GLA_CHUNK = 4 * CHUNK
SUBLANES = 8


def _tile(n, pref):
    if n <= pref:
        return n
    t = pref
    while n % t:
        t //= 2
    return t


def _params(sem):
    return pltpu.CompilerParams(dimension_semantics=sem, vmem_limit_bytes=VMEM_LIMIT_BYTES)


def _rmsnorm(x, g):
    ms = jnp.mean(x * x, axis=-1, keepdims=True)
    return (x * lax.rsqrt(ms + EPS) * g).astype(BF16)


def _norm_matmul_kernel(*refs, n_w, out_scale):
    x_ref, g_ref = refs[:2]
    w_refs = refs[2:2 + n_w]
    out_refs = refs[2 + n_w:2 + 2 * n_w]
    h_ref = refs[2 + 2 * n_w]

    def project(h, rows):
        for w_ref, o_ref in zip(w_refs, out_refs):
            y = jnp.dot(h, w_ref[...], preferred_element_type=F32)
            if out_scale is not None:
                y = y * out_scale
            o_ref[rows, :] = y.astype(o_ref.dtype)

    @pl.when(pl.program_id(1) == 0)
    def _():
        rs = _tile(x_ref.shape[0], NORM_ROWS)
        n_blocks = x_ref.shape[0] // rs
        h_next = _rmsnorm(x_ref[0:rs, :], g_ref[...])
        for s in range(n_blocks):
            h = h_next
            if s + 1 < n_blocks:
                h_next = _rmsnorm(x_ref[(s + 1) * rs:(s + 2) * rs, :], g_ref[...])
            h_ref[s * rs:(s + 1) * rs, :] = h
            project(h, slice(s * rs, (s + 1) * rs))

    @pl.when(pl.program_id(1) > 0)
    def _():
        project(h_ref[...], slice(None))


def _norm_matmul(x, g, ws, out_dtype, out_scale=None, row_tiles=None, col_windows=None, name="norm_matmul"):
    m, d = x.shape
    n, col0 = (ws[0].shape[1], [0] * len(ws)) if col_windows is None else col_windows
    if row_tiles is None:
        tm = _tile(m, ROW_TILE)
        n_tiles, block_of_tile = m // tm, lambda i: i
    else:
        tm, n_tiles, block_of_tile = row_tiles
    tn = _tile(n, COL_TILE if len(ws) == 1 else COL_TILE_MULTI)
    assert all(c0 % tn == 0 for c0 in col0)
    in_specs = [
        pl.BlockSpec((tm, d), lambda i, j: (block_of_tile(i), 0)),
        pl.BlockSpec((1, d), lambda i, j: (0, 0)),
    ] + [pl.BlockSpec((d, tn), functools.partial(lambda i, j, first: (0, first // tn + j), first=c0)) for c0 in col0]
    return pl.pallas_call(
        functools.partial(_norm_matmul_kernel, n_w=len(ws), out_scale=out_scale),
        grid=(n_tiles, n // tn),
        in_specs=in_specs,
        out_specs=[pl.BlockSpec((tm, tn), lambda i, j: (i, j)) for _ in ws],
        out_shape=[jax.ShapeDtypeStruct((n_tiles * tm, n), out_dtype) for _ in ws],
        scratch_shapes=[pltpu.VMEM((tm, d), BF16)],
        compiler_params=_params(("parallel", "arbitrary")),
        name=name,
    )(x, g.reshape(1, d), *ws)


def _gla_in_kernel(x_ref, g_ref, w_ref, wlr_ref, wgk_ref, bgk_ref, proj_ref, gates_ref, gmin_ref, h_ref, lr_ref):
    def project(h, lr, rows):
        proj_ref[rows, :] = jnp.dot(h, w_ref[...], preferred_element_type=F32).astype(proj_ref.dtype)
        z = jnp.dot(lr, wgk_ref[...], preferred_element_type=F32) + bgk_ref[...]
        gates_ref[rows, :] = (jnp.minimum(z, 0.0) - jnp.log(1.0 + jnp.exp(-jnp.abs(z)))) * (LOG2E / GLA_GATE_NORM)

    @pl.when(pl.program_id(1) == 0)
    def _():
        rs = _tile(x_ref.shape[0], NORM_ROWS)
        n_blocks = x_ref.shape[0] // rs
        h_next = _rmsnorm(x_ref[0:rs, :], g_ref[...])
        for s in range(n_blocks):
            rows = slice(s * rs, (s + 1) * rs)
            h = h_next
            if s + 1 < n_blocks:
                h_next = _rmsnorm(x_ref[(s + 1) * rs:(s + 2) * rs, :], g_ref[...])
            lr = jnp.dot(h, wlr_ref[...], preferred_element_type=F32).astype(lr_ref.dtype)
            h_ref[rows, :] = h
            lr_ref[rows, :] = lr
            project(h, lr, rows)

    @pl.when(pl.program_id(1) > 0)
    def _():
        project(h_ref[...], lr_ref[...], slice(None))

    gates = gates_ref[...]
    rows, cols = gates.shape
    totals = jnp.sum(gates.reshape(rows // GLA_CHUNK, GLA_CHUNK, cols), axis=1)
    gmin_ref[...] = jnp.min(totals, axis=0, keepdims=True)


def _gla_in(x, g, w_in, w_lr, wgk, bgk, n_main):
    m, d = x.shape
    dk = wgk.shape[1]
    tm = _tile(m, ROW_TILE)
    steps = GLA_IN_STEPS
    tn, tg = n_main // steps, dk // steps
    assert tn * steps == n_main and tg * steps == dk and tn % LANES == 0 and tg % LANES == 0
    assert tm % GLA_CHUNK == 0
    return pl.pallas_call(
        _gla_in_kernel,
        grid=(m // tm, steps),
        in_specs=[
            pl.BlockSpec((tm, d), lambda i, j: (i, 0)),
            pl.BlockSpec((1, d), lambda i, j: (0, 0)),
            pl.BlockSpec((d, tn), lambda i, j: (0, j)),
            pl.BlockSpec((d, w_lr.shape[1]), lambda i, j: (0, 0)),
            pl.BlockSpec((wgk.shape[0], tg), lambda i, j: (0, j)),
            pl.BlockSpec((1, tg), lambda i, j: (0, j)),
        ],
        out_specs=[
            pl.BlockSpec((tm, tn), lambda i, j: (i, j)),
            pl.BlockSpec((tm, tg), lambda i, j: (i, j)),
            pl.BlockSpec((None, 1, tg), lambda i, j: (i, 0, j)),
        ],
        out_shape=[
            jax.ShapeDtypeStruct((m, n_main), BF16),
            jax.ShapeDtypeStruct((m, dk), F32),
            jax.ShapeDtypeStruct((m // tm, 1, dk), F32),
        ],
        scratch_shapes=[pltpu.VMEM((tm, d), BF16), pltpu.VMEM((tm, w_lr.shape[1]), BF16)],
        compiler_params=_params(("parallel", "arbitrary")),
        name="gla_in",
    )(x, g.reshape(1, d), w_in, w_lr, wgk, bgk.reshape(1, dk))


def _matmul_res_kernel(a_ref, w_ref, x_ref, o_ref):
    o_ref[...] = x_ref[...] + jnp.dot(a_ref[...], w_ref[...], preferred_element_type=F32)


def _matmul_res(a, w, x, name="matmul_res"):
    m, k = a.shape
    n = w.shape[1]
    tm = _tile(m, RES_ROW_TILE)
    return pl.pallas_call(
        _matmul_res_kernel,
        grid=(m // tm,),
        in_specs=[
            pl.BlockSpec((tm, k), lambda i: (i, 0)),
            pl.BlockSpec((k, n), lambda i: (0, 0)),
            pl.BlockSpec((tm, n), lambda i: (i, 0)),
        ],
        out_specs=pl.BlockSpec((tm, n), lambda i: (i, 0)),
        out_shape=jax.ShapeDtypeStruct((m, n), F32),
        compiler_params=_params(("parallel",)),
        name=name,
    )(a, w, x)


def _mlp_kernel(*refs, final_norm):
    if final_norm:
        x_ref, g_ref, w1_ref, w2_ref, gf_ref, o_ref, h_ref = refs
    else:
        x_ref, g_ref, w1_ref, w2_ref, o_ref, h_ref = refs
        gf_ref = None
    f = pl.program_id(1)

    def hidden_tile(h):
        a = jnp.dot(h, w1_ref[...], preferred_element_type=F32)
        a = jnp.maximum(a, 0.0)
        return jnp.dot((a * a).astype(BF16), w2_ref[...], preferred_element_type=F32)

    @pl.when(f == 0)
    def _():
        rs = _tile(x_ref.shape[0], NORM_ROWS)
        n_blocks = x_ref.shape[0] // rs
        h_next = _rmsnorm(x_ref[0:rs, :], g_ref[...])
        for s in range(n_blocks):
            rows = slice(s * rs, (s + 1) * rs)
            h = h_next
            if s + 1 < n_blocks:
                h_next = _rmsnorm(x_ref[(s + 1) * rs:(s + 2) * rs, :], g_ref[...])
            h_ref[rows, :] = h
            o_ref[rows, :] = x_ref[rows, :] + hidden_tile(h)

    @pl.when(f > 0)
    def _():
        o_ref[...] += hidden_tile(h_ref[...])

    if final_norm:
        @pl.when(f == pl.num_programs(1) - 1)
        def _():
            rows = o_ref.shape[0]
            rc = _tile(rows, NORM_ROWS)

            def body(r, carry):
                sl = pl.ds(pl.multiple_of(r * rc, rc), rc)
                y = o_ref[sl, :]
                ms = jnp.mean(y * y, axis=-1, keepdims=True)
                o_ref[sl, :] = y * lax.rsqrt(ms + EPS) * gf_ref[...]
                return carry

            lax.fori_loop(0, rows // rc, body, 0)


def _mlp(x, g, w1, w2, g_final=None, name="mlp"):
    m, d = x.shape
    ff = w1.shape[1]
    tm = _tile(m, ROW_TILE)
    tf = _tile(ff, FF_TILE)
    final_norm = g_final is not None
    in_specs = [
        pl.BlockSpec((tm, d), lambda i, f: (i, 0)),
        pl.BlockSpec((1, d), lambda i, f: (0, 0)),
        pl.BlockSpec((d, tf), lambda i, f: (0, f)),
        pl.BlockSpec((tf, d), lambda i, f: (f, 0)),
    ]
    args = [x, g.reshape(1, d), w1, w2]
    if final_norm:
        in_specs.append(pl.BlockSpec((1, d), lambda i, f: (0, 0)))
        args.append(g_final.reshape(1, d))
    return pl.pallas_call(
        functools.partial(_mlp_kernel, final_norm=final_norm),
        grid=(m // tm, ff // tf),
        in_specs=in_specs,
        out_specs=pl.BlockSpec((tm, d), lambda i, f: (i, 0)),
        out_shape=jax.ShapeDtypeStruct((m, d), F32),
        scratch_shapes=[pltpu.VMEM((tm, d), BF16)],
        compiler_params=_params(("parallel", "arbitrary")),
        name=name,
    )(*args)


def _chunk_cumsums(g, n_chunks, c):
    row = lax.broadcasted_iota(jnp.int32, (c, c), 0)
    col = lax.broadcasted_iota(jnp.int32, (c, c), 1)
    tri = (col <= row).astype(BF16)
    g_hi = g.astype(BF16)
    g_r = g - g_hi.astype(F32)
    g_mid = g_r.astype(BF16)
    g_lo = (g_r - g_mid.astype(F32)).astype(BF16)
    out = []
    for ci in range(n_chunks):
        r = slice(ci * c, (ci + 1) * c)
        out.append(jnp.dot(tri, g_hi[r], preferred_element_type=F32)
                   + jnp.dot(tri, g_mid[r], preferred_element_type=F32)
                   + jnp.dot(tri, g_lo[r], preferred_element_type=F32))
    return out


def _gla_offdiag(q, k, b):
    c = q.shape[0]
    a_rows = [jnp.zeros((GLA_SUB, c), F32)]
    for l in range(1, c // GLA_SUB):
        lo = l * GLA_SUB
        ref = b[lo - 1:lo, :]
        q_ref = q[lo:lo + GLA_SUB, :] * jnp.exp2(b[lo:lo + GLA_SUB, :] - ref)
        k_ref = k * jnp.exp2(jnp.minimum(ref - b, 0.0))
        a_rows.append(lax.dot_general(q_ref.astype(BF16), k_ref.astype(BF16),
                                      (((1,), (1,)), ((), ())), preferred_element_type=F32))
    return jnp.concatenate(a_rows, axis=0)


def _gla_diag(q, k, b):
    c, hk = q.shape
    nsub = c // GLA_SUB
    q3 = q.reshape(nsub, GLA_SUB, hk)
    k3 = k.reshape(nsub, GLA_SUB, hk)
    b3 = b.reshape(nsub, GLA_SUB, hk)
    col3 = lax.broadcasted_iota(jnp.int32, (nsub, SUBLANES, c), 2)
    blk3 = lax.broadcasted_iota(jnp.int32, (nsub, SUBLANES, c), 0) * GLA_SUB
    top = jnp.zeros((nsub, SUBLANES, c), F32)
    bot = jnp.zeros((nsub, GLA_SUB - SUBLANES, c), F32)
    for e in range(GLA_SUB):
        lo = 0 if e < SUBLANES else SUBLANES
        k_e = jnp.broadcast_to(k3[:, e:e + 1, :], (nsub, GLA_SUB - lo, hk))
        b_e = jnp.broadcast_to(b3[:, e:e + 1, :], (nsub, GLA_SUB - lo, hk))
        t = q3[:, lo:, :] * k_e * jnp.exp2(jnp.minimum(b3[:, lo:, :] - b_e, 0.0))
        ts = jnp.sum(t, axis=-1, keepdims=True)
        hit = col3 == blk3 + e
        if lo == 0:
            top = jnp.where(hit, ts[:, :SUBLANES, :], top)
            bot = jnp.where(hit, ts[:, SUBLANES:, :], bot)
        else:
            bot = jnp.where(hit, ts, bot)
    return jnp.concatenate([top, bot], axis=1).reshape(c, c)


def _gla_kernel(*refs, has_state0, n_prev, n_chunks, chunk, rows_per_flag, stacked):
    mild_ref, q_ref, k_ref, v_ref, gate_ref, g_ref, gn_ref = refs[:7]
    s0_ref = refs[7] if has_state0 else None
    first_prev = 8 if has_state0 else 7
    prev_refs = refs[first_prev:first_prev + n_prev]
    o_ref, sfin_ref, s_scr = refs[-3:]
    t = pl.program_id(2)
    hk = q_ref.shape[-1]
    scale = hk ** -0.5
    c = chunk

    @pl.when(t == 0)
    def _():
        if has_state0:
            s_scr[...] = s0_ref[...]
        else:
            s_scr[...] = jnp.zeros_like(s_scr)

    bs = _chunk_cumsums(g_ref[...], n_chunks, c)
    row = lax.broadcasted_iota(jnp.int32, (c, c), 0)
    col = lax.broadcasted_iota(jnp.int32, (c, c), 1)
    row_blk = jnp.right_shift(row, GLA_SUB_SHIFT)
    col_blk = jnp.right_shift(col, GLA_SUB_SHIFT)
    below = col_blk < row_blk
    on_diag = (col_blk == row_blk) & (col <= row)

    def prepare(ci, single_ref):
        r = slice(ci * c, (ci + 1) * c)
        q = q_ref[r, :].astype(F32) * scale
        k = k_ref[r, :].astype(F32)
        b = bs[ci]
        b_end = b[c - 1:c, :]
        decay_col = jnp.transpose(jnp.broadcast_to(jnp.exp2(b_end), (LANES, hk)))[:, :1]
        k_dec = (k * jnp.exp2(b_end - b)).astype(BF16)
        if single_ref:
            q_up = (q * jnp.exp2(b - b_end)).astype(BF16)
            scores = lax.dot_general(q_up, k_dec, (((1,), (1,)), ((), ())), preferred_element_type=F32)
        else:
            scores = _gla_offdiag(q, k, b)
        return dict(q=q, k=k, b=b, q_dec=(q * jnp.exp2(b)).astype(BF16), k_dec=k_dec,
                    decay_col=decay_col, scores=scores)

    def finish(ci, o):
        r = slice(ci * c, (ci + 1) * c)
        ms = jnp.mean(o * o, axis=-1, keepdims=True)
        y = o * lax.rsqrt(ms + EPS) * gn_ref[...]
        gate = gate_ref[r, :].astype(F32)
        o_ref[r, :] = (y * (gate * (1.0 / (1.0 + jnp.exp(-gate))))).astype(o_ref.dtype)

    def run(single_ref):
        s = s_scr[...]
        cur = prepare(0, single_ref)
        prev_out = None
        prev_upd = None
        for ci in range(n_chunks):
            if prev_upd is not None:
                s = prev_upd[0] * s + prev_upd[1]
            o_inter = jnp.dot(cur["q_dec"], s.astype(BF16), preferred_element_type=F32)
            if prev_out is not None:
                finish(ci - 1, prev_out[0] + prev_out[1])
            if single_ref:
                a = jnp.where(col <= row, cur["scores"], 0.0).astype(BF16)
            else:
                a_diag = _gla_diag(cur["q"], cur["k"], cur["b"])
                a = jnp.where(below, cur["scores"], jnp.where(on_diag, a_diag, 0.0)).astype(BF16)
            v16 = v_ref[ci * c:(ci + 1) * c, :]
            prev_out = (jnp.dot(a, v16, preferred_element_type=F32), o_inter)
            prev_upd = (cur["decay_col"],
                        lax.dot_general(cur["k_dec"], v16, (((0,), (0,)), ((), ())), preferred_element_type=F32))
            if ci + 1 < n_chunks:
                cur = prepare(ci + 1, single_ref)
        s_scr[...] = prev_upd[0] * s + prev_upd[1]
        finish(n_chunks - 1, prev_out[0] + prev_out[1])

    step_rows = n_chunks * c
    first_row = pl.program_id(0) * (pl.num_programs(2) * step_rows) + t * step_rows
    flags_per_step = max(1, step_rows // rows_per_flag)
    n_mild = mild_ref[first_row // rows_per_flag, pl.program_id(1)]
    for i in range(1, flags_per_step):
        n_mild = n_mild + mild_ref[first_row // rows_per_flag + i, pl.program_id(1)]
    mild = n_mild == flags_per_step

    @pl.when(mild)
    def _():
        run(True)

    @pl.when(jnp.logical_not(mild))
    def _():
        run(False)

    @pl.when(t == pl.num_programs(2) - 1)
    def _():
        if stacked:
            for i, p_ref in enumerate(prev_refs):
                sfin_ref[i] = p_ref[...]
            sfin_ref[n_prev] = s_scr[...]
        else:
            sfin_ref[...] = s_scr[...]


def _gla(proj, gates, mild, rows_per_flag, g_norm, state0, layer, n_layers, prev_states, batch, seq, name="gla"):
    h = GLA_HEADS
    stacked = layer == n_layers - 1
    assert len(prev_states) == (layer if stacked else 0)
    dv = proj.shape[1] // 3
    dk = dv // 2
    hk, hv = dk // h, dv // h
    tb = _tile(seq, GLA_ROWS)
    assert rows_per_flag % tb == 0 or tb % rows_per_flag == 0
    proj3 = proj.reshape(batch, seq, proj.shape[1])
    gates3 = gates.reshape(batch, seq, dk)
    chunk = _tile(tb, GLA_CHUNK)
    in_specs = [
        pl.BlockSpec((None, tb, hk), lambda b, hh, t, *_: (b, t, hh)),
        pl.BlockSpec((None, tb, hk), lambda b, hh, t, *_: (b, t, h + hh)),
        pl.BlockSpec((None, tb, hv), lambda b, hh, t, *_: (b, t, 2 * dk // hv + hh)),
        pl.BlockSpec((None, tb, hv), lambda b, hh, t, *_: (b, t, (2 * dk + dv) // hv + hh)),
        pl.BlockSpec((None, tb, hk), lambda b, hh, t, *_: (b, t, hh)),
        pl.BlockSpec((1, hv), lambda b, hh, t, *_: (0, 0)),
    ]
    args = [proj3, proj3, proj3, proj3, gates3, g_norm.reshape(1, hv)]
    has_state0 = state0 is not None
    if has_state0:
        in_specs.append(pl.BlockSpec((None, None, None, hk, hv), lambda b, hh, t, *_: (layer, b, hh, 0, 0)))
        args.append(state0)
    one_state = pl.BlockSpec((None, None, hk, hv), lambda b, hh, t, *_: (b, hh, 0, 0))
    in_specs += [one_state] * len(prev_states)
    args += prev_states
    if stacked:
        state_spec = pl.BlockSpec((n_layers, None, None, hk, hv), lambda b, hh, t, *_: (0, b, hh, 0, 0))
        state_shape = jax.ShapeDtypeStruct((n_layers, batch, h, hk, hv), F32)
    else:
        state_spec = one_state
        state_shape = jax.ShapeDtypeStruct((batch, h, hk, hv), F32)
    og, states = pl.pallas_call(
        functools.partial(_gla_kernel, has_state0=has_state0, n_prev=len(prev_states), n_chunks=tb // chunk,
                          chunk=chunk, rows_per_flag=rows_per_flag, stacked=stacked),
        grid_spec=pltpu.PrefetchScalarGridSpec(
            num_scalar_prefetch=1,
            grid=(batch, h, seq // tb),
            in_specs=in_specs,
            out_specs=[pl.BlockSpec((None, tb, hv), lambda b, hh, t, *_: (b, t, hh)), state_spec],
            scratch_shapes=[pltpu.VMEM((hk, hv), F32)],
        ),
        out_shape=[jax.ShapeDtypeStruct((batch, seq, dv), BF16), state_shape],
        compiler_params=_params(("parallel", "parallel", "arbitrary")),
        name=name,
    )(mild, *args)
    return og.reshape(batch * seq, dv), states


def _attn_kernel(*refs, qb, past, hd, hg):
    if past:
        q_ref, k_ref, v_ref, kc_ref, vc_ref, bm_ref, o_ref = refs
    else:
        q_ref, k_ref, v_ref, bm_ref, o_ref = refs
        kc_ref = vc_ref = None
    tq = q_ref.shape[0]
    left = LEFT_CHUNKS * CHUNK
    wfull = bm_ref.shape[-1]
    blocks = [(hh, i) for hh in range(hg) for i in range(tq // qb)]

    def window(i):
        return max(0, i * qb + past - left), i * qb + past + qb

    def rows(new_ref, cache_ref, k0, k1, hh):
        cs = slice(hh * hd, (hh + 1) * hd)
        parts = []
        if k0 < past:
            parts.append(cache_ref[k0:min(k1, past), cs])
        if k1 > past:
            parts.append(new_ref[max(k0, past) - past:k1 - past, cs])
        return parts[0] if len(parts) == 1 else jnp.concatenate(parts, axis=0)

    def scores(hh, i):
        k0, k1 = window(i)
        s = lax.dot_general(q_ref[i * qb:(i + 1) * qb, hh * hd:(hh + 1) * hd], rows(k_ref, kc_ref, k0, k1, hh),
                            (((1,), (1,)), ((), ())), preferred_element_type=F32)
        return s + bm_ref[hh, :, wfull - (k1 - k0):]

    def store(hh, i, o, denom):
        o_ref[i * qb:(i + 1) * qb, hh * hd:(hh + 1) * hd] = (o * (1.0 / denom)).astype(o_ref.dtype)

    s_next = scores(*blocks[0])
    pending = None
    for n, (hh, i) in enumerate(blocks):
        s = s_next
        if n + 1 < len(blocks):
            s_next = scores(*blocks[n + 1])
        e = jnp.exp2(s - jnp.max(s, axis=-1, keepdims=True))
        k0, k1 = window(i)
        o = jnp.dot(e.astype(BF16), rows(v_ref, vc_ref, k0, k1, hh), preferred_element_type=F32)
        if pending is not None:
            store(*pending)
        pending = (hh, i, o, jnp.sum(e, axis=-1, keepdims=True))
    store(*pending)


def _attention(q, kv, bm, cache_k=None, cache_v=None, name="attn"):
    batch, tq, d = q.shape
    hd = d // ATT_HEADS
    hg = ATT_HEAD_GROUP if tq > bm.shape[1] else ATT_HEADS
    gw = hg * hd
    qb = bm.shape[1]
    past = 0 if cache_k is None else cache_k.shape[1]
    assert PAST_LEN % CHUNK == 0 and past in (0, min(LEFT_CHUNKS * CHUNK, PAST_LEN))
    in_specs = [
        pl.BlockSpec((None, tq, gw), lambda b, g: (b, 0, g)),
        pl.BlockSpec((None, tq, gw), lambda b, g: (b, 0, g)),
        pl.BlockSpec((None, tq, gw), lambda b, g: (b, 0, d // gw + g)),
    ]
    args = [q, kv, kv]
    if past:
        in_specs += [pl.BlockSpec((None, past, gw), lambda b, g: (b, 0, g))] * 2
        args += [cache_k, cache_v]
    in_specs.append(pl.BlockSpec((hg, qb, bm.shape[2]), lambda b, g: (g, 0, 0)))
    args.append(bm)
    return pl.pallas_call(
        functools.partial(_attn_kernel, qb=qb, past=past, hd=hd, hg=hg),
        grid=(batch, ATT_HEADS // hg),
        in_specs=in_specs,
        out_specs=pl.BlockSpec((None, tq, gw), lambda b, g: (b, 0, g)),
        out_shape=jax.ShapeDtypeStruct((batch, tq, d), BF16),
        compiler_params=_params(("parallel", "parallel")),
        name=name,
    )(*args)


def _bias_mask(table, qb):
    left = LEFT_CHUNKS * CHUNK
    h = table.shape[0]
    t = table.astype(F32) * LOG2E
    sat = left + 2 * qb
    ext = jnp.concatenate([jnp.broadcast_to(t[:, :1], (h, sat)), t, jnp.broadcast_to(t[:, -1:], (h, sat))], axis=1)
    top = left + qb - 1
    n_w = left + 2 * qb - 1
    start = ext.shape[1] - 1 - (top + MAX_REL + sat)
    desc = ext[:, ::-1][:, start:start + n_w]
    w = jnp.concatenate([desc[:, qb - 1:], desc[:, :qb - 1]], axis=1)
    bias = jnp.tile(w, (1, qb))[:, :qb * (n_w - 1)].reshape(h, qb, n_w - 1)[:, :, :left + qb]
    r = jnp.arange(qb)[:, None]
    c = jnp.arange(left + qb)[None, :]
    qc = r // CHUNK
    kc = c // CHUNK - LEFT_CHUNKS
    allowed = (kc <= qc) & (kc >= qc - LEFT_CHUNKS)
    return jnp.where(allowed[None], bias, NEG_INF)


def _trunk(x, state0, cache_k, cache_v, w, keep):
    batch, seq, d = x.shape
    depth = w["norm_mix"].shape[0]
    n_a = depth // 2
    x = x.reshape(batch * seq, d)
    states = []
    k_rows = None
    v_rows = None
    kv16 = None
    qb = _tile(seq, ATT_QBLOCK)
    if cache_k is not None:
        cache16 = (cache_k.astype(BF16).reshape(batch, -1, d), cache_v.astype(BF16).reshape(batch, -1, d))
    for layer in range(depth):
        if layer < n_a:
            proj, gates, gmin = _gla_in(x, w["norm_mix"][layer], w["gla_w_in"][layer], w["gla_w_lr"][layer],
                                        w["gla_w_gk"][layer], w["gla_b_gk"][layer], 3 * d)
            mild = (jnp.min(gmin.reshape(gmin.shape[0], GLA_HEADS, -1), axis=-1)
                    >= -GLA_SINGLE_REF_LOG2_RANGE).astype(jnp.int32)
            og, s_end = _gla(proj, gates, mild, x.shape[0] // gmin.shape[0], w["gla_g_norm"][layer],
                             state0, layer, n_a, states if layer == n_a - 1 else [], batch, seq)
            states = s_end if layer == n_a - 1 else states + [s_end]
            x = _matmul_res(og, w["gla_w_out"][layer], x, name="gla_out")
        else:
            j = layer - n_a
            if layer == n_a:
                (kv16,) = _norm_matmul(x, w["norm_kv"], [w["w_kv"]], BF16, name="kv_proj")
                if keep == seq:
                    row_tiles = None
                else:
                    assert seq % keep == 0
                    per_seq = seq // keep
                    row_tiles = (keep, batch, lambda i: i * per_seq + per_seq - 1)
                k_rows, v_rows = _norm_matmul(x, w["norm_kv"], [w["w_kv"], w["w_kv"]], F32,
                                              row_tiles=row_tiles, col_windows=(d, [0, d]), name="kv_rows")
            (q16,) = _norm_matmul(x, w["norm_mix"][layer], [w["att_w_q"][j]], BF16,
                                  out_scale=(d // ATT_HEADS) ** -0.5 * LOG2E, name="q_proj")
            q3 = q16.reshape(batch, seq, d)
            bm = _bias_mask(w["att_rel_bias"][j], qb)
            kv3 = kv16.reshape(batch, seq, 2 * d)
            if cache_k is None:
                o = _attention(q3, kv3, bm)
            else:
                o = _attention(q3, kv3, bm, *cache16)
            x = _matmul_res(o.reshape(batch * seq, d), w["att_w_out"][j], x, name="att_out")
        g_final = w["norm_final"] if layer == depth - 1 else None
        x = _mlp(x, w["norm_ffn"][layer], w["w_ff1"][layer], w["w_ff2"][layer], g_final)
    hd = d // ATT_HEADS
    k = k_rows.reshape(batch, keep, ATT_HEADS, hd)
    v = v_rows.reshape(batch, keep, ATT_HEADS, hd)
    return x.reshape(batch, seq, d), states, k, v


def kernel(x_prompt, x_sample, state_gla, cache_k, cache_v, norm_mix, norm_ffn, w_ff1, w_ff2, gla_w_in, gla_w_gk, gla_b_gk, gla_g_norm, gla_w_out, norm_kv, w_kv, att_w_q, att_rel_bias, att_w_out, norm_final):
    d = x_prompt.shape[-1]
    dk = d // 2
    n_main = 2 * dk + 2 * d
    rank = gla_w_in.shape[-1] - n_main
    pad = LANES - rank
    w = {
        "norm_mix": norm_mix, "norm_ffn": norm_ffn, "norm_kv": norm_kv, "norm_final": norm_final,
        "w_ff1": w_ff1.astype(BF16), "w_ff2": w_ff2.astype(BF16),
        "gla_w_in": gla_w_in.astype(BF16),
        "gla_w_lr": jnp.pad(gla_w_in[:, :, n_main:], ((0, 0), (0, 0), (0, pad))).astype(BF16),
        "gla_w_gk": jnp.pad(gla_w_gk, ((0, 0), (0, pad), (0, 0))).astype(BF16),
        "gla_b_gk": gla_b_gk, "gla_g_norm": gla_g_norm,
        "gla_w_out": gla_w_out.astype(BF16),
        "w_kv": w_kv.astype(BF16), "att_w_q": att_w_q.astype(BF16),
        "att_rel_bias": att_rel_bias, "att_w_out": att_w_out.astype(BF16),
    }
    seq = x_prompt.shape[1]
    keep = min(LEFT_CHUNKS * CHUNK, seq)
    y_p, s_p, k_p, v_p = _trunk(x_prompt, None, None, None, w, keep)
    y_s, s_s, k_s, v_s = _trunk(x_sample, state_gla, cache_k, cache_v, w, x_sample.shape[1])
    return (y_p, y_s, s_p, k_p, v_p, s_s, k_s, v_s)
```

```python
import functools

import jax
import jax.numpy as jnp
from jax import lax
from jax.experimental import pallas as pl
from jax.experimental.pallas import tpu as pltpu

F32 = jnp.float32
BF16 = jnp.bfloat16

CHUNK = 64
GLA_HEADS = 4
GLA_SUB = 16
GLA_SUB_SHIFT = GLA_SUB.bit_length() - 1
GLA_GATE_NORM = 16.0
GLA_SINGLE_REF_LOG2_RANGE = 64.0
ATT_HEADS = 16
LEFT_CHUNKS = 8
PAST_LEN = 2048
MAX_REL = 128
EPS = 1e-6
NEG_INF = -1e30
LOG2E = 1.4426950408889634

LANES = 128
VMEM_LIMIT_BYTES = 60 * 2**20
ROW_TILE = 1024
GLA_IN_STEPS = 4
COL_TILE = 2048
COL_TILE_MULTI = 1024
RES_ROW_TILE = 1024
FF_TILE = 1024
NORM_ROWS = 256
ATT_QBLOCK = 4 * CHUNK
ATT_HEAD_GROUP = 4
GLA_ROWS = 32 * CHUNK
GLA_CHUNK = 4 * CHUNK
SUBLANES = 8


def _tile(n, pref):
    if n <= pref:
        return n
    t = pref
    while n % t:
        t //= 2
    return t


def _params(sem):
    return pltpu.CompilerParams(dimension_semantics=sem, vmem_limit_bytes=VMEM_LIMIT_BYTES)


def _rmsnorm(x, g):
    ms = jnp.mean(x * x, axis=-1, keepdims=True)
    return (x * lax.rsqrt(ms + EPS) * g).astype(BF16)


def _norm_matmul_kernel(*refs, n_w, out_scale):
    x_ref, g_ref = refs[:2]
    w_refs = refs[2:2 + n_w]
    out_refs = refs[2 + n_w:2 + 2 * n_w]
    h_ref = refs[2 + 2 * n_w]

    def project(h, rows):
        for w_ref, o_ref in zip(w_refs, out_refs):
            y = jnp.dot(h, w_ref[...], preferred_element_type=F32)
            if out_scale is not None:
                y = y * out_scale
            o_ref[rows, :] = y.astype(o_ref.dtype)

    @pl.when(pl.program_id(1) == 0)
    def _():
        rs = _tile(x_ref.shape[0], NORM_ROWS)
        n_blocks = x_ref.shape[0] // rs
        h_next = _rmsnorm(x_ref[0:rs, :], g_ref[...])
        for s in range(n_blocks):
            h = h_next
            if s + 1 < n_blocks:
                h_next = _rmsnorm(x_ref[(s + 1) * rs:(s + 2) * rs, :], g_ref[...])
            h_ref[s * rs:(s + 1) * rs, :] = h
            project(h, slice(s * rs, (s + 1) * rs))

    @pl.when(pl.program_id(1) > 0)
    def _():
        project(h_ref[...], slice(None))


def _norm_matmul(x, g, ws, out_dtype, out_scale=None, row_tiles=None, col_windows=None, name="norm_matmul"):
    m, d = x.shape
    n, col0 = (ws[0].shape[1], [0] * len(ws)) if col_windows is None else col_windows
    if row_tiles is None:
        tm = _tile(m, ROW_TILE)
        n_tiles, block_of_tile = m // tm, lambda i: i
    else:
        tm, n_tiles, block_of_tile = row_tiles
    tn = _tile(n, COL_TILE if len(ws) == 1 else COL_TILE_MULTI)
    assert all(c0 % tn == 0 for c0 in col0)
    in_specs = [
        pl.BlockSpec((tm, d), lambda i, j: (block_of_tile(i), 0)),
        pl.BlockSpec((1, d), lambda i, j: (0, 0)),
    ] + [pl.BlockSpec((d, tn), functools.partial(lambda i, j, first: (0, first // tn + j), first=c0)) for c0 in col0]
    return pl.pallas_call(
        functools.partial(_norm_matmul_kernel, n_w=len(ws), out_scale=out_scale),
        grid=(n_tiles, n // tn),
        in_specs=in_specs,
        out_specs=[pl.BlockSpec((tm, tn), lambda i, j: (i, j)) for _ in ws],
        out_shape=[jax.ShapeDtypeStruct((n_tiles * tm, n), out_dtype) for _ in ws],
        scratch_shapes=[pltpu.VMEM((tm, d), BF16)],
        compiler_params=_params(("parallel", "arbitrary")),
        name=name,
    )(x, g.reshape(1, d), *ws)


def _gla_in_kernel(x_ref, g_ref, w_ref, wlr_ref, wgk_ref, bgk_ref, proj_ref, gates_ref, gmin_ref, h_ref, lr_ref):
    def project(h, lr, rows):
        proj_ref[rows, :] = jnp.dot(h, w_ref[...], preferred_element_type=F32).astype(proj_ref.dtype)
        z = jnp.dot(lr, wgk_ref[...], preferred_element_type=F32) + bgk_ref[...]
        gates_ref[rows, :] = (jnp.minimum(z, 0.0) - jnp.log(1.0 + jnp.exp(-jnp.abs(z)))) * (LOG2E / GLA_GATE_NORM)

    @pl.when(pl.program_id(1) == 0)
    def _():
        rs = _tile(x_ref.shape[0], NORM_ROWS)
        n_blocks = x_ref.shape[0] // rs
        h_next = _rmsnorm(x_ref[0:rs, :], g_ref[...])
        for s in range(n_blocks):
            rows = slice(s * rs, (s + 1) * rs)
            h = h_next
            if s + 1 < n_blocks:
                h_next = _rmsnorm(x_ref[(s + 1) * rs:(s + 2) * rs, :], g_ref[...])
            lr = jnp.dot(h, wlr_ref[...], preferred_element_type=F32).astype(lr_ref.dtype)
            h_ref[rows, :] = h
            lr_ref[rows, :] = lr
            project(h, lr, rows)

    @pl.when(pl.program_id(1) > 0)
    def _():
        project(h_ref[...], lr_ref[...], slice(None))

    gates = gates_ref[...]
    rows, cols = gates.shape
    totals = jnp.sum(gates.reshape(rows // GLA_CHUNK, GLA_CHUNK, cols), axis=1)
    gmin_ref[...] = jnp.min(totals, axis=0, keepdims=True)


def _gla_in(x, g, w_in, w_lr, wgk, bgk, n_main):
    m, d = x.shape
    dk = wgk.shape[1]
    tm = _tile(m, ROW_TILE)
    steps = GLA_IN_STEPS
    tn, tg = n_main // steps, dk // steps
    assert tn * steps == n_main and tg * steps == dk and tn % LANES == 0 and tg % LANES == 0
    assert tm % GLA_CHUNK == 0
    return pl.pallas_call(
        _gla_in_kernel,
        grid=(m // tm, steps),
        in_specs=[
            pl.BlockSpec((tm, d), lambda i, j: (i, 0)),
            pl.BlockSpec((1, d), lambda i, j: (0, 0)),
            pl.BlockSpec((d, tn), lambda i, j: (0, j)),
            pl.BlockSpec((d, w_lr.shape[1]), lambda i, j: (0, 0)),
            pl.BlockSpec((wgk.shape[0], tg), lambda i, j: (0, j)),
            pl.BlockSpec((1, tg), lambda i, j: (0, j)),
        ],
        out_specs=[
            pl.BlockSpec((tm, tn), lambda i, j: (i, j)),
            pl.BlockSpec((tm, tg), lambda i, j: (i, j)),
            pl.BlockSpec((None, 1, tg), lambda i, j: (i, 0, j)),
        ],
        out_shape=[
            jax.ShapeDtypeStruct((m, n_main), BF16),
            jax.ShapeDtypeStruct((m, dk), F32),
            jax.ShapeDtypeStruct((m // tm, 1, dk), F32),
        ],
        scratch_shapes=[pltpu.VMEM((tm, d), BF16), pltpu.VMEM((tm, w_lr.shape[1]), BF16)],
        compiler_params=_params(("parallel", "arbitrary")),
        name="gla_in",
    )(x, g.reshape(1, d), w_in, w_lr, wgk, bgk.reshape(1, dk))


def _matmul_res_kernel(a_ref, w_ref, x_ref, o_ref):
    o_ref[...] = x_ref[...] + jnp.dot(a_ref[...], w_ref[...], preferred_element_type=F32)


def _matmul_res(a, w, x, name="matmul_res"):
    m, k = a.shape
    n = w.shape[1]
    tm = _tile(m, RES_ROW_TILE)
    return pl.pallas_call(
        _matmul_res_kernel,
        grid=(m // tm,),
        in_specs=[
            pl.BlockSpec((tm, k), lambda i: (i, 0)),
            pl.BlockSpec((k, n), lambda i: (0, 0)),
            pl.BlockSpec((tm, n), lambda i: (i, 0)),
        ],
        out_specs=pl.BlockSpec((tm, n), lambda i: (i, 0)),
        out_shape=jax.ShapeDtypeStruct((m, n), F32),
        compiler_params=_params(("parallel",)),
        name=name,
    )(a, w, x)


def _mlp_kernel(*refs, final_norm):
    if final_norm:
        x_ref, g_ref, w1_ref, w2_ref, gf_ref, o_ref, h_ref = refs
    else:
        x_ref, g_ref, w1_ref, w2_ref, o_ref, h_ref = refs
        gf_ref = None
    f = pl.program_id(1)

    def hidden_tile(h):
        a = jnp.dot(h, w1_ref[...], preferred_element_type=F32)
        a = jnp.maximum(a, 0.0)
        return jnp.dot((a * a).astype(BF16), w2_ref[...], preferred_element_type=F32)

    @pl.when(f == 0)
    def _():
        rs = _tile(x_ref.shape[0], NORM_ROWS)
        n_blocks = x_ref.shape[0] // rs
        h_next = _rmsnorm(x_ref[0:rs, :], g_ref[...])
        for s in range(n_blocks):
            rows = slice(s * rs, (s + 1) * rs)
            h = h_next
            if s + 1 < n_blocks:
                h_next = _rmsnorm(x_ref[(s + 1) * rs:(s + 2) * rs, :], g_ref[...])
            h_ref[rows, :] = h
            o_ref[rows, :] = x_ref[rows, :] + hidden_tile(h)

    @pl.when(f > 0)
    def _():
        o_ref[...] += hidden_tile(h_ref[...])

    if final_norm:
        @pl.when(f == pl.num_programs(1) - 1)
        def _():
            rows = o_ref.shape[0]
            rc = _tile(rows, NORM_ROWS)

            def body(r, carry):
                sl = pl.ds(pl.multiple_of(r * rc, rc), rc)
                y = o_ref[sl, :]
                ms = jnp.mean(y * y, axis=-1, keepdims=True)
                o_ref[sl, :] = y * lax.rsqrt(ms + EPS) * gf_ref[...]
                return carry

            lax.fori_loop(0, rows // rc, body, 0)


def _mlp(x, g, w1, w2, g_final=None, name="mlp"):
    m, d = x.shape
    ff = w1.shape[1]
    tm = _tile(m, ROW_TILE)
    tf = _tile(ff, FF_TILE)
    final_norm = g_final is not None
    in_specs = [
        pl.BlockSpec((tm, d), lambda i, f: (i, 0)),
        pl.BlockSpec((1, d), lambda i, f: (0, 0)),
        pl.BlockSpec((d, tf), lambda i, f: (0, f)),
        pl.BlockSpec((tf, d), lambda i, f: (f, 0)),
    ]
    args = [x, g.reshape(1, d), w1, w2]
    if final_norm:
        in_specs.append(pl.BlockSpec((1, d), lambda i, f: (0, 0)))
        args.append(g_final.reshape(1, d))
    return pl.pallas_call(
        functools.partial(_mlp_kernel, final_norm=final_norm),
        grid=(m // tm, ff // tf),
        in_specs=in_specs,
        out_specs=pl.BlockSpec((tm, d), lambda i, f: (i, 0)),
        out_shape=jax.ShapeDtypeStruct((m, d), F32),
        scratch_shapes=[pltpu.VMEM((tm, d), BF16)],
        compiler_params=_params(("parallel", "arbitrary")),
        name=name,
    )(*args)


def _chunk_cumsums(g, n_chunks, c):
    row = lax.broadcasted_iota(jnp.int32, (c, c), 0)
    col = lax.broadcasted_iota(jnp.int32, (c, c), 1)
    tri = (col <= row).astype(BF16)
    g_hi = g.astype(BF16)
    g_r = g - g_hi.astype(F32)
    g_mid = g_r.astype(BF16)
    g_lo = (g_r - g_mid.astype(F32)).astype(BF16)
    out = []
    for ci in range(n_chunks):
        r = slice(ci * c, (ci + 1) * c)
        out.append(jnp.dot(tri, g_hi[r], preferred_element_type=F32)
                   + jnp.dot(tri, g_mid[r], preferred_element_type=F32)
                   + jnp.dot(tri, g_lo[r], preferred_element_type=F32))
    return out


def _gla_offdiag(q, k, b):
    c = q.shape[0]
    a_rows = [jnp.zeros((GLA_SUB, c), F32)]
    for l in range(1, c // GLA_SUB):
        lo = l * GLA_SUB
        ref = b[lo - 1:lo, :]
        q_ref = q[lo:lo + GLA_SUB, :] * jnp.exp2(b[lo:lo + GLA_SUB, :] - ref)
        k_ref = k * jnp.exp2(jnp.minimum(ref - b, 0.0))
        a_rows.append(lax.dot_general(q_ref.astype(BF16), k_ref.astype(BF16),
                                      (((1,), (1,)), ((), ())), preferred_element_type=F32))
    return jnp.concatenate(a_rows, axis=0)


def _gla_diag(q, k, b):
    c, hk = q.shape
    nsub = c // GLA_SUB
    q3 = q.reshape(nsub, GLA_SUB, hk)
    k3 = k.reshape(nsub, GLA_SUB, hk)
    b3 = b.reshape(nsub, GLA_SUB, hk)
    col3 = lax.broadcasted_iota(jnp.int32, (nsub, SUBLANES, c), 2)
    blk3 = lax.broadcasted_iota(jnp.int32, (nsub, SUBLANES, c), 0) * GLA_SUB
    top = jnp.zeros((nsub, SUBLANES, c), F32)
    bot = jnp.zeros((nsub, GLA_SUB - SUBLANES, c), F32)
    for e in range(GLA_SUB):
        lo = 0 if e < SUBLANES else SUBLANES
        k_e = jnp.broadcast_to(k3[:, e:e + 1, :], (nsub, GLA_SUB - lo, hk))
        b_e = jnp.broadcast_to(b3[:, e:e + 1, :], (nsub, GLA_SUB - lo, hk))
        t = q3[:, lo:, :] * k_e * jnp.exp2(jnp.minimum(b3[:, lo:, :] - b_e, 0.0))
        ts = jnp.sum(t, axis=-1, keepdims=True)
        hit = col3 == blk3 + e
        if lo == 0:
            top = jnp.where(hit, ts[:, :SUBLANES, :], top)
            bot = jnp.where(hit, ts[:, SUBLANES:, :], bot)
        else:
            bot = jnp.where(hit, ts, bot)
    return jnp.concatenate([top, bot], axis=1).reshape(c, c)


def _gla_kernel(*refs, has_state0, n_prev, n_chunks, chunk, rows_per_flag, stacked):
    mild_ref, q_ref, k_ref, v_ref, gate_ref, g_ref, gn_ref = refs[:7]
    s0_ref = refs[7] if has_state0 else None
    first_prev = 8 if has_state0 else 7
    prev_refs = refs[first_prev:first_prev + n_prev]
    o_ref, sfin_ref, s_scr = refs[-3:]
    t = pl.program_id(2)
    hk = q_ref.shape[-1]
    scale = hk ** -0.5
    c = chunk

    @pl.when(t == 0)
    def _():
        if has_state0:
            s_scr[...] = s0_ref[...]
        else:
            s_scr[...] = jnp.zeros_like(s_scr)

    bs = _chunk_cumsums(g_ref[...], n_chunks, c)
    row = lax.broadcasted_iota(jnp.int32, (c, c), 0)
    col = lax.broadcasted_iota(jnp.int32, (c, c), 1)
    row_blk = jnp.right_shift(row, GLA_SUB_SHIFT)
    col_blk = jnp.right_shift(col, GLA_SUB_SHIFT)
    below = col_blk < row_blk
    on_diag = (col_blk == row_blk) & (col <= row)

    def prepare(ci, single_ref):
        r = slice(ci * c, (ci + 1) * c)
        q = q_ref[r, :].astype(F32) * scale
        k = k_ref[r, :].astype(F32)
        b = bs[ci]
        b_end = b[c - 1:c, :]
        decay_col = jnp.transpose(jnp.broadcast_to(jnp.exp2(b_end), (LANES, hk)))[:, :1]
        k_dec = (k * jnp.exp2(b_end - b)).astype(BF16)
        if single_ref:
            q_up = (q * jnp.exp2(b - b_end)).astype(BF16)
            scores = lax.dot_general(q_up, k_dec, (((1,), (1,)), ((), ())), preferred_element_type=F32)
        else:
            scores = _gla_offdiag(q, k, b)
        return dict(q=q, k=k, b=b, q_dec=(q * jnp.exp2(b)).astype(BF16), k_dec=k_dec,
                    decay_col=decay_col, scores=scores)

    def finish(ci, o):
        r = slice(ci * c, (ci + 1) * c)
        ms = jnp.mean(o * o, axis=-1, keepdims=True)
        y = o * lax.rsqrt(ms + EPS) * gn_ref[...]
        gate = gate_ref[r, :].astype(F32)
        o_ref[r, :] = (y * (gate * (1.0 / (1.0 + jnp.exp(-gate))))).astype(o_ref.dtype)

    def run(single_ref):
        s = s_scr[...]
        cur = prepare(0, single_ref)
        prev_out = None
        prev_upd = None
        for ci in range(n_chunks):
            if prev_upd is not None:
                s = prev_upd[0] * s + prev_upd[1]
            o_inter = jnp.dot(cur["q_dec"], s.astype(BF16), preferred_element_type=F32)
            if prev_out is not None:
                finish(ci - 1, prev_out[0] + prev_out[1])
            if single_ref:
                a = jnp.where(col <= row, cur["scores"], 0.0).astype(BF16)
            else:
                a_diag = _gla_diag(cur["q"], cur["k"], cur["b"])
                a = jnp.where(below, cur["scores"], jnp.where(on_diag, a_diag, 0.0)).astype(BF16)
            v16 = v_ref[ci * c:(ci + 1) * c, :]
            prev_out = (jnp.dot(a, v16, preferred_element_type=F32), o_inter)
            prev_upd = (cur["decay_col"],
                        lax.dot_general(cur["k_dec"], v16, (((0,), (0,)), ((), ())), preferred_element_type=F32))
            if ci + 1 < n_chunks:
                cur = prepare(ci + 1, single_ref)
        s_scr[...] = prev_upd[0] * s + prev_upd[1]
        finish(n_chunks - 1, prev_out[0] + prev_out[1])

    step_rows = n_chunks * c
    first_row = pl.program_id(0) * (pl.num_programs(2) * step_rows) + t * step_rows
    flags_per_step = max(1, step_rows // rows_per_flag)
    n_mild = mild_ref[first_row // rows_per_flag, pl.program_id(1)]
    for i in range(1, flags_per_step):
        n_mild = n_mild + mild_ref[first_row // rows_per_flag + i, pl.program_id(1)]
    mild = n_mild == flags_per_step

    @pl.when(mild)
    def _():
        run(True)

    @pl.when(jnp.logical_not(mild))
    def _():
        run(False)

    @pl.when(t == pl.num_programs(2) - 1)
    def _():
        if stacked:
            for i, p_ref in enumerate(prev_refs):
                sfin_ref[i] = p_ref[...]
            sfin_ref[n_prev] = s_scr[...]
        else:
            sfin_ref[...] = s_scr[...]


def _gla(proj, gates, mild, rows_per_flag, g_norm, state0, layer, n_layers, prev_states, batch, seq, name="gla"):
    h = GLA_HEADS
    stacked = layer == n_layers - 1
    assert len(prev_states) == (layer if stacked else 0)
    dv = proj.shape[1] // 3
    dk = dv // 2
    hk, hv = dk // h, dv // h
    tb = _tile(seq, GLA_ROWS)
    assert rows_per_flag % tb == 0 or tb % rows_per_flag == 0
    proj3 = proj.reshape(batch, seq, proj.shape[1])
    gates3 = gates.reshape(batch, seq, dk)
    chunk = _tile(tb, GLA_CHUNK)
    in_specs = [
        pl.BlockSpec((None, tb, hk), lambda b, hh, t, *_: (b, t, hh)),
        pl.BlockSpec((None, tb, hk), lambda b, hh, t, *_: (b, t, h + hh)),
        pl.BlockSpec((None, tb, hv), lambda b, hh, t, *_: (b, t, 2 * dk // hv + hh)),
        pl.BlockSpec((None, tb, hv), lambda b, hh, t, *_: (b, t, (2 * dk + dv) // hv + hh)),
        pl.BlockSpec((None, tb, hk), lambda b, hh, t, *_: (b, t, hh)),
        pl.BlockSpec((1, hv), lambda b, hh, t, *_: (0, 0)),
    ]
    args = [proj3, proj3, proj3, proj3, gates3, g_norm.reshape(1, hv)]
    has_state0 = state0 is not None
    if has_state0:
        in_specs.append(pl.BlockSpec((None, None, None, hk, hv), lambda b, hh, t, *_: (layer, b, hh, 0, 0)))
        args.append(state0)
    one_state = pl.BlockSpec((None, None, hk, hv), lambda b, hh, t, *_: (b, hh, 0, 0))
    in_specs += [one_state] * len(prev_states)
    args += prev_states
    if stacked:
        state_spec = pl.BlockSpec((n_layers, None, None, hk, hv), lambda b, hh, t, *_: (0, b, hh, 0, 0))
        state_shape = jax.ShapeDtypeStruct((n_layers, batch, h, hk, hv), F32)
    else:
        state_spec = one_state
        state_shape = jax.ShapeDtypeStruct((batch, h, hk, hv), F32)
    og, states = pl.pallas_call(
        functools.partial(_gla_kernel, has_state0=has_state0, n_prev=len(prev_states), n_chunks=tb // chunk,
                          chunk=chunk, rows_per_flag=rows_per_flag, stacked=stacked),
        grid_spec=pltpu.PrefetchScalarGridSpec(
            num_scalar_prefetch=1,
            grid=(batch, h, seq // tb),
            in_specs=in_specs,
            out_specs=[pl.BlockSpec((None, tb, hv), lambda b, hh, t, *_: (b, t, hh)), state_spec],
            scratch_shapes=[pltpu.VMEM((hk, hv), F32)],
        ),
        out_shape=[jax.ShapeDtypeStruct((batch, seq, dv), BF16), state_shape],
        compiler_params=_params(("parallel", "parallel", "arbitrary")),
        name=name,
    )(mild, *args)
    return og.reshape(batch * seq, dv), states


def _attn_kernel(*refs, qb, past, hd, hg):
    if past:
        q_ref, k_ref, v_ref, kc_ref, vc_ref, bm_ref, o_ref = refs
    else:
        q_ref, k_ref, v_ref, bm_ref, o_ref = refs
        kc_ref = vc_ref = None
    tq = q_ref.shape[0]
    left = LEFT_CHUNKS * CHUNK
    wfull = bm_ref.shape[-1]
    blocks = [(hh, i) for hh in range(hg) for i in range(tq // qb)]

    def window(i):
        return max(0, i * qb + past - left), i * qb + past + qb

    def rows(new_ref, cache_ref, k0, k1, hh):
        cs = slice(hh * hd, (hh + 1) * hd)
        parts = []
        if k0 < past:
            parts.append(cache_ref[k0:min(k1, past), cs])
        if k1 > past:
            parts.append(new_ref[max(k0, past) - past:k1 - past, cs])
        return parts[0] if len(parts) == 1 else jnp.concatenate(parts, axis=0)

    def scores(hh, i):
        k0, k1 = window(i)
        s = lax.dot_general(q_ref[i * qb:(i + 1) * qb, hh * hd:(hh + 1) * hd], rows(k_ref, kc_ref, k0, k1, hh),
                            (((1,), (1,)), ((), ())), preferred_element_type=F32)
        return s + bm_ref[hh, :, wfull - (k1 - k0):]

    def store(hh, i, o, denom):
        o_ref[i * qb:(i + 1) * qb, hh * hd:(hh + 1) * hd] = (o * (1.0 / denom)).astype(o_ref.dtype)

    s_next = scores(*blocks[0])
    pending = None
    for n, (hh, i) in enumerate(blocks):
        s = s_next
        if n + 1 < len(blocks):
            s_next = scores(*blocks[n + 1])
        e = jnp.exp2(s - jnp.max(s, axis=-1, keepdims=True))
        k0, k1 = window(i)
        o = jnp.dot(e.astype(BF16), rows(v_ref, vc_ref, k0, k1, hh), preferred_element_type=F32)
        if pending is not None:
            store(*pending)
        pending = (hh, i, o, jnp.sum(e, axis=-1, keepdims=True))
    store(*pending)


def _attention(q, kv, bm, cache_k=None, cache_v=None, name="attn"):
    batch, tq, d = q.shape
    hd = d // ATT_HEADS
    hg = ATT_HEAD_GROUP if tq > bm.shape[1] else ATT_HEADS
    gw = hg * hd
    qb = bm.shape[1]
    past = 0 if cache_k is None else cache_k.shape[1]
    assert PAST_LEN % CHUNK == 0 and past in (0, min(LEFT_CHUNKS * CHUNK, PAST_LEN))
    in_specs = [
        pl.BlockSpec((None, tq, gw), lambda b, g: (b, 0, g)),
        pl.BlockSpec((None, tq, gw), lambda b, g: (b, 0, g)),
        pl.BlockSpec((None, tq, gw), lambda b, g: (b, 0, d // gw + g)),
    ]
    args = [q, kv, kv]
    if past:
        in_specs += [pl.BlockSpec((None, past, gw), lambda b, g: (b, 0, g))] * 2
        args += [cache_k, cache_v]
    in_specs.append(pl.BlockSpec((hg, qb, bm.shape[2]), lambda b, g: (g, 0, 0)))
    args.append(bm)
    return pl.pallas_call(
        functools.partial(_attn_kernel, qb=qb, past=past, hd=hd, hg=hg),
        grid=(batch, ATT_HEADS // hg),
        in_specs=in_specs,
        out_specs=pl.BlockSpec((None, tq, gw), lambda b, g: (b, 0, g)),
        out_shape=jax.ShapeDtypeStruct((batch, tq, d), BF16),
        compiler_params=_params(("parallel", "parallel")),
        name=name,
    )(*args)


def _bias_mask(table, qb):
    left = LEFT_CHUNKS * CHUNK
    h = table.shape[0]
    t = table.astype(F32) * LOG2E
    sat = left + 2 * qb
    ext = jnp.concatenate([jnp.broadcast_to(t[:, :1], (h, sat)), t, jnp.broadcast_to(t[:, -1:], (h, sat))], axis=1)
    top = left + qb - 1
    n_w = left + 2 * qb - 1
    start = ext.shape[1] - 1 - (top + MAX_REL + sat)
    desc = ext[:, ::-1][:, start:start + n_w]
    w = jnp.concatenate([desc[:, qb - 1:], desc[:, :qb - 1]], axis=1)
    bias = jnp.tile(w, (1, qb))[:, :qb * (n_w - 1)].reshape(h, qb, n_w - 1)[:, :, :left + qb]
    r = jnp.arange(qb)[:, None]
    c = jnp.arange(left + qb)[None, :]
    qc = r // CHUNK
    kc = c // CHUNK - LEFT_CHUNKS
    allowed = (kc <= qc) & (kc >= qc - LEFT_CHUNKS)
    return jnp.where(allowed[None], bias, NEG_INF)


def _trunk(x, state0, cache_k, cache_v, w, keep):
    batch, seq, d = x.shape
    depth = w["norm_mix"].shape[0]
    n_a = depth // 2
    x = x.reshape(batch * seq, d)
    states = []
    k_rows = None
    v_rows = None
    kv16 = None
    qb = _tile(seq, ATT_QBLOCK)
    if cache_k is not None:
        cache16 = (cache_k.astype(BF16).reshape(batch, -1, d), cache_v.astype(BF16).reshape(batch, -1, d))
    for layer in range(depth):
        if layer < n_a:
            proj, gates, gmin = _gla_in(x, w["norm_mix"][layer], w["gla_w_in"][layer], w["gla_w_lr"][layer],
                                        w["gla_w_gk"][layer], w["gla_b_gk"][layer], 3 * d)
            mild = (jnp.min(gmin.reshape(gmin.shape[0], GLA_HEADS, -1), axis=-1)
                    >= -GLA_SINGLE_REF_LOG2_RANGE).astype(jnp.int32)
            og, s_end = _gla(proj, gates, mild, x.shape[0] // gmin.shape[0], w["gla_g_norm"][layer],
                             state0, layer, n_a, states if layer == n_a - 1 else [], batch, seq)
            states = s_end if layer == n_a - 1 else states + [s_end]
            x = _matmul_res(og, w["gla_w_out"][layer], x, name="gla_out")
        else:
            j = layer - n_a
            if layer == n_a:
                (kv16,) = _norm_matmul(x, w["norm_kv"], [w["w_kv"]], BF16, name="kv_proj")
                if keep == seq:
                    row_tiles = None
                else:
                    assert seq % keep == 0
                    per_seq = seq // keep
                    row_tiles = (keep, batch, lambda i: i * per_seq + per_seq - 1)
                k_rows, v_rows = _norm_matmul(x, w["norm_kv"], [w["w_kv"], w["w_kv"]], F32,
                                              row_tiles=row_tiles, col_windows=(d, [0, d]), name="kv_rows")
            (q16,) = _norm_matmul(x, w["norm_mix"][layer], [w["att_w_q"][j]], BF16,
                                  out_scale=(d // ATT_HEADS) ** -0.5 * LOG2E, name="q_proj")
            q3 = q16.reshape(batch, seq, d)
            bm = _bias_mask(w["att_rel_bias"][j], qb)
            kv3 = kv16.reshape(batch, seq, 2 * d)
            if cache_k is None:
                o = _attention(q3, kv3, bm)
            else:
                o = _attention(q3, kv3, bm, *cache16)
            x = _matmul_res(o.reshape(batch * seq, d), w["att_w_out"][j], x, name="att_out")
        g_final = w["norm_final"] if layer == depth - 1 else None
        x = _mlp(x, w["norm_ffn"][layer], w["w_ff1"][layer], w["w_ff2"][layer], g_final)
    hd = d // ATT_HEADS
    k = k_rows.reshape(batch, keep, ATT_HEADS, hd)
    v = v_rows.reshape(batch, keep, ATT_HEADS, hd)
    return x.reshape(batch, seq, d), states, k, v


def kernel(x_prompt, x_sample, state_gla, cache_k, cache_v, norm_mix, norm_ffn, w_ff1, w_ff2, gla_w_in, gla_w_gk, gla_b_gk, gla_g_norm, gla_w_out, norm_kv, w_kv, att_w_q, att_rel_bias, att_w_out, norm_final):
    d = x_prompt.shape[-1]
    dk = d // 2
    n_main = 2 * dk + 2 * d
    rank = gla_w_in.shape[-1] - n_main
    pad = LANES - rank
    w = {
        "norm_mix": norm_mix, "norm_ffn": norm_ffn, "norm_kv": norm_kv, "norm_final": norm_final,
        "w_ff1": w_ff1.astype(BF16), "w_ff2": w_ff2.astype(BF16),
        "gla_w_in": gla_w_in.astype(BF16),
        "gla_w_lr": jnp.pad(gla_w_in[:, :, n_main:], ((0, 0), (0, 0), (0, pad))).astype(BF16),
        "gla_w_gk": jnp.pad(gla_w_gk, ((0, 0), (0, pad), (0, 0))).astype(BF16),
        "gla_b_gk": gla_b_gk, "gla_g_norm": gla_g_norm,
        "gla_w_out": gla_w_out.astype(BF16),
        "w_kv": w_kv.astype(BF16), "att_w_q": att_w_q.astype(BF16),
        "att_rel_bias": att_rel_bias, "att_w_out": att_w_out.astype(BF16),
    }
    seq = x_prompt.shape[1]
    keep = min(LEFT_CHUNKS * CHUNK, seq)
    y_p, s_p, k_p, v_p = _trunk(x_prompt, None, None, None, w, keep)
    y_s, s_s, k_s, v_s = _trunk(x_sample, state_gla, cache_k, cache_v, w, x_sample.shape[1])
    return (y_p, y_s, s_p, k_p, v_p, s_s, k_s, v_s)
```
